```python
import math
import jax, jax.numpy as jnp
from jax import lax
import numpy as np

D_MODEL = 1024
BATCH = 2
SEQ = 8192
DEPTH = 1

MIX_WIDTH = D_MODEL
WIDTH_A = MIX_WIDTH // 2
WIDTH_B = MIX_WIDTH - WIDTH_A
DIFF_HEAD_DIM = 64
DIFF_V_DIM = 2 * DIFF_HEAD_DIM
N_HEADS_A = WIDTH_A // DIFF_V_DIM
HEAD_DIM_B = 64
N_HEADS_B = WIDTH_B // HEAD_DIM_B
N_KV_B = 2
GQA_GROUP = N_HEADS_B // N_KV_B
GRID_W = 64
ROPE_THETA = 10000.0
ROT_HALF = HEAD_DIM_B // 2
Q_BLOCK = 128
NORM_EPS = 1e-6
PROJ_SIZES = (
    N_HEADS_A * 2 * DIFF_HEAD_DIM,
    N_HEADS_A * 2 * DIFF_HEAD_DIM,
    WIDTH_A,
    WIDTH_A,
    N_HEADS_B * HEAD_DIM_B,
    N_KV_B * HEAD_DIM_B,
    N_KV_B * HEAD_DIM_B,
    WIDTH_B,
)
PROJ_OUT = sum(PROJ_SIZES)
PROJ_SPLITS = tuple(int(v) for v in np.cumsum(PROJ_SIZES)[:-1])

kernel_name = "hybrid_diffattn_gqa_axialrope_adaln"


def rmsnorm(x, w):
    xf = x.astype(jnp.float32)
    y = xf * lax.rsqrt(jnp.mean(xf * xf, axis=-1, keepdims=True) + NORM_EPS)
    return (y * w.astype(jnp.float32)).astype(x.dtype)


def lambda_init_for(layer_idx):
    return 0.8 - 0.6 * math.exp(-0.3 * layer_idx)


def alibi_slopes(n_heads):
    s = 2.0 ** (-8.0 * np.arange(1, n_heads + 1) / n_heads)
    return jnp.asarray(s, dtype=jnp.float32)


def axial_rope_tables(rows):
    row = jnp.repeat(jnp.arange(rows), GRID_W).astype(jnp.float32)
    col = jnp.tile(jnp.arange(GRID_W), rows).astype(jnp.float32)
    freqs = 1.0 / (ROPE_THETA ** (jnp.arange(ROT_HALF // 2, dtype=jnp.float32) * 2.0 / ROT_HALF))
    ang_r = row[:, None] * freqs[None, :]
    ang_c = col[:, None] * freqs[None, :]
    return jnp.cos(ang_r), jnp.sin(ang_r), jnp.cos(ang_c), jnp.sin(ang_c)


def rotate(x, cos, sin):
    x1, x2 = jnp.split(x, 2, axis=-1)
    cos = cos.astype(x.dtype)
    sin = sin.astype(x.dtype)
    return jnp.concatenate([x1 * cos - x2 * sin, x2 * cos + x1 * sin], axis=-1)


def axial_rope(x, tables):
    cos_r, sin_r, cos_c, sin_c = tables
    xr, xc = jnp.split(x, 2, axis=-1)
    return jnp.concatenate([rotate(xr, cos_r, sin_r), rotate(xc, cos_c, sin_c)], axis=-1)


def hybrid_layer(x, c, layer_idx, rope_tables, slopes, w_ada, b_ada, norm_w, w_in,
                 lq1, lk1, lq2, lk2, subln_w, q_norm_w, k_norm_w, w_out):
    bsz, seq, _ = x.shape
    nblk = seq // Q_BLOCK
    lam_init = lambda_init_for(layer_idx)

    mod = jax.nn.silu(c) @ w_ada + b_ada
    shift, scale, gate = jnp.split(mod, 3, axis=-1)
    h = rmsnorm(x, norm_w) * (1.0 + scale[:, None, :]) + shift[:, None, :]

    proj = h @ w_in
    qa, ka, va, ga, qb, kb, vb, gb = jnp.split(proj, PROJ_SPLITS, axis=-1)

    qa = qa.reshape(bsz, seq, N_HEADS_A, 2, DIFF_HEAD_DIM).transpose(0, 2, 3, 1, 4)
    ka = ka.reshape(bsz, seq, N_HEADS_A, 2, DIFF_HEAD_DIM).transpose(0, 2, 3, 1, 4)
    va = va.reshape(bsz, seq, N_HEADS_A, DIFF_V_DIM).transpose(0, 2, 1, 3)
    lam = (jnp.exp(jnp.sum(lq1.astype(jnp.float32) * lk1.astype(jnp.float32)))
           - jnp.exp(jnp.sum(lq2.astype(jnp.float32) * lk2.astype(jnp.float32)))
           + lam_init)

    qb = qb.reshape(bsz, seq, N_KV_B, GQA_GROUP, HEAD_DIM_B).transpose(0, 2, 3, 1, 4)
    kb = kb.reshape(bsz, seq, N_KV_B, HEAD_DIM_B).transpose(0, 2, 1, 3)
    vb = vb.reshape(bsz, seq, N_KV_B, HEAD_DIM_B).transpose(0, 2, 1, 3)
    qb = axial_rope(rmsnorm(qb, q_norm_w), rope_tables)
    kb = axial_rope(rmsnorm(kb, k_norm_w), rope_tables)

    qa_blk = jnp.moveaxis(qa.reshape(bsz, N_HEADS_A, 2, nblk, Q_BLOCK, DIFF_HEAD_DIM), 3, 0)
    qb_blk = jnp.moveaxis(qb.reshape(bsz, N_KV_B, GQA_GROUP, nblk, Q_BLOCK, HEAD_DIM_B), 3, 0)
    kpos = jnp.arange(seq, dtype=jnp.float32)
    scale_a = 1.0 / math.sqrt(DIFF_HEAD_DIM)
    scale_b = 1.0 / math.sqrt(HEAD_DIM_B)

    def query_block(args):
        qa_i, qb_i, i = args
        qpos = (i * Q_BLOCK + jnp.arange(Q_BLOCK)).astype(jnp.float32)
        dist = jnp.abs(qpos[:, None] - kpos[None, :])
        sa = (jnp.einsum('bhmqd,bhmkd->bhmqk', qa_i, ka).astype(jnp.float32) * scale_a
              - slopes[:, None, None, None] * dist)
        pa = jax.nn.softmax(sa, axis=-1)
        diff = pa[:, :, 0] - lam * pa[:, :, 1]
        oa = jnp.einsum('bhqk,bhkd->bhqd', diff.astype(va.dtype), va)
        sb = jnp.einsum('bgrqd,bgkd->bgrqk', qb_i, kb).astype(jnp.float32) * scale_b
        pb = jax.nn.softmax(sb, axis=-1)
        ob = jnp.einsum('bgrqk,bgkd->bgrqd', pb.astype(vb.dtype), vb)
        return oa, ob

    oa, ob = lax.map(query_block, (qa_blk, qb_blk, jnp.arange(nblk)))
    oa = oa.transpose(1, 0, 3, 2, 4).reshape(bsz, seq, N_HEADS_A, DIFF_V_DIM)
    oa = (rmsnorm(oa, subln_w) * (1.0 - lam_init)).reshape(bsz, seq, WIDTH_A)
    ob = ob.transpose(1, 0, 4, 2, 3, 5).reshape(bsz, seq, WIDTH_B)

    y = jnp.concatenate([oa * jax.nn.silu(ga), ob * jax.nn.silu(gb)], axis=-1) @ w_out
    return x + gate[:, None, :] * y


def setup_inputs(seed: int = 0) -> dict:
    key = jax.random.key(seed)
    ks = jax.random.split(key, 16)
    f32 = jnp.float32
    D = D_MODEL
    nrm = lambda k, shape, s: jax.random.normal(k, shape, f32) * s
    return {
        "x": nrm(ks[0], (BATCH, SEQ, D), 1.0),
        "c": nrm(ks[1], (BATCH, D), 1.0),
        "w_ada": nrm(ks[2], (DEPTH, D, 3 * D), 0.5 * D ** -0.5),
        "b_ada": nrm(ks[3], (DEPTH, 3 * D), 0.02),
        "norm_w": 1.0 + nrm(ks[4], (DEPTH, D), 0.02),
        "w_in": nrm(ks[5], (DEPTH, D, PROJ_OUT), D ** -0.5),
        "lambda_q1": nrm(ks[6], (DEPTH, DIFF_HEAD_DIM), 0.1),
        "lambda_k1": nrm(ks[7], (DEPTH, DIFF_HEAD_DIM), 0.1),
        "lambda_q2": nrm(ks[8], (DEPTH, DIFF_HEAD_DIM), 0.1),
        "lambda_k2": nrm(ks[9], (DEPTH, DIFF_HEAD_DIM), 0.1),
        "subln_w": 1.0 + nrm(ks[10], (DEPTH, DIFF_V_DIM), 0.02),
        "q_norm_w": 1.0 + nrm(ks[11], (DEPTH, HEAD_DIM_B), 0.02),
        "k_norm_w": 1.0 + nrm(ks[12], (DEPTH, HEAD_DIM_B), 0.02),
        "w_out": nrm(ks[13], (DEPTH, MIX_WIDTH, D), MIX_WIDTH ** -0.5),
        "final_norm_w": 1.0 + nrm(ks[14], (D,), 0.02),
    }


def reference(x, c, w_ada, b_ada, norm_w, w_in, lambda_q1, lambda_k1, lambda_q2, lambda_k2,
              subln_w, q_norm_w, k_norm_w, w_out, final_norm_w):
    seq = x.shape[1]
    rows = seq // GRID_W
    rope_tables = axial_rope_tables(rows)
    slopes = alibi_slopes(N_HEADS_A)
    for l in range(DEPTH):
        x = hybrid_layer(x, c, l, rope_tables, slopes, w_ada[l], b_ada[l], norm_w[l], w_in[l],
                         lambda_q1[l], lambda_k1[l], lambda_q2[l], lambda_k2[l], subln_w[l],
                         q_norm_w[l], k_norm_w[l], w_out[l])
    return rmsnorm(x, final_norm_w)
```

```python
import functools
import math

import numpy as np
import jax
import jax.numpy as jnp
from jax import lax
from jax.experimental import pallas as pl
from jax.experimental.pallas import tpu as pltpu

D_MODEL = 1024
N_HEADS_A = 4
DIFF_HEAD_DIM = 64
DIFF_V_DIM = 128
WIDTH_A = 512
N_KV_B = 2
GQA_GROUP = 4
HEAD_DIM_B = 64
WIDTH_B = 512
GRID_W = 64
ROPE_THETA = 10000.0
ROT_HALF = 32
NORM_EPS = 1e-6
LOG2E = 1.4426950408889634
NEG_BIG = -1e30

VMEM_LIMIT_BYTES = 56 * 1024 * 1024

PROJ_TILE = 512
TQ_A = 512
TK_A = 512
TQ_B = 256
TK_B = 512


def _silu(v):
    return v * (1.0 / (1.0 + jnp.exp(-v)))


def _mod_kernel(c_ref, w_ref, b_ref, o_ref):
    c = c_ref[...]
    o_ref[...] = jnp.dot(_silu(c), w_ref[...], preferred_element_type=jnp.float32,
                         precision=lax.Precision.HIGHEST) + b_ref[...]


def _adaln_mod(c, w_ada, b_ada):
    bsz, d = c.shape
    n = w_ada.shape[1]
    tn = 512
    return pl.pallas_call(
        _mod_kernel,
        grid=(n // tn,),
        in_specs=[pl.BlockSpec((bsz, d), lambda j: (0, 0)),
                  pl.BlockSpec((d, tn), lambda j: (0, j)),
                  pl.BlockSpec((1, tn), lambda j: (0, j))],
        out_specs=pl.BlockSpec((bsz, tn), lambda j: (0, j)),
        out_shape=jax.ShapeDtypeStruct((bsz, n), jnp.float32),
        name="adaln_mod",
    )(c, w_ada, b_ada.reshape(1, n))


def _in_proj_kernel(x_ref, mod_ref, nw_ref, wtok_ref, wfeat_ref, qnw_ref, knw_ref, knws_ref,
                    cosq_ref, sinq_ref, cosk_ref, sink_ref,
                    ka_ref, kb_ref, qa_ref, va_ref, ga_ref, qb_ref, vb_ref, gb_ref):
    x = x_ref[...]
    shift = mod_ref[0:1, :]
    scale = mod_ref[1:2, :]
    ms = jnp.mean(x * x, axis=1, keepdims=True)
    h = (x * lax.rsqrt(ms + NORM_EPS)) * nw_ref[...] * (1.0 + scale) + shift
    hb = h.astype(jnp.bfloat16)

    tok = jnp.dot(hb, wtok_ref[...], preferred_element_type=jnp.float32)
    ka_ref[...] = tok[:, 0:WIDTH_A].astype(jnp.bfloat16)
    kb = tok[:, 512:640]
    kbs = tok[:, 640:768]
    sq = kb * kb
    r0 = lax.rsqrt(jnp.sum(sq[:, 0:64], axis=1, keepdims=True) * (1.0 / HEAD_DIM_B) + NORM_EPS)
    r1 = lax.rsqrt(jnp.sum(sq[:, 64:128], axis=1, keepdims=True) * (1.0 / HEAD_DIM_B) + NORM_EPS)
    lane = lax.broadcasted_iota(jnp.int32, kb.shape, 1)
    r = jnp.where(lane < HEAD_DIM_B, r0, r1)
    kr = r * ((kb * knw_ref[...]) * cosk_ref[...] + (kbs * knws_ref[...]) * sink_ref[...])
    kb_ref[...] = kr.astype(jnp.bfloat16)

    def feat(lo, hi):
        return lax.dot_general(wfeat_ref[lo:hi, :], hb, (((1,), (1,)), ((), ())),
                               preferred_element_type=jnp.float32)

    qa_ref[...] = (feat(0, 512) * (LOG2E / math.sqrt(DIFF_HEAD_DIM))).astype(jnp.bfloat16)
    va_ref[...] = feat(512, 1024).astype(jnp.bfloat16)
    ga_ref[...] = _silu(feat(1024, 1536)).astype(jnp.bfloat16)
    qb = feat(1536, 2048)
    cq = cosq_ref[...]
    sq_ = sinq_ref[...]
    qnw = qnw_ref[...]
    for hd in range(N_KV_B * GQA_GROUP):
        q = qb[hd * 64:(hd + 1) * 64, :]
        rq = lax.rsqrt(jnp.mean(q * q, axis=0, keepdims=True) + NORM_EPS)
        qn = q * rq * qnw
        partner = jnp.concatenate([qn[16:32], qn[0:16], qn[48:64], qn[32:48]], axis=0)
        rot = qn * cq + partner * sq_
        qb_ref[hd * 64:(hd + 1) * 64, :] = (rot * (LOG2E / math.sqrt(HEAD_DIM_B))).astype(jnp.bfloat16)
    vb_ref[...] = feat(2048, 2176).astype(jnp.bfloat16)
    gb_ref[...] = _silu(feat(2176, 2688)).astype(jnp.bfloat16)


def _in_proj(x, mod3, norm_w, w_tok, w_feat_t, qnw_col, knw_row, knws_row, cosq, sinq, cosk, sink):
    bsz, seq, d = x.shape
    ts = PROJ_TILE
    bf = jnp.bfloat16
    const = lambda b, i: (0, 0)
    feat_spec = lambda rows: pl.BlockSpec((None, rows, ts), lambda b, i: (b, 0, i))
    return pl.pallas_call(
        _in_proj_kernel,
        grid=(bsz, seq // ts),
        in_specs=[
            pl.BlockSpec((None, ts, d), lambda b, i: (b, i, 0)),
            pl.BlockSpec((None, 3, d), lambda b, i: (b, 0, 0)),
            pl.BlockSpec((1, d), const),
            pl.BlockSpec(w_tok.shape, const),
            pl.BlockSpec(w_feat_t.shape, const),
            pl.BlockSpec((HEAD_DIM_B, 1), const),
            pl.BlockSpec((1, 128), const),
            pl.BlockSpec((1, 128), const),
            pl.BlockSpec((HEAD_DIM_B, ts), lambda b, i: (0, i)),
            pl.BlockSpec((HEAD_DIM_B, ts), lambda b, i: (0, i)),
            pl.BlockSpec((ts, 128), lambda b, i: (i, 0)),
            pl.BlockSpec((ts, 128), lambda b, i: (i, 0)),
        ],
        out_specs=[
            pl.BlockSpec((None, ts, WIDTH_A), lambda b, i: (b, i, 0)),
            pl.BlockSpec((None, ts, 128), lambda b, i: (b, i, 0)),
            feat_spec(512), feat_spec(512), feat_spec(512), feat_spec(512), feat_spec(128),
            feat_spec(512),
        ],
        out_shape=[
            jax.ShapeDtypeStruct((bsz, seq, WIDTH_A), bf),
            jax.ShapeDtypeStruct((bsz, seq, 128), bf),
            jax.ShapeDtypeStruct((bsz, 512, seq), bf),
            jax.ShapeDtypeStruct((bsz, 512, seq), bf),
            jax.ShapeDtypeStruct((bsz, 512, seq), bf),
            jax.ShapeDtypeStruct((bsz, 512, seq), bf),
            jax.ShapeDtypeStruct((bsz, 128, seq), bf),
            jax.ShapeDtypeStruct((bsz, 512, seq), bf),
        ],
        compiler_params=pltpu.CompilerParams(
            dimension_semantics=("arbitrary", "arbitrary"), vmem_limit_bytes=VMEM_LIMIT_BYTES),
        name="in_proj",
    )(x, mod3, norm_w, w_tok, w_feat_t, qnw_col, knw_row, knws_row, cosq, sinq, cosk, sink)


def _softmax_tile(u, off, vt, m_ref, l_ref, acc_ref):
    m_old = m_ref[...]
    m_new = jnp.maximum(m_old, jnp.max(u, axis=0, keepdims=True) + off)
    alpha = jnp.exp2(m_old - m_new)
    p = jnp.exp2(u - (m_new - off))
    l_ref[...] = alpha * l_ref[...] + jnp.sum(p, axis=0, keepdims=True)
    acc_ref[...] = alpha * acc_ref[...] + jnp.dot(vt, p.astype(jnp.bfloat16),
                                                  preferred_element_type=jnp.float32)
    m_ref[...] = m_new


def _attn_diff_kernel(qt_ref, k_ref, vt_ref, gt_ref, colb_ref, diagb_ref, slope_ref, sublnw_ref,
                      lq1_ref, lk1_ref, lq2_ref, lk2_ref, o_ref,
                      qbd_ref, m_ref, l_ref, acc_ref, *, lam_init):
    tq, tk = TQ_A, TK_A
    hd = pl.program_id(1)
    qi = pl.program_id(2)
    n_kv = k_ref.shape[0] // tk
    sigma = slope_ref[hd] * LOG2E

    zeros = jnp.zeros((DIFF_HEAD_DIM, tq), jnp.bfloat16)
    qbd_ref[0:64, 0:tq] = qt_ref[0:64, :]
    qbd_ref[64:128, 0:tq] = zeros
    qbd_ref[0:64, tq:2 * tq] = zeros
    qbd_ref[64:128, tq:2 * tq] = qt_ref[64:128, :]
    m_ref[...] = jnp.full(m_ref.shape, NEG_BIG, jnp.float32)
    l_ref[...] = jnp.zeros(l_ref.shape, jnp.float32)
    acc_ref[...] = jnp.zeros(acc_ref.shape, jnp.float32)

    il = lax.broadcasted_iota(jnp.int32, (1, 2 * tq), 1)
    il = jnp.where(il >= tq, il - tq, il).astype(jnp.float32)

    def scores(j):
        j0 = pl.multiple_of(j * tk, tk)
        t = jnp.dot(k_ref[pl.ds(j0, tk), :], qbd_ref[...], preferred_element_type=jnp.float32)
        return t, vt_ref[:, pl.ds(j0, tk)]

    def left(j, carry):
        t, vt = scores(j)
        off = sigma * ((j - qi) * tk).astype(jnp.float32)
        _softmax_tile(t + colb_ref[...], off, vt, m_ref, l_ref, acc_ref)
        return carry

    def right(j, carry):
        t, vt = scores(j)
        off = 2.0 * sigma * il - sigma * ((j - qi) * tk).astype(jnp.float32)
        _softmax_tile(t - colb_ref[...], off, vt, m_ref, l_ref, acc_ref)
        return carry

    lax.fori_loop(0, qi, left, 0)
    t, vt = scores(qi)
    _softmax_tile(t + diagb_ref[...], 0.0, vt, m_ref, l_ref, acc_ref)
    lax.fori_loop(qi + 1, n_kv, right, 0)

    lam = (jnp.exp(jnp.sum(lq1_ref[...] * lk1_ref[...], axis=1, keepdims=True))
           - jnp.exp(jnp.sum(lq2_ref[...] * lk2_ref[...], axis=1, keepdims=True)) + lam_init)
    inv_l = 1.0 / l_ref[...]
    o1 = acc_ref[:, 0:tq] * inv_l[:, 0:tq]
    o2 = acc_ref[:, tq:2 * tq] * inv_l[:, tq:2 * tq]
    diff = o1 - lam * o2
    ms = jnp.mean(diff * diff, axis=0, keepdims=True)
    y = diff * lax.rsqrt(ms + NORM_EPS) * sublnw_ref[...] * (1.0 - lam_init)
    gated = y * gt_ref[...].astype(jnp.float32)
    o_ref[...] = gated.T.astype(o_ref.dtype)


def _attn_diff(qa_t, ka, va_t, ga_t, colb, diagb, slopes, subln_col, lq1, lk1, lq2, lk2, lam_init):
    bsz, seq, _ = ka.shape
    tq = TQ_A
    vec = pl.BlockSpec((1, DIFF_HEAD_DIM), lambda b, h, i: (0, 0))
    return pl.pallas_call(
        functools.partial(_attn_diff_kernel, lam_init=lam_init),
        grid=(bsz, N_HEADS_A, seq // tq),
        in_specs=[
            pl.BlockSpec((None, 128, tq), lambda b, h, i: (b, h, i)),
            pl.BlockSpec((None, seq, 128), lambda b, h, i: (b, 0, h)),
            pl.BlockSpec((None, 128, seq), lambda b, h, i: (b, h, 0)),
            pl.BlockSpec((None, 128, tq), lambda b, h, i: (b, h, i)),
            pl.BlockSpec((None, TK_A, 2 * tq), lambda b, h, i: (h, 0, 0)),
            pl.BlockSpec((None, TK_A, 2 * tq), lambda b, h, i: (h, 0, 0)),
            pl.BlockSpec(memory_space=pltpu.SMEM),
            pl.BlockSpec((DIFF_V_DIM, 1), lambda b, h, i: (0, 0)),
            vec, vec, vec, vec,
        ],
        out_specs=pl.BlockSpec((None, tq, 128), lambda b, h, i: (b, i, h)),
        out_shape=jax.ShapeDtypeStruct((bsz, seq, WIDTH_A), jnp.bfloat16),
        scratch_shapes=[
            pltpu.VMEM((128, 2 * tq), jnp.bfloat16),
            pltpu.VMEM((1, 2 * tq), jnp.float32),
            pltpu.VMEM((1, 2 * tq), jnp.float32),
            pltpu.VMEM((DIFF_V_DIM, 2 * tq), jnp.float32),
        ],
        compiler_params=pltpu.CompilerParams(
            dimension_semantics=("arbitrary", "arbitrary", "arbitrary"),
            vmem_limit_bytes=VMEM_LIMIT_BYTES),
        name="attn_diff",
    )(qa_t, ka, va_t, ga_t, colb, diagb, slopes, subln_col, lq1, lk1, lq2, lk2)


def _attn_gqa_kernel(qt_ref, k_ref, vt_ref, gt_ref, o_ref, qp_ref, m_ref, l_ref, acc_ref):
    tq, tk = TQ_B, TK_B
    g = pl.program_id(1)
    n_kv = k_ref.shape[0] // tk

    row = lax.broadcasted_iota(jnp.int32, (128, tq), 0)
    lo = g * HEAD_DIM_B
    mine = (row >= lo) & (row < lo + HEAD_DIM_B)
    for r in range(GQA_GROUP):
        q = qt_ref[r * 64:(r + 1) * 64, :].astype(jnp.float32)
        q2 = jnp.concatenate([q, q], axis=0)
        qp_ref[:, r * tq:(r + 1) * tq] = jnp.where(mine, q2, 0.0).astype(jnp.bfloat16)
    m_ref[...] = jnp.full(m_ref.shape, NEG_BIG, jnp.float32)
    l_ref[...] = jnp.zeros(l_ref.shape, jnp.float32)
    acc_ref[...] = jnp.zeros(acc_ref.shape, jnp.float32)

    def body(j, carry):
        j0 = pl.multiple_of(j * tk, tk)
        t = jnp.dot(k_ref[pl.ds(j0, tk), :], qp_ref[...], preferred_element_type=jnp.float32)
        _softmax_tile(t, 0.0, vt_ref[:, pl.ds(j0, tk)], m_ref, l_ref, acc_ref)
        return carry

    lax.fori_loop(0, n_kv, body, 0)

    o = acc_ref[...] * (1.0 / l_ref[...])
    o = jnp.concatenate([o[:, r * tq:(r + 1) * tq] for r in range(GQA_GROUP)], axis=0)
    gated = o * gt_ref[...].astype(jnp.float32)
    o_ref[...] = gated.T.astype(o_ref.dtype)


def _attn_gqa(qb_t, kb, vb_t, gb_t):
    bsz, seq, _ = kb.shape
    tq = TQ_B
    return pl.pallas_call(
        _attn_gqa_kernel,
        grid=(bsz, N_KV_B, seq // tq),
        in_specs=[
            pl.BlockSpec((None, 256, tq), lambda b, g, i: (b, g, i)),
            pl.BlockSpec((None, seq, 128), lambda b, g, i: (b, 0, 0)),
            pl.BlockSpec((None, HEAD_DIM_B, seq), lambda b, g, i: (b, g, 0)),
            pl.BlockSpec((None, 256, tq), lambda b, g, i: (b, g, i)),
        ],
        out_specs=pl.BlockSpec((None, tq, 256), lambda b, g, i: (b, i, g)),
        out_shape=jax.ShapeDtypeStruct((bsz, seq, WIDTH_B), jnp.bfloat16),
        scratch_shapes=[
            pltpu.VMEM((128, GQA_GROUP * tq), jnp.bfloat16),
            pltpu.VMEM((1, GQA_GROUP * tq), jnp.float32),
            pltpu.VMEM((1, GQA_GROUP * tq), jnp.float32),
            pltpu.VMEM((HEAD_DIM_B, GQA_GROUP * tq), jnp.float32),
        ],
        compiler_params=pltpu.CompilerParams(
            dimension_semantics=("arbitrary", "arbitrary", "arbitrary"),
            vmem_limit_bytes=VMEM_LIMIT_BYTES),
        name="attn_gqa",
    )(qb_t, kb, vb_t, gb_t)


def _out_proj_kernel(ga_ref, gb_ref, wa_ref, wb_ref, x_ref, mod_ref, fw_ref, o_ref):
    y = jnp.dot(ga_ref[...], wa_ref[...], preferred_element_type=jnp.float32)
    y = y + jnp.dot(gb_ref[...], wb_ref[...], preferred_element_type=jnp.float32)
    z = x_ref[...] + mod_ref[2:3, :] * y
    ms = jnp.mean(z * z, axis=1, keepdims=True)
    o_ref[...] = z * lax.rsqrt(ms + NORM_EPS) * fw_ref[...]


def _out_proj(ga, gb, w_a, w_b, x, mod3, final_w):
    bsz, seq, d = x.shape
    ts = PROJ_TILE
    const = lambda b, i: (0, 0)
    return pl.pallas_call(
        _out_proj_kernel,
        grid=(bsz, seq // ts),
        in_specs=[
            pl.BlockSpec((None, ts, WIDTH_A), lambda b, i: (b, i, 0)),
            pl.BlockSpec((None, ts, WIDTH_B), lambda b, i: (b, i, 0)),
            pl.BlockSpec(w_a.shape, const),
            pl.BlockSpec(w_b.shape, const),
            pl.BlockSpec((None, ts, d), lambda b, i: (b, i, 0)),
            pl.BlockSpec((None, 3, d), lambda b, i: (b, 0, 0)),
            pl.BlockSpec((1, d), const),
        ],
        out_specs=pl.BlockSpec((None, ts, d), lambda b, i: (b, i, 0)),
        out_shape=jax.ShapeDtypeStruct((bsz, seq, d), jnp.float32),
        compiler_params=pltpu.CompilerParams(
            dimension_semantics=("arbitrary", "arbitrary"), vmem_limit_bytes=VMEM_LIMIT_BYTES),
        name="out_proj",
    )(ga, gb, w_a, w_b, x, mod3, final_w)


def _rope_tables(seq):
    pos = np.arange(seq)
    row = (pos // GRID_W).astype(np.float32)
    col = (pos % GRID_W).astype(np.float32)
    n_freq = ROT_HALF // 2
    freqs = (1.0 / (ROPE_THETA ** (np.arange(n_freq, dtype=np.float32) * 2.0 / ROT_HALF))).astype(np.float32)
    row, col, freqs = jnp.asarray(row), jnp.asarray(col), jnp.asarray(freqs)
    ang_r = row[:, None] * freqs[None, :]
    ang_c = col[:, None] * freqs[None, :]
    cos = jnp.concatenate([jnp.cos(ang_r), jnp.cos(ang_r), jnp.cos(ang_c), jnp.cos(ang_c)], axis=1)
    sin = jnp.concatenate([-jnp.sin(ang_r), jnp.sin(ang_r), -jnp.sin(ang_c), jnp.sin(ang_c)], axis=1)
    d = np.arange(HEAD_DIM_B)
    partner = np.where((d // n_freq) % 2 == 0, d + n_freq, d - n_freq)
    return cos, sin, partner


def _alibi_tables(slopes):
    sig = (slopes * LOG2E)[:, None, None]
    jl = jnp.arange(TK_A, dtype=jnp.float32)[None, :, None]
    il = jnp.tile(jnp.arange(TQ_A, dtype=jnp.float32), 2)[None, None, :]
    colb = jnp.broadcast_to(sig * jl, (N_HEADS_A, TK_A, 2 * TQ_A))
    diagb = sig * (il - jnp.abs(il - jl))
    return colb, diagb


def kernel(x, c, w_ada, b_ada, norm_w, w_in, lambda_q1, lambda_k1, lambda_q2, lambda_k2,
           subln_w, q_norm_w, k_norm_w, w_out, final_norm_w):
    assert w_ada.shape[0] == 1, "single-layer problem: the final norm is fused into the output projection"
    bsz, seq, d = x.shape
    assert seq % TQ_A == 0 and seq % TK_B == 0 and seq % PROJ_TILE == 0
    bf = jnp.bfloat16
    cos, sin, partner = _rope_tables(seq)
    cosq, sinq = cos.T, sin.T
    cosk, sink = jnp.tile(cos, (1, N_KV_B)), jnp.tile(sin, (1, N_KV_B))
    slopes = jnp.asarray(2.0 ** (-8.0 * np.arange(1, N_HEADS_A + 1) / N_HEADS_A), jnp.float32)
    colb, diagb = _alibi_tables(slopes)
    lam_init = 0.8 - 0.6 * math.exp(-0.3 * 0)

    w = w_in[0]
    q_a, k_a, v_a, g_a = w[:, 0:512], w[:, 512:1024], w[:, 1024:1536], w[:, 1536:2048]
    q_b, k_b, v_b, g_b = w[:, 2048:2560], w[:, 2560:2688], w[:, 2688:2816], w[:, 2816:3328]
    partner2 = np.concatenate([partner, partner + HEAD_DIM_B])
    w_tok = jnp.concatenate([k_a, k_b, k_b[:, partner2]], axis=1).astype(bf)
    w_feat_t = jnp.concatenate([q_a, v_a, g_a, q_b, v_b, g_b], axis=1).T.astype(bf)
    knw = jnp.tile(k_norm_w[0], N_KV_B)
    wo = w_out[0].astype(bf)

    mod3 = _adaln_mod(c, w_ada[0], b_ada[0]).reshape(bsz, 3, d)
    ka, kb, qa_t, va_t, ga_t, qb_t, vb_t, gb_t = _in_proj(
        x, mod3, norm_w[0].reshape(1, d), w_tok, w_feat_t,
        q_norm_w[0].reshape(HEAD_DIM_B, 1), knw.reshape(1, 128), knw[partner2].reshape(1, 128),
        cosq, sinq, cosk, sink)
    oa = _attn_diff(qa_t, ka, va_t, ga_t, colb, diagb, slopes,
                    subln_w[0].reshape(DIFF_V_DIM, 1),
                    lambda_q1[0].reshape(1, -1), lambda_k1[0].reshape(1, -1),
                    lambda_q2[0].reshape(1, -1), lambda_k2[0].reshape(1, -1), lam_init)
    ob = _attn_gqa(qb_t, kb, vb_t, gb_t)
    return _out_proj(oa, ob, wo[:WIDTH_A], wo[WIDTH_A:], x, mod3, final_norm_w.reshape(1, d))
```

```python
import functools
import math

import numpy as np
import jax
import jax.numpy as jnp
from jax import lax
from jax.experimental import pallas as pl
from jax.experimental.pallas import tpu as pltpu

D_MODEL = 1024
N_HEADS_A = 4
DIFF_HEAD_DIM = 64
DIFF_V_DIM = 128
WIDTH_A = 512
N_KV_B = 2
GQA_GROUP = 4
HEAD_DIM_B = 64
WIDTH_B = 512
GRID_W = 64
ROPE_THETA = 10000.0
ROT_HALF = 32
NORM_EPS = 1e-6
LOG2E = 1.4426950408889634
NEG_BIG = -1e30

VMEM_LIMIT_BYTES = 56 * 1024 * 1024

PROJ_TILE = 512
TQ_A = 512
TK_A = 512
TQ_B = 256
TK_B = 512


def _silu(v):
    return v * (1.0 / (1.0 + jnp.exp(-v)))


def _mod_kernel(c_ref, w_ref, b_ref, o_ref):
    c = c_ref[...]
    o_ref[...] = jnp.dot(_silu(c), w_ref[...], preferred_element_type=jnp.float32,
                         precision=lax.Precision.HIGHEST) + b_ref[...]


def _adaln_mod(c, w_ada, b_ada):
    bsz, d = c.shape
    n = w_ada.shape[1]
    tn = 512
    return pl.pallas_call(
        _mod_kernel,
        grid=(n // tn,),
        in_specs=[pl.BlockSpec((bsz, d), lambda j: (0, 0)),
                  pl.BlockSpec((d, tn), lambda j: (0, j)),
                  pl.BlockSpec((1, tn), lambda j: (0, j))],
        out_specs=pl.BlockSpec((bsz, tn), lambda j: (0, j)),
        out_shape=jax.ShapeDtypeStruct((bsz, n), jnp.float32),
        name="adaln_mod",
    )(c, w_ada, b_ada.reshape(1, n))


def _in_proj_kernel(x_ref, mod_ref, nw_ref, wtok_ref, wfeat_ref, qnw_ref, knw_ref, knws_ref,
                    cosq_ref, sinq_ref, cosk_ref, sink_ref,
                    ka_ref, kb_ref, qa_ref, va_ref, ga_ref, qb_ref, vb_ref, gb_ref):
    x = x_ref[...]
    shift = mod_ref[0:1, :]
    scale = mod_ref[1:2, :]
    ms = jnp.mean(x * x, axis=1, keepdims=True)
    h = (x * lax.rsqrt(ms + NORM_EPS)) * nw_ref[...] * (1.0 + scale) + shift
    hb = h.astype(jnp.bfloat16)

    tok = jnp.dot(hb, wtok_ref[...], preferred_element_type=jnp.float32)
    ka_ref[...] = tok[:, 0:WIDTH_A].astype(jnp.bfloat16)
    kb = tok[:, 512:640]
    kbs = tok[:, 640:768]
    sq = kb * kb
    r0 = lax.rsqrt(jnp.sum(sq[:, 0:64], axis=1, keepdims=True) * (1.0 / HEAD_DIM_B) + NORM_EPS)
    r1 = lax.rsqrt(jnp.sum(sq[:, 64:128], axis=1, keepdims=True) * (1.0 / HEAD_DIM_B) + NORM_EPS)
    lane = lax.broadcasted_iota(jnp.int32, kb.shape, 1)
    r = jnp.where(lane < HEAD_DIM_B, r0, r1)
    kr = r * ((kb * knw_ref[...]) * cosk_ref[...] + (kbs * knws_ref[...]) * sink_ref[...])
    kb_ref[...] = kr.astype(jnp.bfloat16)

    def feat(lo, hi):
        return lax.dot_general(wfeat_ref[lo:hi, :], hb, (((1,), (1,)), ((), ())),
                               preferred_element_type=jnp.float32)

    qa_ref[...] = (feat(0, 512) * (LOG2E / math.sqrt(DIFF_HEAD_DIM))).astype(jnp.bfloat16)
    va_ref[...] = feat(512, 1024).astype(jnp.bfloat16)
    ga_ref[...] = _silu(feat(1024, 1536)).astype(jnp.bfloat16)
    qb = feat(1536, 2048)
    cq = cosq_ref[...]
    sq_ = sinq_ref[...]
    qnw = qnw_ref[...]
    for hd in range(N_KV_B * GQA_GROUP):
        q = qb[hd * 64:(hd + 1) * 64, :]
        rq = lax.rsqrt(jnp.mean(q * q, axis=0, keepdims=True) + NORM_EPS)
        qn = q * rq * qnw
        partner = jnp.concatenate([qn[16:32], qn[0:16], qn[48:64], qn[32:48]], axis=0)
        rot = qn * cq + partner * sq_
        qb_ref[hd * 64:(hd + 1) * 64, :] = (rot * (LOG2E / math.sqrt(HEAD_DIM_B))).astype(jnp.bfloat16)
    vb_ref[...] = feat(2048, 2176).astype(jnp.bfloat16)
    gb_ref[...] = _silu(feat(2176, 2688)).astype(jnp.bfloat16)


def _in_proj(x, mod3, norm_w, w_tok, w_feat_t, qnw_col, knw_row, knws_row, cosq, sinq, cosk, sink):
    bsz, seq, d = x.shape
    ts = PROJ_TILE
    bf = jnp.bfloat16
    const = lambda b, i: (0, 0)
    feat_spec = lambda rows: pl.BlockSpec((None, rows, ts), lambda b, i: (b, 0, i))
    return pl.pallas_call(
        _in_proj_kernel,
        grid=(bsz, seq // ts),
        in_specs=[
            pl.BlockSpec((None, ts, d), lambda b, i: (b, i, 0)),
            pl.BlockSpec((None, 3, d), lambda b, i: (b, 0, 0)),
            pl.BlockSpec((1, d), const),
            pl.BlockSpec(w_tok.shape, const),
            pl.BlockSpec(w_feat_t.shape, const),
            pl.BlockSpec((HEAD_DIM_B, 1), const),
            pl.BlockSpec((1, 128), const),
            pl.BlockSpec((1, 128), const),
            pl.BlockSpec((HEAD_DIM_B, ts), lambda b, i: (0, i)),
            pl.BlockSpec((HEAD_DIM_B, ts), lambda b, i: (0, i)),
            pl.BlockSpec((ts, 128), lambda b, i: (i, 0)),
            pl.BlockSpec((ts, 128), lambda b, i: (i, 0)),
        ],
        out_specs=[
            pl.BlockSpec((None, ts, WIDTH_A), lambda b, i: (b, i, 0)),
            pl.BlockSpec((None, ts, 128), lambda b, i: (b, i, 0)),
            feat_spec(512), feat_spec(512), feat_spec(512), feat_spec(512), feat_spec(128),
            feat_spec(512),
        ],
        out_shape=[
            jax.ShapeDtypeStruct((bsz, seq, WIDTH_A), bf),
            jax.ShapeDtypeStruct((bsz, seq, 128), bf),
            jax.ShapeDtypeStruct((bsz, 512, seq), bf),
            jax.ShapeDtypeStruct((bsz, 512, seq), bf),
            jax.ShapeDtypeStruct((bsz, 512, seq), bf),
            jax.ShapeDtypeStruct((bsz, 512, seq), bf),
            jax.ShapeDtypeStruct((bsz, 128, seq), bf),
            jax.ShapeDtypeStruct((bsz, 512, seq), bf),
        ],
        compiler_params=pltpu.CompilerParams(
            dimension_semantics=("arbitrary", "arbitrary"), vmem_limit_bytes=VMEM_LIMIT_BYTES),
        name="in_proj",
    )(x, mod3, norm_w, w_tok, w_feat_t, qnw_col, knw_row, knws_row, cosq, sinq, cosk, sink)


def _online_softmax_loop(n_kv, logits_fn, off_fn, vt_fn, s_refs, bm_refs, m_ref, l_ref, acc_ref):
    m_ref[...] = jnp.full(m_ref.shape, NEG_BIG, jnp.float32)
    l_ref[...] = jnp.zeros(l_ref.shape, jnp.float32)
    acc_ref[...] = jnp.zeros(acc_ref.shape, jnp.float32)

    def produce(j, slot):
        u = logits_fn(j)
        s_refs[slot][...] = u
        bm_refs[slot][...] = jnp.max(u, axis=0, keepdims=True)

    def consume(j, slot):
        off = off_fn(j)
        m_old = m_ref[...]
        m_new = jnp.maximum(m_old, bm_refs[slot][...] + off)
        alpha = jnp.exp2(m_old - m_new)
        p = jnp.exp2(s_refs[slot][...] - (m_new - off))
        l_ref[...] = alpha * l_ref[...] + jnp.sum(p, axis=0, keepdims=True)
        acc_ref[...] = alpha * acc_ref[...] + jnp.dot(vt_fn(j), p.astype(jnp.bfloat16),
                                                      preferred_element_type=jnp.float32)
        m_ref[...] = m_new

    assert n_kv % 2 == 0
    produce(0, 0)

    def pair(i, carry):
        j = 2 * i
        produce(j + 1, 1)
        consume(j, 0)
        produce(j + 2, 0)
        consume(j + 1, 1)
        return carry

    lax.fori_loop(0, n_kv // 2 - 1, pair, 0)
    produce(n_kv - 1, 1)
    consume(n_kv - 2, 0)
    consume(n_kv - 1, 1)


def _attn_diff_kernel(qt_ref, k_ref, vt_ref, gt_ref, bias_ref, slope_ref, sublnw_ref,
                      lq1_ref, lk1_ref, lq2_ref, lk2_ref, o_ref,
                      qbd_ref, s0_ref, s1_ref, bm0_ref, bm1_ref, m_ref, l_ref, acc_ref, *, lam_init):
    tq, tk = TQ_A, TK_A
    hd = pl.program_id(1)
    qi = pl.program_id(2)
    n_kv = k_ref.shape[0] // tk
    sigma = slope_ref[hd] * LOG2E

    zeros = jnp.zeros((DIFF_HEAD_DIM, tq), jnp.bfloat16)
    qbd_ref[0:64, 0:tq] = qt_ref[0:64, :]
    qbd_ref[64:128, 0:tq] = zeros
    qbd_ref[0:64, tq:2 * tq] = zeros
    qbd_ref[64:128, tq:2 * tq] = qt_ref[64:128, :]

    il = lax.broadcasted_iota(jnp.int32, (1, 2 * tq), 1)
    il = jnp.where(il >= tq, il - tq, il).astype(jnp.float32)

    def logits(j):
        j0 = pl.multiple_of(j * tk, tk)
        t = jnp.dot(k_ref[pl.ds(j0, tk), :], qbd_ref[...], preferred_element_type=jnp.float32)
        kind = jnp.where(j < qi, 0, jnp.where(j > qi, 1, 2))
        return t + bias_ref[kind]

    def offset(j):
        per_query = jnp.where(j > qi, 2.0 * sigma, 0.0)
        return per_query * il - sigma * (jnp.abs(j - qi) * tk).astype(jnp.float32)

    def values(j):
        return vt_ref[:, pl.ds(pl.multiple_of(j * tk, tk), tk)]

    _online_softmax_loop(n_kv, logits, offset, values, (s0_ref, s1_ref), (bm0_ref, bm1_ref),
                         m_ref, l_ref, acc_ref)

    lam = (jnp.exp(jnp.sum(lq1_ref[...] * lk1_ref[...], axis=1, keepdims=True))
           - jnp.exp(jnp.sum(lq2_ref[...] * lk2_ref[...], axis=1, keepdims=True)) + lam_init)
    inv_l = 1.0 / l_ref[...]
    o1 = acc_ref[:, 0:tq] * inv_l[:, 0:tq]
    o2 = acc_ref[:, tq:2 * tq] * inv_l[:, tq:2 * tq]
    diff = o1 - lam * o2
    ms = jnp.mean(diff * diff, axis=0, keepdims=True)
    y = diff * lax.rsqrt(ms + NORM_EPS) * sublnw_ref[...] * (1.0 - lam_init)
    gated = y * gt_ref[...].astype(jnp.float32)
    o_ref[...] = gated.T.astype(o_ref.dtype)


def _attn_diff(qa_t, ka, va_t, ga_t, bias, slopes, subln_col, lq1, lk1, lq2, lk2, lam_init):
    bsz, seq, _ = ka.shape
    tq = TQ_A
    n = 2 * tq
    vec = pl.BlockSpec((1, DIFF_HEAD_DIM), lambda b, h, i: (0, 0))
    return pl.pallas_call(
        functools.partial(_attn_diff_kernel, lam_init=lam_init),
        grid=(bsz, N_HEADS_A, seq // tq),
        in_specs=[
            pl.BlockSpec((None, 128, tq), lambda b, h, i: (b, h, i)),
            pl.BlockSpec((None, seq, 128), lambda b, h, i: (b, 0, h)),
            pl.BlockSpec((None, 128, seq), lambda b, h, i: (b, h, 0)),
            pl.BlockSpec((None, 128, tq), lambda b, h, i: (b, h, i)),
            pl.BlockSpec((None, 3, TK_A, n), lambda b, h, i: (h, 0, 0, 0)),
            pl.BlockSpec(memory_space=pltpu.SMEM),
            pl.BlockSpec((DIFF_V_DIM, 1), lambda b, h, i: (0, 0)),
            vec, vec, vec, vec,
        ],
        out_specs=pl.BlockSpec((None, tq, 128), lambda b, h, i: (b, i, h)),
        out_shape=jax.ShapeDtypeStruct((bsz, seq, WIDTH_A), jnp.bfloat16),
        scratch_shapes=[
            pltpu.VMEM((128, n), jnp.bfloat16),
            pltpu.VMEM((TK_A, n), jnp.float32),
            pltpu.VMEM((TK_A, n), jnp.float32),
            pltpu.VMEM((1, n), jnp.float32),
            pltpu.VMEM((1, n), jnp.float32),
            pltpu.VMEM((1, n), jnp.float32),
            pltpu.VMEM((1, n), jnp.float32),
            pltpu.VMEM((DIFF_V_DIM, n), jnp.float32),
        ],
        compiler_params=pltpu.CompilerParams(
            dimension_semantics=("arbitrary", "arbitrary", "arbitrary"),
            vmem_limit_bytes=VMEM_LIMIT_BYTES),
        name="attn_diff",
    )(qa_t, ka, va_t, ga_t, bias, slopes, subln_col, lq1, lk1, lq2, lk2)


def _attn_gqa_kernel(qt_ref, k_ref, vt_ref, gt_ref, o_ref,
                     qp_ref, s0_ref, s1_ref, bm0_ref, bm1_ref, m_ref, l_ref, acc_ref):
    tq, tk = TQ_B, TK_B
    g = pl.program_id(1)
    n_kv = k_ref.shape[0] // tk

    row = lax.broadcasted_iota(jnp.int32, (128, tq), 0)
    lo = g * HEAD_DIM_B
    mine = (row >= lo) & (row < lo + HEAD_DIM_B)
    for r in range(GQA_GROUP):
        q = qt_ref[r * 64:(r + 1) * 64, :].astype(jnp.float32)
        q2 = jnp.concatenate([q, q], axis=0)
        qp_ref[:, r * tq:(r + 1) * tq] = jnp.where(mine, q2, 0.0).astype(jnp.bfloat16)

    def logits(j):
        j0 = pl.multiple_of(j * tk, tk)
        return jnp.dot(k_ref[pl.ds(j0, tk), :], qp_ref[...], preferred_element_type=jnp.float32)

    def values(j):
        return vt_ref[:, pl.ds(pl.multiple_of(j * tk, tk), tk)]

    _online_softmax_loop(n_kv, logits, lambda j: 0.0, values, (s0_ref, s1_ref), (bm0_ref, bm1_ref),
                         m_ref, l_ref, acc_ref)

    o = acc_ref[...] * (1.0 / l_ref[...])
    o = jnp.concatenate([o[:, r * tq:(r + 1) * tq] for r in range(GQA_GROUP)], axis=0)
    gated = o * gt_ref[...].astype(jnp.float32)
    o_ref[...] = gated.T.astype(o_ref.dtype)


def _attn_gqa(qb_t, kb, vb_t, gb_t):
    bsz, seq, _ = kb.shape
    tq = TQ_B
    n = GQA_GROUP * tq
    return pl.pallas_call(
        _attn_gqa_kernel,
        grid=(bsz, N_KV_B, seq // tq),
        in_specs=[
            pl.BlockSpec((None, 256, tq), lambda b, g, i: (b, g, i)),
            pl.BlockSpec((None, seq, 128), lambda b, g, i: (b, 0, 0)),
            pl.BlockSpec((None, HEAD_DIM_B, seq), lambda b, g, i: (b, g, 0)),
            pl.BlockSpec((None, 256, tq), lambda b, g, i: (b, g, i)),
        ],
        out_specs=pl.BlockSpec((None, tq, 256), lambda b, g, i: (b, i, g)),
        out_shape=jax.ShapeDtypeStruct((bsz, seq, WIDTH_B), jnp.bfloat16),
        scratch_shapes=[
            pltpu.VMEM((128, n), jnp.bfloat16),
            pltpu.VMEM((TK_B, n), jnp.float32),
            pltpu.VMEM((TK_B, n), jnp.float32),
            pltpu.VMEM((1, n), jnp.float32),
            pltpu.VMEM((1, n), jnp.float32),
            pltpu.VMEM((1, n), jnp.float32),
            pltpu.VMEM((1, n), jnp.float32),
            pltpu.VMEM((HEAD_DIM_B, n), jnp.float32),
        ],
        compiler_params=pltpu.CompilerParams(
            dimension_semantics=("arbitrary", "arbitrary", "arbitrary"),
            vmem_limit_bytes=VMEM_LIMIT_BYTES),
        name="attn_gqa",
    )(qb_t, kb, vb_t, gb_t)


def _out_proj_kernel(ga_ref, gb_ref, wa_ref, wb_ref, x_ref, mod_ref, fw_ref, o_ref):
    y = jnp.dot(ga_ref[...], wa_ref[...], preferred_element_type=jnp.float32)
    y = y + jnp.dot(gb_ref[...], wb_ref[...], preferred_element_type=jnp.float32)
    z = x_ref[...] + mod_ref[2:3, :] * y
    ms = jnp.mean(z * z, axis=1, keepdims=True)
    o_ref[...] = z * lax.rsqrt(ms + NORM_EPS) * fw_ref[...]


def _out_proj(ga, gb, w_a, w_b, x, mod3, final_w):
    bsz, seq, d = x.shape
    ts = PROJ_TILE
    const = lambda b, i: (0, 0)
    return pl.pallas_call(
        _out_proj_kernel,
        grid=(bsz, seq // ts),
        in_specs=[
            pl.BlockSpec((None, ts, WIDTH_A), lambda b, i: (b, i, 0)),
            pl.BlockSpec((None, ts, WIDTH_B), lambda b, i: (b, i, 0)),
            pl.BlockSpec(w_a.shape, const),
            pl.BlockSpec(w_b.shape, const),
            pl.BlockSpec((None, ts, d), lambda b, i: (b, i, 0)),
            pl.BlockSpec((None, 3, d), lambda b, i: (b, 0, 0)),
            pl.BlockSpec((1, d), const),
        ],
        out_specs=pl.BlockSpec((None, ts, d), lambda b, i: (b, i, 0)),
        out_shape=jax.ShapeDtypeStruct((bsz, seq, d), jnp.float32),
        compiler_params=pltpu.CompilerParams(
            dimension_semantics=("arbitrary", "arbitrary"), vmem_limit_bytes=VMEM_LIMIT_BYTES),
        name="out_proj",
    )(ga, gb, w_a, w_b, x, mod3, final_w)


def _rope_tables(seq):
    pos = np.arange(seq)
    row = (pos // GRID_W).astype(np.float32)
    col = (pos % GRID_W).astype(np.float32)
    n_freq = ROT_HALF // 2
    freqs = (1.0 / (ROPE_THETA ** (np.arange(n_freq, dtype=np.float32) * 2.0 / ROT_HALF))).astype(np.float32)
    row, col, freqs = jnp.asarray(row), jnp.asarray(col), jnp.asarray(freqs)
    ang_r = row[:, None] * freqs[None, :]
    ang_c = col[:, None] * freqs[None, :]
    cos = jnp.concatenate([jnp.cos(ang_r), jnp.cos(ang_r), jnp.cos(ang_c), jnp.cos(ang_c)], axis=1)
    sin = jnp.concatenate([-jnp.sin(ang_r), jnp.sin(ang_r), -jnp.sin(ang_c), jnp.sin(ang_c)], axis=1)
    d = np.arange(HEAD_DIM_B)
    partner = np.where((d // n_freq) % 2 == 0, d + n_freq, d - n_freq)
    return cos, sin, partner


def _alibi_tables(slopes):
    sig = (slopes * LOG2E)[:, None, None]
    jl = jnp.arange(TK_A, dtype=jnp.float32)[None, :, None]
    il = jnp.tile(jnp.arange(TQ_A, dtype=jnp.float32), 2)[None, None, :]
    colb = jnp.broadcast_to(sig * jl, (N_HEADS_A, TK_A, 2 * TQ_A))
    diagb = sig * (il - jnp.abs(il - jl))
    return jnp.stack([colb, -colb, diagb], axis=1)


def kernel(x, c, w_ada, b_ada, norm_w, w_in, lambda_q1, lambda_k1, lambda_q2, lambda_k2,
           subln_w, q_norm_w, k_norm_w, w_out, final_norm_w):
    assert w_ada.shape[0] == 1, "single-layer problem: the final norm is fused into the output projection"
    bsz, seq, d = x.shape
    assert seq % TQ_A == 0 and seq % TK_B == 0 and seq % PROJ_TILE == 0
    bf = jnp.bfloat16
    cos, sin, partner = _rope_tables(seq)
    cosq, sinq = cos.T, sin.T
    cosk, sink = jnp.tile(cos, (1, N_KV_B)), jnp.tile(sin, (1, N_KV_B))
    slopes = jnp.asarray(2.0 ** (-8.0 * np.arange(1, N_HEADS_A + 1) / N_HEADS_A), jnp.float32)
    alibi = _alibi_tables(slopes)
    lam_init = 0.8 - 0.6 * math.exp(-0.3 * 0)

    w = w_in[0]
    q_a, k_a, v_a, g_a = w[:, 0:512], w[:, 512:1024], w[:, 1024:1536], w[:, 1536:2048]
    q_b, k_b, v_b, g_b = w[:, 2048:2560], w[:, 2560:2688], w[:, 2688:2816], w[:, 2816:3328]
    partner2 = np.concatenate([partner, partner + HEAD_DIM_B])
    w_tok = jnp.concatenate([k_a, k_b, k_b[:, partner2]], axis=1).astype(bf)
    w_feat_t = jnp.concatenate([q_a, v_a, g_a, q_b, v_b, g_b], axis=1).T.astype(bf)
    knw = jnp.tile(k_norm_w[0], N_KV_B)
    wo = w_out[0].astype(bf)

    mod3 = _adaln_mod(c, w_ada[0], b_ada[0]).reshape(bsz, 3, d)
    ka, kb, qa_t, va_t, ga_t, qb_t, vb_t, gb_t = _in_proj(
        x, mod3, norm_w[0].reshape(1, d), w_tok, w_feat_t,
        q_norm_w[0].reshape(HEAD_DIM_B, 1), knw.reshape(1, 128), knw[partner2].reshape(1, 128),
        cosq, sinq, cosk, sink)
    oa = _attn_diff(qa_t, ka, va_t, ga_t, alibi, slopes,
                    subln_w[0].reshape(DIFF_V_DIM, 1),
                    lambda_q1[0].reshape(1, -1), lambda_k1[0].reshape(1, -1),
                    lambda_q2[0].reshape(1, -1), lambda_k2[0].reshape(1, -1), lam_init)
    ob = _attn_gqa(qb_t, kb, vb_t, gb_t)
    return _out_proj(oa, ob, wo[:WIDTH_A], wo[WIDTH_A:], x, mod3, final_norm_w.reshape(1, d))
```

```python
import functools
import math

import numpy as np
import jax
import jax.numpy as jnp
from jax import lax
from jax.experimental import pallas as pl
from jax.experimental.pallas import tpu as pltpu

D_MODEL = 1024
N_HEADS_A = 4
DIFF_HEAD_DIM = 64
DIFF_V_DIM = 128
WIDTH_A = 512
N_KV_B = 2
GQA_GROUP = 4
HEAD_DIM_B = 64
WIDTH_B = 512
GRID_W = 64
ROPE_THETA = 10000.0
ROT_HALF = 32
NORM_EPS = 1e-6
LANE_CHUNK = 512
ONES_ROWS = 16
LOG2E = 1.4426950408889634
NEG_BIG = -1e30
GUARD_LOG2 = 100.0
NORM_MARGIN = 1.01

VMEM_LIMIT_BYTES = 56 * 1024 * 1024

PROJ_TILE = 512
TQ_A = 512
TK_A = 512
TQ_B = 256
TK_B = 512


def _silu(v):
    return v * (1.0 / (1.0 + jnp.exp(-v)))


def _mod_kernel(c_ref, w_ref, b_ref, o_ref):
    c = c_ref[...]
    o_ref[...] = jnp.dot(_silu(c), w_ref[...], preferred_element_type=jnp.float32,
                         precision=lax.Precision.HIGHEST) + b_ref[...]


def _adaln_mod(c, w_ada, b_ada):
    bsz, d = c.shape
    n = w_ada.shape[1]
    tn = 512
    return pl.pallas_call(
        _mod_kernel,
        grid=(n // tn,),
        in_specs=[pl.BlockSpec((bsz, d), lambda j: (0, 0)),
                  pl.BlockSpec((d, tn), lambda j: (0, j)),
                  pl.BlockSpec((1, tn), lambda j: (0, j))],
        out_specs=pl.BlockSpec((bsz, tn), lambda j: (0, j)),
        out_shape=jax.ShapeDtypeStruct((bsz, n), jnp.float32),
        name="adaln_mod",
    )(c, w_ada, b_ada.reshape(1, n))


def _in_proj_kernel(x_ref, mod_ref, nw_ref, wtok_ref, wfeat_ref, qnw_ref, knw_ref, knws_ref,
                    cosq_ref, sinq_ref, cosk_ref, sink_ref, grp_ref,
                    ka_ref, kb_ref, qa_ref, va_ref, ga_ref, qb_ref, vb_ref, gb_ref, kn_ref):
    x = x_ref[...]
    shift = mod_ref[0:1, :]
    scale = mod_ref[1:2, :]
    ms = jnp.mean(x * x, axis=1, keepdims=True)
    h = (x * lax.rsqrt(ms + NORM_EPS)) * nw_ref[...] * (1.0 + scale) + shift
    hb = h.astype(jnp.bfloat16)

    tok = jnp.dot(hb, wtok_ref[...], preferred_element_type=jnp.float32)
    ka = tok[:, 0:WIDTH_A]
    ka_ref[...] = ka.astype(jnp.bfloat16)
    kn = jnp.dot(ka * ka, grp_ref[...], preferred_element_type=jnp.float32,
                 precision=lax.Precision.HIGHEST)
    kn_ref[...] = jnp.max(kn, axis=0, keepdims=True)
    kb = tok[:, 512:640]
    kbs = tok[:, 640:768]
    sq = kb * kb
    r0 = lax.rsqrt(jnp.sum(sq[:, 0:64], axis=1, keepdims=True) * (1.0 / HEAD_DIM_B) + NORM_EPS)
    r1 = lax.rsqrt(jnp.sum(sq[:, 64:128], axis=1, keepdims=True) * (1.0 / HEAD_DIM_B) + NORM_EPS)
    lane = lax.broadcasted_iota(jnp.int32, kb.shape, 1)
    r = jnp.where(lane < HEAD_DIM_B, r0, r1)
    kr = r * ((kb * knw_ref[...]) * cosk_ref[...] + (kbs * knws_ref[...]) * sink_ref[...])
    kb_ref[...] = kr.astype(jnp.bfloat16)

    def feat(lo, hi):
        return lax.dot_general(wfeat_ref[lo:hi, :], hb, (((1,), (1,)), ((), ())),
                               preferred_element_type=jnp.float32)

    qa_ref[...] = (feat(0, 512) * (LOG2E / math.sqrt(DIFF_HEAD_DIM))).astype(jnp.bfloat16)
    va = feat(512, 1024).astype(jnp.bfloat16)
    ones = jnp.ones((ONES_ROWS, va.shape[1]), jnp.bfloat16)
    for hd in range(N_HEADS_A):
        va_ref[hd, 0:DIFF_V_DIM, :] = va[hd * DIFF_V_DIM:(hd + 1) * DIFF_V_DIM, :]
        va_ref[hd, DIFF_V_DIM:DIFF_V_DIM + ONES_ROWS, :] = ones
    ga_ref[...] = _silu(feat(1024, 1536)).astype(jnp.bfloat16)
    qb = feat(1536, 2048)
    cq = cosq_ref[...]
    sq_ = sinq_ref[...]
    qnw = qnw_ref[...]
    for hd in range(N_KV_B * GQA_GROUP):
        q = qb[hd * 64:(hd + 1) * 64, :]
        rq = lax.rsqrt(jnp.mean(q * q, axis=0, keepdims=True) + NORM_EPS)
        qn = q * rq * qnw
        partner = jnp.concatenate([qn[16:32], qn[0:16], qn[48:64], qn[32:48]], axis=0)
        rot = qn * cq + partner * sq_
        qb_ref[hd * 64:(hd + 1) * 64, :] = (rot * (LOG2E / math.sqrt(HEAD_DIM_B))).astype(jnp.bfloat16)
    vb = feat(2048, 2176).astype(jnp.bfloat16)
    for g in range(N_KV_B):
        vb_ref[g, 0:HEAD_DIM_B, :] = vb[g * HEAD_DIM_B:(g + 1) * HEAD_DIM_B, :]
        vb_ref[g, HEAD_DIM_B:HEAD_DIM_B + ONES_ROWS, :] = ones
    gb_ref[...] = _silu(feat(2176, 2688)).astype(jnp.bfloat16)


def _in_proj(x, mod3, norm_w, w_tok, w_feat_t, qnw_col, knw_row, knws_row, cosq, sinq, cosk, sink,
             grp):
    bsz, seq, d = x.shape
    ts = PROJ_TILE
    bf = jnp.bfloat16
    const = lambda b, i: (0, 0)
    feat_spec = lambda rows: pl.BlockSpec((None, rows, ts), lambda b, i: (b, 0, i))
    return pl.pallas_call(
        _in_proj_kernel,
        grid=(bsz, seq // ts),
        in_specs=[
            pl.BlockSpec((None, ts, d), lambda b, i: (b, i, 0)),
            pl.BlockSpec((None, 3, d), lambda b, i: (b, 0, 0)),
            pl.BlockSpec((1, d), const),
            pl.BlockSpec(w_tok.shape, const),
            pl.BlockSpec(w_feat_t.shape, const),
            pl.BlockSpec((HEAD_DIM_B, 1), const),
            pl.BlockSpec((1, 128), const),
            pl.BlockSpec((1, 128), const),
            pl.BlockSpec((HEAD_DIM_B, ts), lambda b, i: (0, i)),
            pl.BlockSpec((HEAD_DIM_B, ts), lambda b, i: (0, i)),
            pl.BlockSpec((ts, 128), lambda b, i: (i, 0)),
            pl.BlockSpec((ts, 128), lambda b, i: (i, 0)),
            pl.BlockSpec(grp.shape, const),
        ],
        out_specs=[
            pl.BlockSpec((None, ts, WIDTH_A), lambda b, i: (b, i, 0)),
            pl.BlockSpec((None, ts, 128), lambda b, i: (b, i, 0)),
            feat_spec(512),
            pl.BlockSpec((None, N_HEADS_A, DIFF_V_DIM + ONES_ROWS, ts), lambda b, i: (b, 0, 0, i)),
            feat_spec(512), feat_spec(512),
            pl.BlockSpec((None, N_KV_B, HEAD_DIM_B + ONES_ROWS, ts), lambda b, i: (b, 0, 0, i)),
            feat_spec(512),
            pl.BlockSpec((None, None, 1, 128), lambda b, i: (b, i, 0, 0)),
        ],
        out_shape=[
            jax.ShapeDtypeStruct((bsz, seq, WIDTH_A), bf),
            jax.ShapeDtypeStruct((bsz, seq, 128), bf),
            jax.ShapeDtypeStruct((bsz, 512, seq), bf),
            jax.ShapeDtypeStruct((bsz, N_HEADS_A, DIFF_V_DIM + ONES_ROWS, seq), bf),
            jax.ShapeDtypeStruct((bsz, 512, seq), bf),
            jax.ShapeDtypeStruct((bsz, 512, seq), bf),
            jax.ShapeDtypeStruct((bsz, N_KV_B, HEAD_DIM_B + ONES_ROWS, seq), bf),
            jax.ShapeDtypeStruct((bsz, 512, seq), bf),
            jax.ShapeDtypeStruct((bsz, seq // ts, 1, 128), jnp.float32),
        ],
        compiler_params=pltpu.CompilerParams(
            dimension_semantics=("arbitrary", "arbitrary"), vmem_limit_bytes=VMEM_LIMIT_BYTES),
        name="in_proj",
    )(x, mod3, norm_w, w_tok, w_feat_t, qnw_col, knw_row, knws_row, cosq, sinq, cosk, sink, grp)


def _softmax_over_tiles(n_rest, first_tile, tile_fn, logits_fn, off_fn, vt_fn, bound, p_refs, m_ref,
                        acc_ref):
    chunks = [slice(c, c + LANE_CHUNK) for c in range(0, m_ref.shape[1], LANE_CHUNK)]
    m_ref[...] = jnp.full(m_ref.shape, NEG_BIG, jnp.float32)
    acc_ref[...] = jnp.zeros(acc_ref.shape, jnp.float32)

    def online_tile(j):
        off = off_fn(j)
        vt = vt_fn(j)
        for cols in chunks:
            u = logits_fn(j, cols)
            m_old = m_ref[:, cols]
            m_new = jnp.maximum(m_old, jnp.max(u, axis=0, keepdims=True) + off[:, cols])
            alpha = jnp.exp2(m_old - m_new)
            p = jnp.exp2((u - (m_new - off[:, cols])).astype(jnp.bfloat16))
            acc_ref[:, cols] = alpha * acc_ref[:, cols] + jnp.dot(
                vt, p, preferred_element_type=jnp.float32)
            m_ref[:, cols] = m_new

    online_tile(first_tile)
    fixed_ok = jnp.max(bound - m_ref[...]) <= GUARD_LOG2

    def probs(t, slot, cols):
        j = tile_fn(t)
        shift = m_ref[:, cols] - off_fn(j)[:, cols]
        p_refs[slot][:, cols] = jnp.exp2((logits_fn(j, cols) - shift).astype(jnp.bfloat16))

    def accumulate(t, slot, cols, vt):
        acc_ref[:, cols] += jnp.dot(vt, p_refs[slot][:, cols], preferred_element_type=jnp.float32)

    def fixed_step(t, slot):
        vt = vt_fn(tile_fn(t))
        for cols in chunks:
            probs(t + 1, 1 - slot, cols)
            accumulate(t, slot, cols, vt)

    assert n_rest % 2 == 1

    @pl.when(fixed_ok)
    def _():
        for cols in chunks:
            probs(0, 0, cols)

        for t in range(n_rest - 1):
            fixed_step(t, t % 2)
        vt = vt_fn(tile_fn(n_rest - 1))
        for cols in chunks:
            accumulate(n_rest - 1, 0, cols, vt)

    @pl.when(jnp.logical_not(fixed_ok))
    def _():
        def body(t, carry):
            online_tile(tile_fn(t))
            return carry

        lax.fori_loop(0, n_rest, body, 0)


def _attn_diff_kernel(qt_ref, k_ref, vt_ref, gt_ref, bias_ref, slope_ref, kmax_ref, sublnw_ref,
                      lq1_ref, lk1_ref, lq2_ref, lk2_ref, o_ref,
                      qbd_ref, p0_ref, p1_ref, m_ref, acc_ref, *, lam_init):
    tq, tk = TQ_A, TK_A
    b = pl.program_id(0)
    hd = pl.program_id(1)
    qi = pl.program_id(2)
    n_kv = k_ref.shape[0] // tk
    sigma = slope_ref[hd] * LOG2E

    zeros = jnp.zeros((DIFF_HEAD_DIM, tq), jnp.bfloat16)
    qbd_ref[0:64, 0:tq] = qt_ref[0:64, :]
    qbd_ref[64:128, 0:tq] = zeros
    qbd_ref[0:64, tq:2 * tq] = zeros
    qbd_ref[64:128, tq:2 * tq] = qt_ref[64:128, :]

    lane = lax.broadcasted_iota(jnp.int32, (1, 2 * tq), 1)
    il = jnp.where(lane >= tq, lane - tq, lane).astype(jnp.float32)

    def logits(j, cols):
        j0 = pl.multiple_of(j * tk, tk)
        t = jnp.dot(k_ref[pl.ds(j0, tk), :], qbd_ref[:, cols], preferred_element_type=jnp.float32)
        kind = jnp.where(j < qi, 0, jnp.where(j > qi, 1, 2))
        return t + bias_ref[kind, :, cols]

    def offset(j):
        per_query = jnp.where(j > qi, 2.0 * sigma, 0.0)
        return per_query * il - sigma * (jnp.abs(j - qi) * tk).astype(jnp.float32)

    def values(j):
        return vt_ref[:, pl.ds(pl.multiple_of(j * tk, tk), tk)]

    q = qbd_ref[...].astype(jnp.float32)
    q_norm = jnp.sqrt(jnp.sum(q * q, axis=0, keepdims=True))
    k_norm = jnp.where(lane < tq, kmax_ref[(b * N_HEADS_A + hd) * 2], kmax_ref[(b * N_HEADS_A + hd) * 2 + 1])
    bound = q_norm * k_norm + sigma * il

    _softmax_over_tiles(n_kv - 1, qi, lambda t: jnp.where(t >= qi, t + 1, t), logits, offset, values,
                        bound, (p0_ref, p1_ref), m_ref, acc_ref)

    lam = (jnp.exp(jnp.sum(lq1_ref[...] * lk1_ref[...], axis=1, keepdims=True))
           - jnp.exp(jnp.sum(lq2_ref[...] * lk2_ref[...], axis=1, keepdims=True)) + lam_init)
    inv_l = 1.0 / acc_ref[DIFF_V_DIM:DIFF_V_DIM + 1, :]
    o1 = acc_ref[0:DIFF_V_DIM, 0:tq] * inv_l[:, 0:tq]
    o2 = acc_ref[0:DIFF_V_DIM, tq:2 * tq] * inv_l[:, tq:2 * tq]
    diff = o1 - lam * o2
    ms = jnp.mean(diff * diff, axis=0, keepdims=True)
    y = diff * lax.rsqrt(ms + NORM_EPS) * sublnw_ref[...] * (1.0 - lam_init)
    gated = y * gt_ref[...].astype(jnp.float32)
    o_ref[...] = gated.T.astype(o_ref.dtype)


def _attn_diff(qa_t, ka, va_t, ga_t, bias, slopes, kmax, subln_col, lq1, lk1, lq2, lk2, lam_init):
    bsz, seq, _ = ka.shape
    tq = TQ_A
    n = 2 * tq
    vec = pl.BlockSpec((1, DIFF_HEAD_DIM), lambda b, h, i: (0, 0))
    smem = pl.BlockSpec(memory_space=pltpu.SMEM)
    return pl.pallas_call(
        functools.partial(_attn_diff_kernel, lam_init=lam_init),
        grid=(bsz, N_HEADS_A, seq // tq),
        in_specs=[
            pl.BlockSpec((None, 128, tq), lambda b, h, i: (b, h, i)),
            pl.BlockSpec((None, seq, 128), lambda b, h, i: (b, 0, h)),
            pl.BlockSpec((None, None, DIFF_V_DIM + ONES_ROWS, seq), lambda b, h, i: (b, h, 0, 0)),
            pl.BlockSpec((None, 128, tq), lambda b, h, i: (b, h, i)),
            pl.BlockSpec((None, 3, TK_A, n), lambda b, h, i: (h, 0, 0, 0)),
            smem, smem,
            pl.BlockSpec((DIFF_V_DIM, 1), lambda b, h, i: (0, 0)),
            vec, vec, vec, vec,
        ],
        out_specs=pl.BlockSpec((None, tq, 128), lambda b, h, i: (b, i, h)),
        out_shape=jax.ShapeDtypeStruct((bsz, seq, WIDTH_A), jnp.bfloat16),
        scratch_shapes=[
            pltpu.VMEM((128, n), jnp.bfloat16),
            pltpu.VMEM((TK_A, n), jnp.bfloat16),
            pltpu.VMEM((TK_A, n), jnp.bfloat16),
            pltpu.VMEM((1, n), jnp.float32),
            pltpu.VMEM((DIFF_V_DIM + ONES_ROWS, n), jnp.float32),
        ],
        compiler_params=pltpu.CompilerParams(
            dimension_semantics=("arbitrary", "arbitrary", "arbitrary"),
            vmem_limit_bytes=VMEM_LIMIT_BYTES),
        name="attn_diff",
    )(qa_t, ka, va_t, ga_t, bias, slopes, kmax, subln_col, lq1, lk1, lq2, lk2)


def _attn_gqa_kernel(qt_ref, k_ref, vt_ref, gt_ref, kmax_ref, o_ref,
                     qp_ref, p0_ref, p1_ref, m_ref, acc_ref):
    tq, tk = TQ_B, TK_B
    g = pl.program_id(1)
    n_kv = k_ref.shape[0] // tk
    n = GQA_GROUP * tq

    row = lax.broadcasted_iota(jnp.int32, (128, tq), 0)
    lo = g * HEAD_DIM_B
    mine = (row >= lo) & (row < lo + HEAD_DIM_B)
    for r in range(GQA_GROUP):
        q = qt_ref[r * 64:(r + 1) * 64, :].astype(jnp.float32)
        q2 = jnp.concatenate([q, q], axis=0)
        qp_ref[:, r * tq:(r + 1) * tq] = jnp.where(mine, q2, 0.0).astype(jnp.bfloat16)

    def logits(j, cols):
        j0 = pl.multiple_of(j * tk, tk)
        return jnp.dot(k_ref[pl.ds(j0, tk), :], qp_ref[:, cols], preferred_element_type=jnp.float32)

    def values(j):
        return vt_ref[:, pl.ds(pl.multiple_of(j * tk, tk), tk)]

    no_offset = jnp.zeros((1, n), jnp.float32)
    q = qp_ref[...].astype(jnp.float32)
    bound = jnp.sqrt(jnp.sum(q * q, axis=0, keepdims=True)) * kmax_ref[0]

    _softmax_over_tiles(n_kv - 1, jnp.int32(0), lambda t: jnp.asarray(t, jnp.int32) + 1, logits,
                        lambda j: no_offset, values, bound, (p0_ref, p1_ref), m_ref, acc_ref)

    o = acc_ref[0:HEAD_DIM_B, :] * (1.0 / acc_ref[HEAD_DIM_B:HEAD_DIM_B + 1, :])
    o = jnp.concatenate([o[:, r * tq:(r + 1) * tq] for r in range(GQA_GROUP)], axis=0)
    gated = o * gt_ref[...].astype(jnp.float32)
    o_ref[...] = gated.T.astype(o_ref.dtype)


def _attn_gqa(qb_t, kb, vb_t, gb_t, kmax):
    bsz, seq, _ = kb.shape
    tq = TQ_B
    n = GQA_GROUP * tq
    return pl.pallas_call(
        _attn_gqa_kernel,
        grid=(bsz, N_KV_B, seq // tq),
        in_specs=[
            pl.BlockSpec((None, 256, tq), lambda b, g, i: (b, g, i)),
            pl.BlockSpec((None, seq, 128), lambda b, g, i: (b, 0, 0)),
            pl.BlockSpec((None, None, HEAD_DIM_B + ONES_ROWS, seq), lambda b, g, i: (b, g, 0, 0)),
            pl.BlockSpec((None, 256, tq), lambda b, g, i: (b, g, i)),
            pl.BlockSpec(memory_space=pltpu.SMEM),
        ],
        out_specs=pl.BlockSpec((None, tq, 256), lambda b, g, i: (b, i, g)),
        out_shape=jax.ShapeDtypeStruct((bsz, seq, WIDTH_B), jnp.bfloat16),
        scratch_shapes=[
            pltpu.VMEM((128, n), jnp.bfloat16),
            pltpu.VMEM((TK_B, n), jnp.bfloat16),
            pltpu.VMEM((TK_B, n), jnp.bfloat16),
            pltpu.VMEM((1, n), jnp.float32),
            pltpu.VMEM((HEAD_DIM_B + ONES_ROWS, n), jnp.float32),
        ],
        compiler_params=pltpu.CompilerParams(
            dimension_semantics=("arbitrary", "arbitrary", "arbitrary"),
            vmem_limit_bytes=VMEM_LIMIT_BYTES),
        name="attn_gqa",
    )(qb_t, kb, vb_t, gb_t, kmax)


def _out_proj_kernel(ga_ref, gb_ref, wa_ref, wb_ref, x_ref, mod_ref, fw_ref, o_ref):
    y = jnp.dot(ga_ref[...], wa_ref[...], preferred_element_type=jnp.float32)
    y = y + jnp.dot(gb_ref[...], wb_ref[...], preferred_element_type=jnp.float32)
    z = x_ref[...] + mod_ref[2:3, :] * y
    ms = jnp.mean(z * z, axis=1, keepdims=True)
    o_ref[...] = z * lax.rsqrt(ms + NORM_EPS) * fw_ref[...]


def _out_proj(ga, gb, w_a, w_b, x, mod3, final_w):
    bsz, seq, d = x.shape
    ts = PROJ_TILE
    const = lambda b, i: (0, 0)
    return pl.pallas_call(
        _out_proj_kernel,
        grid=(bsz, seq // ts),
        in_specs=[
            pl.BlockSpec((None, ts, WIDTH_A), lambda b, i: (b, i, 0)),
            pl.BlockSpec((None, ts, WIDTH_B), lambda b, i: (b, i, 0)),
            pl.BlockSpec(w_a.shape, const),
            pl.BlockSpec(w_b.shape, const),
            pl.BlockSpec((None, ts, d), lambda b, i: (b, i, 0)),
            pl.BlockSpec((None, 3, d), lambda b, i: (b, 0, 0)),
            pl.BlockSpec((1, d), const),
        ],
        out_specs=pl.BlockSpec((None, ts, d), lambda b, i: (b, i, 0)),
        out_shape=jax.ShapeDtypeStruct((bsz, seq, d), jnp.float32),
        compiler_params=pltpu.CompilerParams(
            dimension_semantics=("arbitrary", "arbitrary"), vmem_limit_bytes=VMEM_LIMIT_BYTES),
        name="out_proj",
    )(ga, gb, w_a, w_b, x, mod3, final_w)


def _rope_tables(seq):
    pos = np.arange(seq)
    row = (pos // GRID_W).astype(np.float32)
    col = (pos % GRID_W).astype(np.float32)
    n_freq = ROT_HALF // 2
    freqs = (1.0 / (ROPE_THETA ** (np.arange(n_freq, dtype=np.float32) * 2.0 / ROT_HALF))).astype(np.float32)
    row, col, freqs = jnp.asarray(row), jnp.asarray(col), jnp.asarray(freqs)
    ang_r = row[:, None] * freqs[None, :]
    ang_c = col[:, None] * freqs[None, :]
    cos = jnp.concatenate([jnp.cos(ang_r), jnp.cos(ang_r), jnp.cos(ang_c), jnp.cos(ang_c)], axis=1)
    sin = jnp.concatenate([-jnp.sin(ang_r), jnp.sin(ang_r), -jnp.sin(ang_c), jnp.sin(ang_c)], axis=1)
    d = np.arange(HEAD_DIM_B)
    partner = np.where((d // n_freq) % 2 == 0, d + n_freq, d - n_freq)
    return cos, sin, partner


def _alibi_tables(slopes):
    sig = (slopes * LOG2E)[:, None, None]
    jl = jnp.arange(TK_A, dtype=jnp.float32)[None, :, None]
    il = jnp.tile(jnp.arange(TQ_A, dtype=jnp.float32), 2)[None, None, :]
    colb = jnp.broadcast_to(sig * jl, (N_HEADS_A, TK_A, 2 * TQ_A))
    diagb = sig * (il - jnp.abs(il - jl))
    return jnp.stack([colb, -colb, diagb], axis=1)


def kernel(x, c, w_ada, b_ada, norm_w, w_in, lambda_q1, lambda_k1, lambda_q2, lambda_k2,
           subln_w, q_norm_w, k_norm_w, w_out, final_norm_w):
    assert w_ada.shape[0] == 1, "single-layer problem: the final norm is fused into the output projection"
    bsz, seq, d = x.shape
    assert seq % TQ_A == 0 and seq % TK_B == 0 and seq % PROJ_TILE == 0
    bf = jnp.bfloat16
    cos, sin, partner = _rope_tables(seq)
    cosq, sinq = cos.T, sin.T
    cosk, sink = jnp.tile(cos, (1, N_KV_B)), jnp.tile(sin, (1, N_KV_B))
    slopes = jnp.asarray(2.0 ** (-8.0 * np.arange(1, N_HEADS_A + 1) / N_HEADS_A), jnp.float32)
    alibi = _alibi_tables(slopes)
    lam_init = 0.8 - 0.6 * math.exp(-0.3 * 0)
    grp = jnp.asarray(np.arange(WIDTH_A)[:, None] // DIFF_HEAD_DIM == np.arange(128)[None, :], jnp.float32)

    w = w_in[0]
    q_a, k_a, v_a, g_a = w[:, 0:512], w[:, 512:1024], w[:, 1024:1536], w[:, 1536:2048]
    q_b, k_b, v_b, g_b = w[:, 2048:2560], w[:, 2560:2688], w[:, 2688:2816], w[:, 2816:3328]
    partner2 = np.concatenate([partner, partner + HEAD_DIM_B])
    w_tok = jnp.concatenate([k_a, k_b, k_b[:, partner2]], axis=1).astype(bf)
    w_feat_t = jnp.concatenate([q_a, v_a, g_a, q_b, v_b, g_b], axis=1).T.astype(bf)
    knw = jnp.tile(k_norm_w[0], N_KV_B)
    wo = w_out[0].astype(bf)

    mod3 = _adaln_mod(c, w_ada[0], b_ada[0]).reshape(bsz, 3, d)
    ka, kb, qa_t, va_t, ga_t, qb_t, vb_t, gb_t, kn = _in_proj(
        x, mod3, norm_w[0].reshape(1, d), w_tok, w_feat_t,
        q_norm_w[0].reshape(HEAD_DIM_B, 1), knw.reshape(1, 128), knw[partner2].reshape(1, 128),
        cosq, sinq, cosk, sink, grp)
    kmax_a = (jnp.sqrt(jnp.max(kn[:, :, 0, :2 * N_HEADS_A], axis=1)) * NORM_MARGIN).reshape(-1)
    kmax_b = (math.sqrt(HEAD_DIM_B) * NORM_MARGIN * jnp.max(jnp.abs(k_norm_w[0]))).reshape(1)
    oa = _attn_diff(qa_t, ka, va_t, ga_t, alibi, slopes, kmax_a,
                    subln_w[0].reshape(DIFF_V_DIM, 1),
                    lambda_q1[0].reshape(1, -1), lambda_k1[0].reshape(1, -1),
                    lambda_q2[0].reshape(1, -1), lambda_k2[0].reshape(1, -1), lam_init)
    ob = _attn_gqa(qb_t, kb, vb_t, gb_t, kmax_b)
    return _out_proj(oa, ob, wo[:WIDTH_A], wo[WIDTH_A:], x, mod3, final_norm_w.reshape(1, d))
```

```python
import functools
import math

import numpy as np
import jax
import jax.numpy as jnp
from jax import lax
from jax.experimental import pallas as pl
from jax.experimental.pallas import tpu as pltpu

D_MODEL = 1024
N_HEADS_A = 4
DIFF_HEAD_DIM = 64
DIFF_V_DIM = 128
WIDTH_A = 512
N_KV_B = 2
GQA_GROUP = 4
HEAD_DIM_B = 64
WIDTH_B = 512
GRID_W = 64
ROPE_THETA = 10000.0
ROT_HALF = 32
NORM_EPS = 1e-6
LANE_CHUNK = 512
ONES_ROWS = 16
LOG2E = 1.4426950408889634
NEG_BIG = -1e30
GUARD_LOG2 = 100.0
NORM_MARGIN = 1.01
F8 = jnp.float8_e4m3fn
F8_SAFE = 440.0
SPLIT = 4
QK_SCALE = LOG2E / math.sqrt(DIFF_HEAD_DIM)
QK_SCALE_ROOT = math.sqrt(QK_SCALE)

VMEM_LIMIT_BYTES = 56 * 1024 * 1024

PROJ_TILE = 512
TQ_A = 512
TK_A = 512
TQ_B = 256
TK_B = 512

assert DIFF_HEAD_DIM == HEAD_DIM_B and LANE_CHUNK == TQ_A


def _silu(v):
    return v * (1.0 / (1.0 + jnp.exp(-v)))


def _split_f8(v):
    hi = v.astype(F8)
    lo = (v - hi.astype(jnp.float32)).astype(F8)
    return hi, lo


def _mod_kernel(c_ref, w_ref, b_ref, o_ref):
    c = c_ref[...]
    o_ref[...] = jnp.dot(_silu(c), w_ref[...], preferred_element_type=jnp.float32,
                         precision=lax.Precision.HIGHEST) + b_ref[...]


def _adaln_mod(c, w_ada, b_ada):
    bsz, d = c.shape
    n = w_ada.shape[1]
    tn = 512
    return pl.pallas_call(
        _mod_kernel,
        grid=(n // tn,),
        in_specs=[pl.BlockSpec((bsz, d), lambda j: (0, 0)),
                  pl.BlockSpec((d, tn), lambda j: (0, j)),
                  pl.BlockSpec((1, tn), lambda j: (0, j))],
        out_specs=pl.BlockSpec((bsz, tn), lambda j: (0, j)),
        out_shape=jax.ShapeDtypeStruct((bsz, n), jnp.float32),
        name="adaln_mod",
    )(c, w_ada, b_ada.reshape(1, n))


def _store_keys_f8(k2, out_ref, first):
    lane = lax.broadcasted_iota(jnp.int32, k2.shape, 1)
    swapped = pltpu.roll(k2, HEAD_DIM_B, axis=1)
    for g, dup in enumerate((jnp.where(lane < HEAD_DIM_B, k2, swapped),
                             jnp.where(lane < HEAD_DIM_B, swapped, k2))):
        hi, lo = _split_f8(dup)
        out_ref[first + g, :, 0:128] = hi
        out_ref[first + g, :, 128:256] = lo


def _store_queries_f8(q, out_ref, g):
    hi, lo = _split_f8(q)
    for part, val in enumerate((hi, lo, hi, lo)):
        out_ref[g, part * 64:(part + 1) * 64, :] = val


def _in_proj_kernel(x_ref, mod_ref, nw_ref, wtok_ref, wfeat_ref, qnw_ref, knw_ref, knws_ref,
                    cosq_ref, sinq_ref, cosk_ref, sink_ref, grp_ref,
                    ka_ref, kb_ref, qa_ref, va_ref, ga_ref, qb_ref, vb_ref, gb_ref, kn_ref,
                    kaf_ref, kbf_ref, qaf_ref, qbf_ref):
    x = x_ref[...]
    shift = mod_ref[0:1, :]
    scale = mod_ref[1:2, :]
    ms = jnp.mean(x * x, axis=1, keepdims=True)
    h = (x * lax.rsqrt(ms + NORM_EPS)) * nw_ref[...] * (1.0 + scale) + shift
    hb = h.astype(jnp.bfloat16)

    tok = jnp.dot(hb, wtok_ref[...], preferred_element_type=jnp.float32)
    ka = tok[:, 0:WIDTH_A]
    ka_ref[...] = ka.astype(jnp.bfloat16)
    for pair in range(N_HEADS_A):
        _store_keys_f8(ka[:, pair * 128:(pair + 1) * 128] * QK_SCALE_ROOT, kaf_ref, 2 * pair)
    kn = jnp.dot(ka * ka, grp_ref[...], preferred_element_type=jnp.float32)
    kn_ref[...] = jnp.max(kn, axis=0, keepdims=True)
    kb = tok[:, 512:640]
    kbs = tok[:, 640:768]
    sq = kb * kb
    r0 = lax.rsqrt(jnp.sum(sq[:, 0:64], axis=1, keepdims=True) * (1.0 / HEAD_DIM_B) + NORM_EPS)
    r1 = lax.rsqrt(jnp.sum(sq[:, 64:128], axis=1, keepdims=True) * (1.0 / HEAD_DIM_B) + NORM_EPS)
    lane = lax.broadcasted_iota(jnp.int32, kb.shape, 1)
    r = jnp.where(lane < HEAD_DIM_B, r0, r1)
    kr = r * ((kb * knw_ref[...]) * cosk_ref[...] + (kbs * knws_ref[...]) * sink_ref[...])
    kb_ref[...] = kr.astype(jnp.bfloat16)
    _store_keys_f8(kr * QK_SCALE_ROOT, kbf_ref, 0)

    def feat(lo, hi):
        return lax.dot_general(wfeat_ref[lo:hi, :], hb, (((1,), (1,)), ((), ())),
                               preferred_element_type=jnp.float32)

    qa = feat(0, 512)
    qa_ref[...] = (qa * QK_SCALE).astype(jnp.bfloat16)
    for g in range(2 * N_HEADS_A):
        _store_queries_f8(qa[g * 64:(g + 1) * 64, :] * QK_SCALE_ROOT, qaf_ref, g)
    va = feat(512, 1024).astype(jnp.bfloat16)
    ones = jnp.ones((ONES_ROWS, va.shape[1]), jnp.bfloat16)
    for hd in range(N_HEADS_A):
        va_ref[hd, 0:DIFF_V_DIM, :] = va[hd * DIFF_V_DIM:(hd + 1) * DIFF_V_DIM, :]
        va_ref[hd, DIFF_V_DIM:DIFF_V_DIM + ONES_ROWS, :] = ones
    ga_ref[...] = _silu(feat(1024, 1536)).astype(jnp.bfloat16)
    qb = feat(1536, 2048)
    cq = cosq_ref[...]
    sq_ = sinq_ref[...]
    qnw = qnw_ref[...]
    for hd in range(N_KV_B * GQA_GROUP):
        q = qb[hd * 64:(hd + 1) * 64, :]
        rq = lax.rsqrt(jnp.mean(q * q, axis=0, keepdims=True) + NORM_EPS)
        qn = q * rq * qnw
        partner = jnp.concatenate([qn[16:32], qn[0:16], qn[48:64], qn[32:48]], axis=0)
        rot = qn * cq + partner * sq_
        qb_ref[hd * 64:(hd + 1) * 64, :] = (rot * QK_SCALE).astype(jnp.bfloat16)
        _store_queries_f8(rot * QK_SCALE_ROOT, qbf_ref, hd)
    vb = feat(2048, 2176).astype(jnp.bfloat16)
    for g in range(N_KV_B):
        vb_ref[g, 0:HEAD_DIM_B, :] = vb[g * HEAD_DIM_B:(g + 1) * HEAD_DIM_B, :]
        vb_ref[g, HEAD_DIM_B:HEAD_DIM_B + ONES_ROWS, :] = ones
    gb_ref[...] = _silu(feat(2176, 2688)).astype(jnp.bfloat16)


def _in_proj(x, mod3, norm_w, w_tok, w_feat_t, qnw_col, knw_row, knws_row, cosq, sinq, cosk, sink,
             grp):
    bsz, seq, d = x.shape
    ts = PROJ_TILE
    bf = jnp.bfloat16
    kdim = SPLIT * DIFF_HEAD_DIM
    const = lambda b, i: (0, 0)
    feat_spec = lambda rows: pl.BlockSpec((None, rows, ts), lambda b, i: (b, 0, i))
    return pl.pallas_call(
        _in_proj_kernel,
        grid=(bsz, seq // ts),
        in_specs=[
            pl.BlockSpec((None, ts, d), lambda b, i: (b, i, 0)),
            pl.BlockSpec((None, 3, d), lambda b, i: (b, 0, 0)),
            pl.BlockSpec((1, d), const),
            pl.BlockSpec(w_tok.shape, const),
            pl.BlockSpec(w_feat_t.shape, const),
            pl.BlockSpec((HEAD_DIM_B, 1), const),
            pl.BlockSpec((1, 128), const),
            pl.BlockSpec((1, 128), const),
            pl.BlockSpec((HEAD_DIM_B, ts), lambda b, i: (0, i)),
            pl.BlockSpec((HEAD_DIM_B, ts), lambda b, i: (0, i)),
            pl.BlockSpec((ts, 128), lambda b, i: (i, 0)),
            pl.BlockSpec((ts, 128), lambda b, i: (i, 0)),
            pl.BlockSpec(grp.shape, const),
        ],
        out_specs=[
            pl.BlockSpec((None, ts, WIDTH_A), lambda b, i: (b, i, 0)),
            pl.BlockSpec((None, ts, 128), lambda b, i: (b, i, 0)),
            feat_spec(512),
            pl.BlockSpec((None, N_HEADS_A, DIFF_V_DIM + ONES_ROWS, ts), lambda b, i: (b, 0, 0, i)),
            feat_spec(512), feat_spec(512),
            pl.BlockSpec((None, N_KV_B, HEAD_DIM_B + ONES_ROWS, ts), lambda b, i: (b, 0, 0, i)),
            feat_spec(512),
            pl.BlockSpec((None, None, 1, 128), lambda b, i: (b, i, 0, 0)),
            pl.BlockSpec((None, 2 * N_HEADS_A, ts, kdim), lambda b, i: (b, 0, i, 0)),
            pl.BlockSpec((None, N_KV_B, ts, kdim), lambda b, i: (b, 0, i, 0)),
            pl.BlockSpec((None, 2 * N_HEADS_A, kdim, ts), lambda b, i: (b, 0, 0, i)),
            pl.BlockSpec((None, N_KV_B * GQA_GROUP, kdim, ts), lambda b, i: (b, 0, 0, i)),
        ],
        out_shape=[
            jax.ShapeDtypeStruct((bsz, seq, WIDTH_A), bf),
            jax.ShapeDtypeStruct((bsz, seq, 128), bf),
            jax.ShapeDtypeStruct((bsz, 512, seq), bf),
            jax.ShapeDtypeStruct((bsz, N_HEADS_A, DIFF_V_DIM + ONES_ROWS, seq), bf),
            jax.ShapeDtypeStruct((bsz, 512, seq), bf),
            jax.ShapeDtypeStruct((bsz, 512, seq), bf),
            jax.ShapeDtypeStruct((bsz, N_KV_B, HEAD_DIM_B + ONES_ROWS, seq), bf),
            jax.ShapeDtypeStruct((bsz, 512, seq), bf),
            jax.ShapeDtypeStruct((bsz, seq // ts, 1, 128), jnp.float32),
            jax.ShapeDtypeStruct((bsz, 2 * N_HEADS_A, seq, kdim), F8),
            jax.ShapeDtypeStruct((bsz, N_KV_B, seq, kdim), F8),
            jax.ShapeDtypeStruct((bsz, 2 * N_HEADS_A, kdim, seq), F8),
            jax.ShapeDtypeStruct((bsz, N_KV_B * GQA_GROUP, kdim, seq), F8),
        ],
        compiler_params=pltpu.CompilerParams(
            dimension_semantics=("arbitrary", "arbitrary"), vmem_limit_bytes=VMEM_LIMIT_BYTES),
        name="in_proj",
    )(x, mod3, norm_w, w_tok, w_feat_t, qnw_col, knw_row, knws_row, cosq, sinq, cosk, sink, grp)


def _softmax_over_tiles(n_rest, first_tile, tile_fn, logits_fn, fast_logits_fn, off_fn, vt_fn,
                        fixed_ok_fn, p_refs, m_ref, acc_ref):
    chunks = [slice(c, c + LANE_CHUNK) for c in range(0, m_ref.shape[1], LANE_CHUNK)]
    m_ref[...] = jnp.full(m_ref.shape, NEG_BIG, jnp.float32)
    acc_ref[...] = jnp.zeros(acc_ref.shape, jnp.float32)

    def online_tile(j):
        off = off_fn(j)
        vt = vt_fn(j)
        for cols in chunks:
            u = logits_fn(j, cols)
            m_old = m_ref[:, cols]
            m_new = jnp.maximum(m_old, jnp.max(u, axis=0, keepdims=True) + off[:, cols])
            alpha = jnp.exp2(m_old - m_new)
            p = jnp.exp2((u - (m_new - off[:, cols])).astype(jnp.bfloat16))
            acc_ref[:, cols] = alpha * acc_ref[:, cols] + jnp.dot(
                vt, p, preferred_element_type=jnp.float32)
            m_ref[:, cols] = m_new

    online_tile(first_tile)
    fixed_ok = fixed_ok_fn(m_ref[...])

    def probs(t, slot, cols):
        j = tile_fn(t)
        shift = m_ref[:, cols] - off_fn(j)[:, cols]
        p_refs[slot][:, cols] = jnp.exp2((fast_logits_fn(j, cols) - shift).astype(jnp.bfloat16))

    def accumulate(t, slot, cols, vt):
        acc_ref[:, cols] += jnp.dot(vt, p_refs[slot][:, cols], preferred_element_type=jnp.float32)

    def fixed_step(t, slot):
        vt = vt_fn(tile_fn(t))
        for cols in chunks:
            probs(t + 1, 1 - slot, cols)
            accumulate(t, slot, cols, vt)

    @pl.when(fixed_ok)
    def _():
        for cols in chunks:
            probs(0, 0, cols)
        for t in range(n_rest - 1):
            fixed_step(t, t % 2)
        vt = vt_fn(tile_fn(n_rest - 1))
        for cols in chunks:
            accumulate(n_rest - 1, (n_rest - 1) % 2, cols, vt)

    @pl.when(jnp.logical_not(fixed_ok))
    def _():
        def body(t, carry):
            online_tile(tile_fn(t))
            return carry

        lax.fori_loop(0, n_rest, body, 0)


def _attn_diff_kernel(qt_ref, k_ref, vt_ref, gt_ref, bias_ref, qf_ref, kf_ref, slope_ref, kmax_ref,
                      sublnw_ref, lq1_ref, lk1_ref, lq2_ref, lk2_ref, o_ref,
                      qbd_ref, p0_ref, p1_ref, m_ref, acc_ref, *, lam_init):
    tq, tk = TQ_A, TK_A
    b = pl.program_id(0)
    hd = pl.program_id(1)
    qi = pl.program_id(2)
    n_kv = k_ref.shape[0] // tk
    sigma = slope_ref[hd] * LOG2E

    zeros = jnp.zeros((DIFF_HEAD_DIM, tq), jnp.bfloat16)
    qbd_ref[0:64, 0:tq] = qt_ref[0:64, :]
    qbd_ref[64:128, 0:tq] = zeros
    qbd_ref[0:64, tq:2 * tq] = zeros
    qbd_ref[64:128, tq:2 * tq] = qt_ref[64:128, :]

    lane = lax.broadcasted_iota(jnp.int32, (1, 2 * tq), 1)
    il = jnp.where(lane >= tq, lane - tq, lane).astype(jnp.float32)

    def bias(j, cols):
        kind = jnp.where(j < qi, 0, jnp.where(j > qi, 1, 2))
        return bias_ref[kind, :, cols]

    def logits(j, cols):
        j0 = pl.multiple_of(j * tk, tk)
        t = jnp.dot(k_ref[pl.ds(j0, tk), :], qbd_ref[:, cols], preferred_element_type=jnp.float32)
        return t + bias(j, cols)

    def fast_logits(j, cols):
        j0 = pl.multiple_of(j * tk, tk)
        mp = cols.start // tq
        t = jnp.dot(kf_ref[mp, pl.ds(j0, tk), :], qf_ref[mp], preferred_element_type=jnp.float32)
        return t + bias(j, cols)

    def offset(j):
        per_query = jnp.where(j > qi, 2.0 * sigma, 0.0)
        return per_query * il - sigma * (jnp.abs(j - qi) * tk).astype(jnp.float32)

    def values(j):
        return vt_ref[:, pl.ds(pl.multiple_of(j * tk, tk), tk)]

    q = qbd_ref[...].astype(jnp.float32)
    q_norm = jnp.sqrt(jnp.sum(q * q, axis=0, keepdims=True))
    k_norm0 = kmax_ref[(b * N_HEADS_A + hd) * 2]
    k_norm1 = kmax_ref[(b * N_HEADS_A + hd) * 2 + 1]
    bound = q_norm * jnp.where(lane < tq, k_norm0, k_norm1) + sigma * il
    f8_ok = ((jnp.max(q_norm) * (QK_SCALE_ROOT / QK_SCALE) < F8_SAFE)
             & (jnp.maximum(k_norm0, k_norm1) * QK_SCALE_ROOT < F8_SAFE))

    _softmax_over_tiles(n_kv - 1, qi, lambda t: jnp.where(t >= qi, t + 1, t), logits, fast_logits,
                        offset, values, lambda m1: (jnp.max(bound - m1) <= GUARD_LOG2) & f8_ok,
                        (p0_ref, p1_ref), m_ref, acc_ref)

    lam = (jnp.exp(jnp.sum(lq1_ref[...] * lk1_ref[...], axis=1, keepdims=True))
           - jnp.exp(jnp.sum(lq2_ref[...] * lk2_ref[...], axis=1, keepdims=True)) + lam_init)
    inv_l = 1.0 / acc_ref[DIFF_V_DIM:DIFF_V_DIM + 1, :]
    o1 = acc_ref[0:DIFF_V_DIM, 0:tq] * inv_l[:, 0:tq]
    o2 = acc_ref[0:DIFF_V_DIM, tq:2 * tq] * inv_l[:, tq:2 * tq]
    diff = o1 - lam * o2
    ms = jnp.mean(diff * diff, axis=0, keepdims=True)
    y = diff * lax.rsqrt(ms + NORM_EPS) * sublnw_ref[...] * (1.0 - lam_init)
    gated = y * gt_ref[...].astype(jnp.float32)
    o_ref[...] = gated.T.astype(o_ref.dtype)


def _attn_diff(qa_t, ka, va_t, ga_t, bias, qa_f8, ka_f8, slopes, kmax, subln_col, lq1, lk1, lq2, lk2,
               lam_init):
    bsz, seq, _ = ka.shape
    tq = TQ_A
    n = 2 * tq
    kdim = SPLIT * DIFF_HEAD_DIM
    vec = pl.BlockSpec((1, DIFF_HEAD_DIM), lambda b, h, i: (0, 0))
    smem = pl.BlockSpec(memory_space=pltpu.SMEM)
    return pl.pallas_call(
        functools.partial(_attn_diff_kernel, lam_init=lam_init),
        grid=(bsz, N_HEADS_A, seq // tq),
        in_specs=[
            pl.BlockSpec((None, 128, tq), lambda b, h, i: (b, h, i)),
            pl.BlockSpec((None, seq, 128), lambda b, h, i: (b, 0, h)),
            pl.BlockSpec((None, None, DIFF_V_DIM + ONES_ROWS, seq), lambda b, h, i: (b, h, 0, 0)),
            pl.BlockSpec((None, 128, tq), lambda b, h, i: (b, h, i)),
            pl.BlockSpec((None, 3, TK_A, n), lambda b, h, i: (h, 0, 0, 0)),
            pl.BlockSpec((None, None, 2, kdim, tq), lambda b, h, i: (b, h, 0, 0, i)),
            pl.BlockSpec((None, None, 2, seq, kdim), lambda b, h, i: (b, h, 0, 0, 0)),
            smem, smem,
            pl.BlockSpec((DIFF_V_DIM, 1), lambda b, h, i: (0, 0)),
            vec, vec, vec, vec,
        ],
        out_specs=pl.BlockSpec((None, tq, 128), lambda b, h, i: (b, i, h)),
        out_shape=jax.ShapeDtypeStruct((bsz, seq, WIDTH_A), jnp.bfloat16),
        scratch_shapes=[
            pltpu.VMEM((128, n), jnp.bfloat16),
            pltpu.VMEM((TK_A, n), jnp.bfloat16),
            pltpu.VMEM((TK_A, n), jnp.bfloat16),
            pltpu.VMEM((1, n), jnp.float32),
            pltpu.VMEM((DIFF_V_DIM + ONES_ROWS, n), jnp.float32),
        ],
        compiler_params=pltpu.CompilerParams(
            dimension_semantics=("arbitrary", "arbitrary", "arbitrary"),
            vmem_limit_bytes=VMEM_LIMIT_BYTES),
        name="attn_diff",
    )(qa_t, ka, va_t, ga_t, bias, qa_f8, ka_f8, slopes, kmax, subln_col, lq1, lk1, lq2, lk2)


def _attn_gqa_kernel(qt_ref, k_ref, vt_ref, gt_ref, qf_ref, kf_ref, kmax_ref, o_ref,
                     qp_ref, qpf_ref, p0_ref, p1_ref, m_ref, acc_ref):
    tq, tk = TQ_B, TK_B
    g = pl.program_id(1)
    n_kv = k_ref.shape[0] // tk
    n = GQA_GROUP * tq

    row = lax.broadcasted_iota(jnp.int32, (128, tq), 0)
    lo = g * HEAD_DIM_B
    mine = (row >= lo) & (row < lo + HEAD_DIM_B)
    for r in range(GQA_GROUP):
        q = qt_ref[r * 64:(r + 1) * 64, :].astype(jnp.float32)
        q2 = jnp.concatenate([q, q], axis=0)
        qp_ref[:, r * tq:(r + 1) * tq] = jnp.where(mine, q2, 0.0).astype(jnp.bfloat16)
        qpf_ref[:, r * tq:(r + 1) * tq] = qf_ref[r]

    def logits(j, cols):
        j0 = pl.multiple_of(j * tk, tk)
        return jnp.dot(k_ref[pl.ds(j0, tk), :], qp_ref[:, cols], preferred_element_type=jnp.float32)

    def fast_logits(j, cols):
        j0 = pl.multiple_of(j * tk, tk)
        return jnp.dot(kf_ref[pl.ds(j0, tk), :], qpf_ref[:, cols], preferred_element_type=jnp.float32)

    def values(j):
        return vt_ref[:, pl.ds(pl.multiple_of(j * tk, tk), tk)]

    no_offset = jnp.zeros((1, n), jnp.float32)
    q = qp_ref[...].astype(jnp.float32)
    q_norm = jnp.sqrt(jnp.sum(q * q, axis=0, keepdims=True))
    bound = q_norm * kmax_ref[0]
    f8_ok = ((jnp.max(q_norm) * (QK_SCALE_ROOT / QK_SCALE) < F8_SAFE)
             & (kmax_ref[0] * QK_SCALE_ROOT < F8_SAFE))

    _softmax_over_tiles(n_kv - 1, jnp.int32(0), lambda t: jnp.asarray(t, jnp.int32) + 1, logits,
                        fast_logits, lambda j: no_offset, values,
                        lambda m1: (jnp.max(bound - m1) <= GUARD_LOG2) & f8_ok,
                        (p0_ref, p1_ref), m_ref, acc_ref)

    o = acc_ref[0:HEAD_DIM_B, :] * (1.0 / acc_ref[HEAD_DIM_B:HEAD_DIM_B + 1, :])
    o = jnp.concatenate([o[:, r * tq:(r + 1) * tq] for r in range(GQA_GROUP)], axis=0)
    gated = o * gt_ref[...].astype(jnp.float32)
    o_ref[...] = gated.T.astype(o_ref.dtype)


def _attn_gqa(qb_t, kb, vb_t, gb_t, qb_f8, kb_f8, kmax):
    bsz, seq, _ = kb.shape
    tq = TQ_B
    n = GQA_GROUP * tq
    kdim = SPLIT * HEAD_DIM_B
    return pl.pallas_call(
        _attn_gqa_kernel,
        grid=(bsz, N_KV_B, seq // tq),
        in_specs=[
            pl.BlockSpec((None, 256, tq), lambda b, g, i: (b, g, i)),
            pl.BlockSpec((None, seq, 128), lambda b, g, i: (b, 0, 0)),
            pl.BlockSpec((None, None, HEAD_DIM_B + ONES_ROWS, seq), lambda b, g, i: (b, g, 0, 0)),
            pl.BlockSpec((None, 256, tq), lambda b, g, i: (b, g, i)),
            pl.BlockSpec((None, GQA_GROUP, kdim, tq), lambda b, g, i: (b, g, 0, i)),
            pl.BlockSpec((None, None, seq, kdim), lambda b, g, i: (b, g, 0, 0)),
            pl.BlockSpec(memory_space=pltpu.SMEM),
        ],
        out_specs=pl.BlockSpec((None, tq, 256), lambda b, g, i: (b, i, g)),
        out_shape=jax.ShapeDtypeStruct((bsz, seq, WIDTH_B), jnp.bfloat16),
        scratch_shapes=[
            pltpu.VMEM((128, n), jnp.bfloat16),
            pltpu.VMEM((kdim, n), F8),
            pltpu.VMEM((TK_B, n), jnp.bfloat16),
            pltpu.VMEM((TK_B, n), jnp.bfloat16),
            pltpu.VMEM((1, n), jnp.float32),
            pltpu.VMEM((HEAD_DIM_B + ONES_ROWS, n), jnp.float32),
        ],
        compiler_params=pltpu.CompilerParams(
            dimension_semantics=("arbitrary", "arbitrary", "arbitrary"),
            vmem_limit_bytes=VMEM_LIMIT_BYTES),
        name="attn_gqa",
    )(qb_t, kb, vb_t, gb_t, qb_f8, kb_f8, kmax)


def _out_proj_kernel(ga_ref, gb_ref, wa_ref, wb_ref, x_ref, mod_ref, fw_ref, o_ref):
    y = jnp.dot(ga_ref[...], wa_ref[...], preferred_element_type=jnp.float32)
    y = y + jnp.dot(gb_ref[...], wb_ref[...], preferred_element_type=jnp.float32)
    z = x_ref[...] + mod_ref[2:3, :] * y
    ms = jnp.mean(z * z, axis=1, keepdims=True)
    o_ref[...] = z * lax.rsqrt(ms + NORM_EPS) * fw_ref[...]


def _out_proj(ga, gb, w_a, w_b, x, mod3, final_w):
    bsz, seq, d = x.shape
    ts = PROJ_TILE
    const = lambda b, i: (0, 0)
    return pl.pallas_call(
        _out_proj_kernel,
        grid=(bsz, seq // ts),
        in_specs=[
            pl.BlockSpec((None, ts, WIDTH_A), lambda b, i: (b, i, 0)),
            pl.BlockSpec((None, ts, WIDTH_B), lambda b, i: (b, i, 0)),
            pl.BlockSpec(w_a.shape, const),
            pl.BlockSpec(w_b.shape, const),
            pl.BlockSpec((None, ts, d), lambda b, i: (b, i, 0)),
            pl.BlockSpec((None, 3, d), lambda b, i: (b, 0, 0)),
            pl.BlockSpec((1, d), const),
        ],
        out_specs=pl.BlockSpec((None, ts, d), lambda b, i: (b, i, 0)),
        out_shape=jax.ShapeDtypeStruct((bsz, seq, d), jnp.float32),
        compiler_params=pltpu.CompilerParams(
            dimension_semantics=("arbitrary", "arbitrary"), vmem_limit_bytes=VMEM_LIMIT_BYTES),
        name="out_proj",
    )(ga, gb, w_a, w_b, x, mod3, final_w)


def _rope_tables(seq):
    pos = np.arange(seq)
    row = (pos // GRID_W).astype(np.float32)
    col = (pos % GRID_W).astype(np.float32)
    n_freq = ROT_HALF // 2
    freqs = (1.0 / (ROPE_THETA ** (np.arange(n_freq, dtype=np.float32) * 2.0 / ROT_HALF))).astype(np.float32)
    row, col, freqs = jnp.asarray(row), jnp.asarray(col), jnp.asarray(freqs)
    ang_r = row[:, None] * freqs[None, :]
    ang_c = col[:, None] * freqs[None, :]
    cos = jnp.concatenate([jnp.cos(ang_r), jnp.cos(ang_r), jnp.cos(ang_c), jnp.cos(ang_c)], axis=1)
    sin = jnp.concatenate([-jnp.sin(ang_r), jnp.sin(ang_r), -jnp.sin(ang_c), jnp.sin(ang_c)], axis=1)
    d = np.arange(HEAD_DIM_B)
    partner = np.where((d // n_freq) % 2 == 0, d + n_freq, d - n_freq)
    return cos, sin, partner


def _alibi_tables(slopes):
    sig = (slopes * LOG2E)[:, None, None]
    jl = jnp.arange(TK_A, dtype=jnp.float32)[None, :, None]
    il = jnp.tile(jnp.arange(TQ_A, dtype=jnp.float32), 2)[None, None, :]
    colb = jnp.broadcast_to(sig * jl, (N_HEADS_A, TK_A, 2 * TQ_A))
    diagb = sig * (il - jnp.abs(il - jl))
    return jnp.stack([colb, -colb, diagb], axis=1)


def kernel(x, c, w_ada, b_ada, norm_w, w_in, lambda_q1, lambda_k1, lambda_q2, lambda_k2,
           subln_w, q_norm_w, k_norm_w, w_out, final_norm_w):
    assert w_ada.shape[0] == 1, "single-layer problem: the final norm is fused into the output projection"
    bsz, seq, d = x.shape
    assert seq % TQ_A == 0 and seq % TK_B == 0 and seq % PROJ_TILE == 0
    assert (seq // TK_A) % 2 == 0 and (seq // TK_B) % 2 == 0
    bf = jnp.bfloat16
    kdim = SPLIT * DIFF_HEAD_DIM
    cos, sin, partner = _rope_tables(seq)
    cosq, sinq = cos.T, sin.T
    cosk, sink = jnp.tile(cos, (1, N_KV_B)), jnp.tile(sin, (1, N_KV_B))
    slopes = jnp.asarray(2.0 ** (-8.0 * np.arange(1, N_HEADS_A + 1) / N_HEADS_A), jnp.float32)
    alibi = _alibi_tables(slopes)
    lam_init = 0.8 - 0.6 * math.exp(-0.3 * 0)
    grp = jnp.asarray(np.arange(WIDTH_A)[:, None] // DIFF_HEAD_DIM == np.arange(128)[None, :], jnp.float32)

    w = w_in[0]
    q_a, k_a, v_a, g_a = w[:, 0:512], w[:, 512:1024], w[:, 1024:1536], w[:, 1536:2048]
    q_b, k_b, v_b, g_b = w[:, 2048:2560], w[:, 2560:2688], w[:, 2688:2816], w[:, 2816:3328]
    partner2 = np.concatenate([partner, partner + HEAD_DIM_B])
    w_tok = jnp.concatenate([k_a, k_b, k_b[:, partner2]], axis=1).astype(bf)
    w_feat_t = jnp.concatenate([q_a, v_a, g_a, q_b, v_b, g_b], axis=1).T.astype(bf)
    knw = jnp.tile(k_norm_w[0], N_KV_B)
    wo = w_out[0].astype(bf)

    mod3 = _adaln_mod(c, w_ada[0], b_ada[0]).reshape(bsz, 3, d)
    ka, kb, qa_t, va_t, ga_t, qb_t, vb_t, gb_t, kn, ka_f8, kb_f8, qa_f8, qb_f8 = _in_proj(
        x, mod3, norm_w[0].reshape(1, d), w_tok, w_feat_t,
        q_norm_w[0].reshape(HEAD_DIM_B, 1), knw.reshape(1, 128), knw[partner2].reshape(1, 128),
        cosq, sinq, cosk, sink, grp)
    kmax_a = (jnp.sqrt(jnp.max(kn[:, :, 0, :2 * N_HEADS_A], axis=1)) * NORM_MARGIN).reshape(-1)
    kmax_b = (math.sqrt(HEAD_DIM_B) * NORM_MARGIN * jnp.max(jnp.abs(k_norm_w[0]))).reshape(1)
    oa = _attn_diff(qa_t, ka, va_t, ga_t, alibi,
                    qa_f8.reshape(bsz, N_HEADS_A, 2, kdim, seq), ka_f8.reshape(bsz, N_HEADS_A, 2, seq, kdim),
                    slopes, kmax_a, subln_w[0].reshape(DIFF_V_DIM, 1),
                    lambda_q1[0].reshape(1, -1), lambda_k1[0].reshape(1, -1),
                    lambda_q2[0].reshape(1, -1), lambda_k2[0].reshape(1, -1), lam_init)
    ob = _attn_gqa(qb_t, kb, vb_t, gb_t, qb_f8, kb_f8, kmax_b)
    return _out_proj(oa, ob, wo[:WIDTH_A], wo[WIDTH_A:], x, mod3, final_norm_w.reshape(1, d))
```

```python
import functools
import math

import numpy as np
import jax
import jax.numpy as jnp
from jax import lax
from jax.experimental import pallas as pl
from jax.experimental.pallas import tpu as pltpu

D_MODEL = 1024
N_HEADS_A = 4
DIFF_HEAD_DIM = 64
DIFF_V_DIM = 128
WIDTH_A = 512
N_KV_B = 2
GQA_GROUP = 4
HEAD_DIM_B = 64
WIDTH_B = 512
GRID_W = 64
ROPE_THETA = 10000.0
ROT_HALF = 32
NORM_EPS = 1e-6
LANE_CHUNK = 512
KEY_ALIGN = 256
KEY_CHUNK_A = 512
KEY_CHUNK_B = 256
ONES_ROWS = 16
LOG2E = 1.4426950408889634
NEG_BIG = -1e30
GUARD_LOG2 = 100.0
NORM_MARGIN = 1.01
F8 = jnp.float8_e4m3fn
F8_SAFE = 440.0
SPLIT = 4
QK_SCALE = LOG2E / math.sqrt(DIFF_HEAD_DIM)
QK_SCALE_ROOT = math.sqrt(QK_SCALE)

VMEM_LIMIT_BYTES = 56 * 1024 * 1024

PROJ_TILE = 512
TQ_A = 512
TK_A = 512
TQ_B = 256
TK_B = 512

assert DIFF_HEAD_DIM == HEAD_DIM_B and LANE_CHUNK == TQ_A


def _silu(v):
    return v * (1.0 / (1.0 + jnp.exp(-v)))


def _split_f8(v):
    hi = v.astype(F8)
    lo = (v - hi.astype(jnp.float32)).astype(F8)
    return hi, lo


def _mod_kernel(c_ref, w_ref, b_ref, o_ref):
    c = c_ref[...]
    o_ref[...] = jnp.dot(_silu(c), w_ref[...], preferred_element_type=jnp.float32,
                         precision=lax.Precision.HIGHEST) + b_ref[...]


def _adaln_mod(c, w_ada, b_ada):
    bsz, d = c.shape
    n = w_ada.shape[1]
    tn = 512
    return pl.pallas_call(
        _mod_kernel,
        grid=(n // tn,),
        in_specs=[pl.BlockSpec((bsz, d), lambda j: (0, 0)),
                  pl.BlockSpec((d, tn), lambda j: (0, j)),
                  pl.BlockSpec((1, tn), lambda j: (0, j))],
        out_specs=pl.BlockSpec((bsz, tn), lambda j: (0, j)),
        out_shape=jax.ShapeDtypeStruct((bsz, n), jnp.float32),
        name="adaln_mod",
    )(c, w_ada, b_ada.reshape(1, n))


def _store_keys_f8(k2, out_ref, first):
    lane = lax.broadcasted_iota(jnp.int32, k2.shape, 1)
    swapped = pltpu.roll(k2, HEAD_DIM_B, axis=1)
    for g, dup in enumerate((jnp.where(lane < HEAD_DIM_B, k2, swapped),
                             jnp.where(lane < HEAD_DIM_B, swapped, k2))):
        hi, lo = _split_f8(dup)
        out_ref[first + g, :, 0:128] = hi
        out_ref[first + g, :, 128:256] = lo


def _store_queries_f8(q, out_ref, g):
    hi, lo = _split_f8(q)
    for part, val in enumerate((hi, lo, hi, lo)):
        out_ref[g, part * 64:(part + 1) * 64, :] = val


def _in_proj_kernel(x_ref, mod_ref, nw_ref, wtok_ref, wfeat_ref, qnw_ref, knw_ref, knws_ref,
                    cosq_ref, sinq_ref, cosk_ref, sink_ref, grp_ref,
                    ka_ref, kb_ref, qa_ref, va_ref, ga_ref, qb_ref, vb_ref, gb_ref, kn_ref,
                    kaf_ref, kbf_ref, qaf_ref, qbf_ref):
    x = x_ref[...]
    shift = mod_ref[0:1, :]
    scale = mod_ref[1:2, :]
    ms = jnp.mean(x * x, axis=1, keepdims=True)
    h = (x * lax.rsqrt(ms + NORM_EPS)) * nw_ref[...] * (1.0 + scale) + shift
    hb = h.astype(jnp.bfloat16)

    tok = jnp.dot(hb, wtok_ref[...], preferred_element_type=jnp.float32)
    ka = tok[:, 0:WIDTH_A]
    ka_ref[...] = ka.astype(jnp.bfloat16)
    for pair in range(N_HEADS_A):
        _store_keys_f8(ka[:, pair * 128:(pair + 1) * 128] * QK_SCALE_ROOT, kaf_ref, 2 * pair)
    kn = jnp.dot(ka * ka, grp_ref[...], preferred_element_type=jnp.float32)
    kn_ref[...] = jnp.max(kn, axis=0, keepdims=True)
    kb = tok[:, 512:640]
    kbs = tok[:, 640:768]
    sq = kb * kb
    r0 = lax.rsqrt(jnp.sum(sq[:, 0:64], axis=1, keepdims=True) * (1.0 / HEAD_DIM_B) + NORM_EPS)
    r1 = lax.rsqrt(jnp.sum(sq[:, 64:128], axis=1, keepdims=True) * (1.0 / HEAD_DIM_B) + NORM_EPS)
    lane = lax.broadcasted_iota(jnp.int32, kb.shape, 1)
    r = jnp.where(lane < HEAD_DIM_B, r0, r1)
    kr = r * ((kb * knw_ref[...]) * cosk_ref[...] + (kbs * knws_ref[...]) * sink_ref[...])
    kb_ref[...] = kr.astype(jnp.bfloat16)
    _store_keys_f8(kr * QK_SCALE_ROOT, kbf_ref, 0)

    def feat(lo, hi):
        return lax.dot_general(wfeat_ref[lo:hi, :], hb, (((1,), (1,)), ((), ())),
                               preferred_element_type=jnp.float32)

    qa = feat(0, 512)
    qa_ref[...] = (qa * QK_SCALE).astype(jnp.bfloat16)
    for g in range(2 * N_HEADS_A):
        _store_queries_f8(qa[g * 64:(g + 1) * 64, :] * QK_SCALE_ROOT, qaf_ref, g)
    va = feat(512, 1024).astype(jnp.bfloat16)
    ones = jnp.ones((ONES_ROWS, va.shape[1]), jnp.bfloat16)
    for hd in range(N_HEADS_A):
        va_ref[hd, 0:DIFF_V_DIM, :] = va[hd * DIFF_V_DIM:(hd + 1) * DIFF_V_DIM, :]
        va_ref[hd, DIFF_V_DIM:DIFF_V_DIM + ONES_ROWS, :] = ones
    ga_ref[...] = _silu(feat(1024, 1536)).astype(jnp.bfloat16)
    qb = feat(1536, 2048)
    cq = cosq_ref[...]
    sq_ = sinq_ref[...]
    qnw = qnw_ref[...]
    for hd in range(N_KV_B * GQA_GROUP):
        q = qb[hd * 64:(hd + 1) * 64, :]
        rq = lax.rsqrt(jnp.mean(q * q, axis=0, keepdims=True) + NORM_EPS)
        qn = q * rq * qnw
        partner = jnp.concatenate([qn[16:32], qn[0:16], qn[48:64], qn[32:48]], axis=0)
        rot = qn * cq + partner * sq_
        qb_ref[hd * 64:(hd + 1) * 64, :] = (rot * QK_SCALE).astype(jnp.bfloat16)
        _store_queries_f8(rot * QK_SCALE_ROOT, qbf_ref, hd)
    vb = feat(2048, 2176).astype(jnp.bfloat16)
    for g in range(N_KV_B):
        vb_ref[g, 0:HEAD_DIM_B, :] = vb[g * HEAD_DIM_B:(g + 1) * HEAD_DIM_B, :]
        vb_ref[g, HEAD_DIM_B:HEAD_DIM_B + ONES_ROWS, :] = ones
    gb_ref[...] = _silu(feat(2176, 2688)).astype(jnp.bfloat16)


def _in_proj(x, mod3, norm_w, w_tok, w_feat_t, qnw_col, knw_row, knws_row, cosq, sinq, cosk, sink,
             grp):
    bsz, seq, d = x.shape
    ts = PROJ_TILE
    bf = jnp.bfloat16
    kdim = SPLIT * DIFF_HEAD_DIM
    const = lambda b, i: (0, 0)
    feat_spec = lambda rows: pl.BlockSpec((None, rows, ts), lambda b, i: (b, 0, i))
    return pl.pallas_call(
        _in_proj_kernel,
        grid=(bsz, seq // ts),
        in_specs=[
            pl.BlockSpec((None, ts, d), lambda b, i: (b, i, 0)),
            pl.BlockSpec((None, 3, d), lambda b, i: (b, 0, 0)),
            pl.BlockSpec((1, d), const),
            pl.BlockSpec(w_tok.shape, const),
            pl.BlockSpec(w_feat_t.shape, const),
            pl.BlockSpec((HEAD_DIM_B, 1), const),
            pl.BlockSpec((1, 128), const),
            pl.BlockSpec((1, 128), const),
            pl.BlockSpec((HEAD_DIM_B, ts), lambda b, i: (0, i)),
            pl.BlockSpec((HEAD_DIM_B, ts), lambda b, i: (0, i)),
            pl.BlockSpec((ts, 128), lambda b, i: (i, 0)),
            pl.BlockSpec((ts, 128), lambda b, i: (i, 0)),
            pl.BlockSpec(grp.shape, const),
        ],
        out_specs=[
            pl.BlockSpec((None, ts, WIDTH_A), lambda b, i: (b, i, 0)),
            pl.BlockSpec((None, ts, 128), lambda b, i: (b, i, 0)),
            feat_spec(512),
            pl.BlockSpec((None, N_HEADS_A, DIFF_V_DIM + ONES_ROWS, ts), lambda b, i: (b, 0, 0, i)),
            feat_spec(512), feat_spec(512),
            pl.BlockSpec((None, N_KV_B, HEAD_DIM_B + ONES_ROWS, ts), lambda b, i: (b, 0, 0, i)),
            feat_spec(512),
            pl.BlockSpec((None, None, 1, 128), lambda b, i: (b, i, 0, 0)),
            pl.BlockSpec((None, 2 * N_HEADS_A, ts, kdim), lambda b, i: (b, 0, i, 0)),
            pl.BlockSpec((None, N_KV_B, ts, kdim), lambda b, i: (b, 0, i, 0)),
            pl.BlockSpec((None, 2 * N_HEADS_A, kdim, ts), lambda b, i: (b, 0, 0, i)),
            pl.BlockSpec((None, N_KV_B * GQA_GROUP, kdim, ts), lambda b, i: (b, 0, 0, i)),
        ],
        out_shape=[
            jax.ShapeDtypeStruct((bsz, seq, WIDTH_A), bf),
            jax.ShapeDtypeStruct((bsz, seq, 128), bf),
            jax.ShapeDtypeStruct((bsz, 512, seq), bf),
            jax.ShapeDtypeStruct((bsz, N_HEADS_A, DIFF_V_DIM + ONES_ROWS, seq), bf),
            jax.ShapeDtypeStruct((bsz, 512, seq), bf),
            jax.ShapeDtypeStruct((bsz, 512, seq), bf),
            jax.ShapeDtypeStruct((bsz, N_KV_B, HEAD_DIM_B + ONES_ROWS, seq), bf),
            jax.ShapeDtypeStruct((bsz, 512, seq), bf),
            jax.ShapeDtypeStruct((bsz, seq // ts, 1, 128), jnp.float32),
            jax.ShapeDtypeStruct((bsz, 2 * N_HEADS_A, seq, kdim), F8),
            jax.ShapeDtypeStruct((bsz, N_KV_B, seq, kdim), F8),
            jax.ShapeDtypeStruct((bsz, 2 * N_HEADS_A, kdim, seq), F8),
            jax.ShapeDtypeStruct((bsz, N_KV_B * GQA_GROUP, kdim, seq), F8),
        ],
        compiler_params=pltpu.CompilerParams(
            dimension_semantics=("arbitrary", "arbitrary"), vmem_limit_bytes=VMEM_LIMIT_BYTES),
        name="in_proj",
    )(x, mod3, norm_w, w_tok, w_feat_t, qnw_col, knw_row, knws_row, cosq, sinq, cosk, sink, grp)


def _softmax_over_tiles(n_rest, first_tile, tile_fn, logits_fn, fast_logits_fn, off_fn, vt_fn,
                        fixed_ok_fn, key_chunk, s_ref, p_refs, m_ref, acc_ref):
    chunks = [slice(c, c + LANE_CHUNK) for c in range(0, m_ref.shape[1], LANE_CHUNK)]
    tk = s_ref.shape[0]
    pieces = [(slice(r, r + key_chunk), cols) for r in range(0, tk, key_chunk) for cols in chunks]
    acc_ref[...] = jnp.zeros(acc_ref.shape, jnp.float32)

    off_first = off_fn(first_tile)
    for cols in chunks:
        u = logits_fn(first_tile, cols)
        s_ref[:, cols] = u
        m_ref[:, cols] = jnp.max(u, axis=0, keepdims=True) + off_first[:, cols]
    fixed_ok = fixed_ok_fn(m_ref[...])

    def first_probs(rows, cols):
        shift = m_ref[:, cols] - off_first[:, cols]
        return jnp.exp2((s_ref[rows, cols] - shift).astype(jnp.bfloat16))

    def probs(t, slot, rows, cols):
        j = tile_fn(t)
        shift = m_ref[:, cols] - off_fn(j)[:, cols]
        p_refs[slot][rows, cols] = jnp.exp2(
            (fast_logits_fn(j, rows, cols) - shift).astype(jnp.bfloat16))

    def accumulate(t, slot, rows, cols):
        j = first_tile if t < 0 else tile_fn(t)
        acc_ref[:, cols] += jnp.dot(vt_fn(j, rows), p_refs[slot][rows, cols],
                                    preferred_element_type=jnp.float32)

    @pl.when(fixed_ok)
    def _():
        for rows, cols in pieces:
            p_refs[1][rows, cols] = first_probs(rows, cols)
        for t in range(-1, n_rest - 1):
            for rows, cols in pieces:
                probs(t + 1, (t + 1) % 2, rows, cols)
                accumulate(t, t % 2, rows, cols)
        for rows, cols in pieces:
            accumulate(n_rest - 1, (n_rest - 1) % 2, rows, cols)

    @pl.when(jnp.logical_not(fixed_ok))
    def _():
        vt = vt_fn(first_tile)
        for cols in chunks:
            acc_ref[:, cols] = jnp.dot(vt, first_probs(slice(0, tk), cols),
                                       preferred_element_type=jnp.float32)

        def body(t, carry):
            j = tile_fn(t)
            off = off_fn(j)
            vt = vt_fn(j)
            for cols in chunks:
                u = logits_fn(j, cols)
                m_old = m_ref[:, cols]
                m_new = jnp.maximum(m_old, jnp.max(u, axis=0, keepdims=True) + off[:, cols])
                alpha = jnp.exp2(m_old - m_new)
                p = jnp.exp2((u - (m_new - off[:, cols])).astype(jnp.bfloat16))
                acc_ref[:, cols] = alpha * acc_ref[:, cols] + jnp.dot(
                    vt, p, preferred_element_type=jnp.float32)
                m_ref[:, cols] = m_new
            return carry

        lax.fori_loop(0, n_rest, body, 0)


def _attn_diff_kernel(qt_ref, k_ref, vt_ref, gt_ref, bias_ref, qf_ref, kf_ref, slope_ref, kmax_ref,
                      sublnw_ref, lq1_ref, lk1_ref, lq2_ref, lk2_ref, o_ref,
                      qbd_ref, s_ref, p0_ref, p1_ref, m_ref, acc_ref, *, lam_init):
    tq, tk = TQ_A, TK_A
    b = pl.program_id(0)
    hd = pl.program_id(1)
    qi = pl.program_id(2)
    n_kv = k_ref.shape[0] // tk
    sigma = slope_ref[hd] * LOG2E

    zeros = jnp.zeros((DIFF_HEAD_DIM, tq), jnp.bfloat16)
    qbd_ref[0:64, 0:tq] = qt_ref[0:64, :]
    qbd_ref[64:128, 0:tq] = zeros
    qbd_ref[0:64, tq:2 * tq] = zeros
    qbd_ref[64:128, tq:2 * tq] = qt_ref[64:128, :]

    lane = lax.broadcasted_iota(jnp.int32, (1, 2 * tq), 1)
    il = jnp.where(lane >= tq, lane - tq, lane).astype(jnp.float32)

    whole = slice(0, tk)

    def bias(j, rows, cols):
        kind = jnp.where(j < qi, 0, jnp.where(j > qi, 1, 2))
        return bias_ref[kind, rows, cols]

    def key_rows(j, rows):
        return pl.ds(pl.multiple_of(j * tk + rows.start, KEY_ALIGN), rows.stop - rows.start)

    def logits(j, cols):
        t = jnp.dot(k_ref[key_rows(j, whole), :], qbd_ref[:, cols], preferred_element_type=jnp.float32)
        return t + bias(j, whole, cols)

    def fast_logits(j, rows, cols):
        mp = cols.start // tq
        t = jnp.dot(kf_ref[mp, key_rows(j, rows), :], qf_ref[mp], preferred_element_type=jnp.float32)
        return t + bias(j, rows, cols)

    def offset(j):
        per_query = jnp.where(j > qi, 2.0 * sigma, 0.0)
        return per_query * il - sigma * (jnp.abs(j - qi) * tk).astype(jnp.float32)

    def values(j, rows=whole):
        return vt_ref[:, key_rows(j, rows)]

    q = qbd_ref[...].astype(jnp.float32)
    q_norm = jnp.sqrt(jnp.sum(q * q, axis=0, keepdims=True))
    k_norm0 = kmax_ref[(b * N_HEADS_A + hd) * 2]
    k_norm1 = kmax_ref[(b * N_HEADS_A + hd) * 2 + 1]
    bound = q_norm * jnp.where(lane < tq, k_norm0, k_norm1) + sigma * il
    f8_ok = ((jnp.max(q_norm) * (QK_SCALE_ROOT / QK_SCALE) < F8_SAFE)
             & (jnp.maximum(k_norm0, k_norm1) * QK_SCALE_ROOT < F8_SAFE))

    _softmax_over_tiles(n_kv - 1, qi, lambda t: jnp.where(t >= qi, t + 1, t), logits, fast_logits,
                        offset, values, lambda m1: (jnp.max(bound - m1) <= GUARD_LOG2) & f8_ok,
                        KEY_CHUNK_A, s_ref, (p0_ref, p1_ref), m_ref, acc_ref)

    lam = (jnp.exp(jnp.sum(lq1_ref[...] * lk1_ref[...], axis=1, keepdims=True))
           - jnp.exp(jnp.sum(lq2_ref[...] * lk2_ref[...], axis=1, keepdims=True)) + lam_init)
    inv_l = 1.0 / acc_ref[DIFF_V_DIM:DIFF_V_DIM + 1, :]
    o1 = acc_ref[0:DIFF_V_DIM, 0:tq] * inv_l[:, 0:tq]
    o2 = acc_ref[0:DIFF_V_DIM, tq:2 * tq] * inv_l[:, tq:2 * tq]
    diff = o1 - lam * o2
    ms = jnp.mean(diff * diff, axis=0, keepdims=True)
    y = diff * lax.rsqrt(ms + NORM_EPS) * sublnw_ref[...] * (1.0 - lam_init)
    gated = y * gt_ref[...].astype(jnp.float32)
    o_ref[...] = gated.T.astype(o_ref.dtype)


def _attn_diff(qa_t, ka, va_t, ga_t, bias, qa_f8, ka_f8, slopes, kmax, subln_col, lq1, lk1, lq2, lk2,
               lam_init):
    bsz, seq, _ = ka.shape
    tq = TQ_A
    n = 2 * tq
    kdim = SPLIT * DIFF_HEAD_DIM
    vec = pl.BlockSpec((1, DIFF_HEAD_DIM), lambda b, h, i: (0, 0))
    smem = pl.BlockSpec(memory_space=pltpu.SMEM)
    return pl.pallas_call(
        functools.partial(_attn_diff_kernel, lam_init=lam_init),
        grid=(bsz, N_HEADS_A, seq // tq),
        in_specs=[
            pl.BlockSpec((None, 128, tq), lambda b, h, i: (b, h, i)),
            pl.BlockSpec((None, seq, 128), lambda b, h, i: (b, 0, h)),
            pl.BlockSpec((None, None, DIFF_V_DIM + ONES_ROWS, seq), lambda b, h, i: (b, h, 0, 0)),
            pl.BlockSpec((None, 128, tq), lambda b, h, i: (b, h, i)),
            pl.BlockSpec((None, 3, TK_A, n), lambda b, h, i: (h, 0, 0, 0)),
            pl.BlockSpec((None, None, 2, kdim, tq), lambda b, h, i: (b, h, 0, 0, i)),
            pl.BlockSpec((None, None, 2, seq, kdim), lambda b, h, i: (b, h, 0, 0, 0)),
            smem, smem,
            pl.BlockSpec((DIFF_V_DIM, 1), lambda b, h, i: (0, 0)),
            vec, vec, vec, vec,
        ],
        out_specs=pl.BlockSpec((None, tq, 128), lambda b, h, i: (b, i, h)),
        out_shape=jax.ShapeDtypeStruct((bsz, seq, WIDTH_A), jnp.bfloat16),
        scratch_shapes=[
            pltpu.VMEM((128, n), jnp.bfloat16),
            pltpu.VMEM((TK_A, n), jnp.float32),
            pltpu.VMEM((TK_A, n), jnp.bfloat16),
            pltpu.VMEM((TK_A, n), jnp.bfloat16),
            pltpu.VMEM((1, n), jnp.float32),
            pltpu.VMEM((DIFF_V_DIM + ONES_ROWS, n), jnp.float32),
        ],
        compiler_params=pltpu.CompilerParams(
            dimension_semantics=("arbitrary", "arbitrary", "arbitrary"),
            vmem_limit_bytes=VMEM_LIMIT_BYTES),
        name="attn_diff",
    )(qa_t, ka, va_t, ga_t, bias, qa_f8, ka_f8, slopes, kmax, subln_col, lq1, lk1, lq2, lk2)


def _attn_gqa_kernel(qt_ref, k_ref, vt_ref, gt_ref, qf_ref, kf_ref, kmax_ref, o_ref,
                     qp_ref, qpf_ref, s_ref, p0_ref, p1_ref, m_ref, acc_ref):
    tq, tk = TQ_B, TK_B
    g = pl.program_id(1)
    n_kv = k_ref.shape[0] // tk
    n = GQA_GROUP * tq

    row = lax.broadcasted_iota(jnp.int32, (128, tq), 0)
    lo = g * HEAD_DIM_B
    mine = (row >= lo) & (row < lo + HEAD_DIM_B)
    for r in range(GQA_GROUP):
        q = qt_ref[r * 64:(r + 1) * 64, :].astype(jnp.float32)
        q2 = jnp.concatenate([q, q], axis=0)
        qp_ref[:, r * tq:(r + 1) * tq] = jnp.where(mine, q2, 0.0).astype(jnp.bfloat16)
        qpf_ref[:, r * tq:(r + 1) * tq] = qf_ref[r]

    whole = slice(0, tk)

    def key_rows(j, rows):
        return pl.ds(pl.multiple_of(j * tk + rows.start, KEY_ALIGN), rows.stop - rows.start)

    def logits(j, cols):
        return jnp.dot(k_ref[key_rows(j, whole), :], qp_ref[:, cols], preferred_element_type=jnp.float32)

    def fast_logits(j, rows, cols):
        return jnp.dot(kf_ref[key_rows(j, rows), :], qpf_ref[:, cols], preferred_element_type=jnp.float32)

    def values(j, rows=whole):
        return vt_ref[:, key_rows(j, rows)]

    no_offset = jnp.zeros((1, n), jnp.float32)
    q = qp_ref[...].astype(jnp.float32)
    q_norm = jnp.sqrt(jnp.sum(q * q, axis=0, keepdims=True))
    bound = q_norm * kmax_ref[0]
    f8_ok = ((jnp.max(q_norm) * (QK_SCALE_ROOT / QK_SCALE) < F8_SAFE)
             & (kmax_ref[0] * QK_SCALE_ROOT < F8_SAFE))

    _softmax_over_tiles(n_kv - 1, jnp.int32(0), lambda t: jnp.asarray(t, jnp.int32) + 1, logits,
                        fast_logits, lambda j: no_offset, values,
                        lambda m1: (jnp.max(bound - m1) <= GUARD_LOG2) & f8_ok,
                        KEY_CHUNK_B, s_ref, (p0_ref, p1_ref), m_ref, acc_ref)

    o = acc_ref[0:HEAD_DIM_B, :] * (1.0 / acc_ref[HEAD_DIM_B:HEAD_DIM_B + 1, :])
    o = jnp.concatenate([o[:, r * tq:(r + 1) * tq] for r in range(GQA_GROUP)], axis=0)
    gated = o * gt_ref[...].astype(jnp.float32)
    o_ref[...] = gated.T.astype(o_ref.dtype)


def _attn_gqa(qb_t, kb, vb_t, gb_t, qb_f8, kb_f8, kmax):
    bsz, seq, _ = kb.shape
    tq = TQ_B
    n = GQA_GROUP * tq
    kdim = SPLIT * HEAD_DIM_B
    return pl.pallas_call(
        _attn_gqa_kernel,
        grid=(bsz, N_KV_B, seq // tq),
        in_specs=[
            pl.BlockSpec((None, 256, tq), lambda b, g, i: (b, g, i)),
            pl.BlockSpec((None, seq, 128), lambda b, g, i: (b, 0, 0)),
            pl.BlockSpec((None, None, HEAD_DIM_B + ONES_ROWS, seq), lambda b, g, i: (b, g, 0, 0)),
            pl.BlockSpec((None, 256, tq), lambda b, g, i: (b, g, i)),
            pl.BlockSpec((None, GQA_GROUP, kdim, tq), lambda b, g, i: (b, g, 0, i)),
            pl.BlockSpec((None, None, seq, kdim), lambda b, g, i: (b, g, 0, 0)),
            pl.BlockSpec(memory_space=pltpu.SMEM),
        ],
        out_specs=pl.BlockSpec((None, tq, 256), lambda b, g, i: (b, i, g)),
        out_shape=jax.ShapeDtypeStruct((bsz, seq, WIDTH_B), jnp.bfloat16),
        scratch_shapes=[
            pltpu.VMEM((128, n), jnp.bfloat16),
            pltpu.VMEM((kdim, n), F8),
            pltpu.VMEM((TK_B, n), jnp.float32),
            pltpu.VMEM((TK_B, n), jnp.bfloat16),
            pltpu.VMEM((TK_B, n), jnp.bfloat16),
            pltpu.VMEM((1, n), jnp.float32),
            pltpu.VMEM((HEAD_DIM_B + ONES_ROWS, n), jnp.float32),
        ],
        compiler_params=pltpu.CompilerParams(
            dimension_semantics=("arbitrary", "arbitrary", "arbitrary"),
            vmem_limit_bytes=VMEM_LIMIT_BYTES),
        name="attn_gqa",
    )(qb_t, kb, vb_t, gb_t, qb_f8, kb_f8, kmax)


def _out_proj_kernel(ga_ref, gb_ref, wa_ref, wb_ref, x_ref, mod_ref, fw_ref, o_ref):
    y = jnp.dot(ga_ref[...], wa_ref[...], preferred_element_type=jnp.float32)
    y = y + jnp.dot(gb_ref[...], wb_ref[...], preferred_element_type=jnp.float32)
    z = x_ref[...] + mod_ref[2:3, :] * y
    ms = jnp.mean(z * z, axis=1, keepdims=True)
    o_ref[...] = z * lax.rsqrt(ms + NORM_EPS) * fw_ref[...]


def _out_proj(ga, gb, w_a, w_b, x, mod3, final_w):
    bsz, seq, d = x.shape
    ts = PROJ_TILE
    const = lambda b, i: (0, 0)
    return pl.pallas_call(
        _out_proj_kernel,
        grid=(bsz, seq // ts),
        in_specs=[
            pl.BlockSpec((None, ts, WIDTH_A), lambda b, i: (b, i, 0)),
            pl.BlockSpec((None, ts, WIDTH_B), lambda b, i: (b, i, 0)),
            pl.BlockSpec(w_a.shape, const),
            pl.BlockSpec(w_b.shape, const),
            pl.BlockSpec((None, ts, d), lambda b, i: (b, i, 0)),
            pl.BlockSpec((None, 3, d), lambda b, i: (b, 0, 0)),
            pl.BlockSpec((1, d), const),
        ],
        out_specs=pl.BlockSpec((None, ts, d), lambda b, i: (b, i, 0)),
        out_shape=jax.ShapeDtypeStruct((bsz, seq, d), jnp.float32),
        compiler_params=pltpu.CompilerParams(
            dimension_semantics=("arbitrary", "arbitrary"), vmem_limit_bytes=VMEM_LIMIT_BYTES),
        name="out_proj",
    )(ga, gb, w_a, w_b, x, mod3, final_w)


def _rope_tables(seq):
    pos = np.arange(seq)
    row = (pos // GRID_W).astype(np.float32)
    col = (pos % GRID_W).astype(np.float32)
    n_freq = ROT_HALF // 2
    freqs = (1.0 / (ROPE_THETA ** (np.arange(n_freq, dtype=np.float32) * 2.0 / ROT_HALF))).astype(np.float32)
    row, col, freqs = jnp.asarray(row), jnp.asarray(col), jnp.asarray(freqs)
    ang_r = row[:, None] * freqs[None, :]
    ang_c = col[:, None] * freqs[None, :]
    cos = jnp.concatenate([jnp.cos(ang_r), jnp.cos(ang_r), jnp.cos(ang_c), jnp.cos(ang_c)], axis=1)
    sin = jnp.concatenate([-jnp.sin(ang_r), jnp.sin(ang_r), -jnp.sin(ang_c), jnp.sin(ang_c)], axis=1)
    d = np.arange(HEAD_DIM_B)
    partner = np.where((d // n_freq) % 2 == 0, d + n_freq, d - n_freq)
    return cos, sin, partner


def _alibi_tables(slopes):
    sig = (slopes * LOG2E)[:, None, None]
    jl = jnp.arange(TK_A, dtype=jnp.float32)[None, :, None]
    il = jnp.tile(jnp.arange(TQ_A, dtype=jnp.float32), 2)[None, None, :]
    colb = jnp.broadcast_to(sig * jl, (N_HEADS_A, TK_A, 2 * TQ_A))
    diagb = sig * (il - jnp.abs(il - jl))
    return jnp.stack([colb, -colb, diagb], axis=1)


def kernel(x, c, w_ada, b_ada, norm_w, w_in, lambda_q1, lambda_k1, lambda_q2, lambda_k2,
           subln_w, q_norm_w, k_norm_w, w_out, final_norm_w):
    assert w_ada.shape[0] == 1, "single-layer problem: the final norm is fused into the output projection"
    bsz, seq, d = x.shape
    assert seq % TQ_A == 0 and seq % TK_B == 0 and seq % PROJ_TILE == 0
    assert (seq // TK_A) % 2 == 0 and (seq // TK_B) % 2 == 0
    bf = jnp.bfloat16
    kdim = SPLIT * DIFF_HEAD_DIM
    cos, sin, partner = _rope_tables(seq)
    cosq, sinq = cos.T, sin.T
    cosk, sink = jnp.tile(cos, (1, N_KV_B)), jnp.tile(sin, (1, N_KV_B))
    slopes = jnp.asarray(2.0 ** (-8.0 * np.arange(1, N_HEADS_A + 1) / N_HEADS_A), jnp.float32)
    alibi = _alibi_tables(slopes)
    lam_init = 0.8 - 0.6 * math.exp(-0.3 * 0)
    grp = jnp.asarray(np.arange(WIDTH_A)[:, None] // DIFF_HEAD_DIM == np.arange(128)[None, :], jnp.float32)

    w = w_in[0]
    q_a, k_a, v_a, g_a = w[:, 0:512], w[:, 512:1024], w[:, 1024:1536], w[:, 1536:2048]
    q_b, k_b, v_b, g_b = w[:, 2048:2560], w[:, 2560:2688], w[:, 2688:2816], w[:, 2816:3328]
    partner2 = np.concatenate([partner, partner + HEAD_DIM_B])
    w_tok = jnp.concatenate([k_a, k_b, k_b[:, partner2]], axis=1).astype(bf)
    w_feat_t = jnp.concatenate([q_a, v_a, g_a, q_b, v_b, g_b], axis=1).T.astype(bf)
    knw = jnp.tile(k_norm_w[0], N_KV_B)
    wo = w_out[0].astype(bf)

    mod3 = _adaln_mod(c, w_ada[0], b_ada[0]).reshape(bsz, 3, d)
    ka, kb, qa_t, va_t, ga_t, qb_t, vb_t, gb_t, kn, ka_f8, kb_f8, qa_f8, qb_f8 = _in_proj(
        x, mod3, norm_w[0].reshape(1, d), w_tok, w_feat_t,
        q_norm_w[0].reshape(HEAD_DIM_B, 1), knw.reshape(1, 128), knw[partner2].reshape(1, 128),
        cosq, sinq, cosk, sink, grp)
    kmax_a = (jnp.sqrt(jnp.max(kn[:, :, 0, :2 * N_HEADS_A], axis=1)) * NORM_MARGIN).reshape(-1)
    kmax_b = (math.sqrt(HEAD_DIM_B) * NORM_MARGIN * jnp.max(jnp.abs(k_norm_w[0]))).reshape(1)
    oa = _attn_diff(qa_t, ka, va_t, ga_t, alibi,
                    qa_f8.reshape(bsz, N_HEADS_A, 2, kdim, seq), ka_f8.reshape(bsz, N_HEADS_A, 2, seq, kdim),
                    slopes, kmax_a, subln_w[0].reshape(DIFF_V_DIM, 1),
                    lambda_q1[0].reshape(1, -1), lambda_k1[0].reshape(1, -1),
                    lambda_q2[0].reshape(1, -1), lambda_k2[0].reshape(1, -1), lam_init)
    ob = _attn_gqa(qb_t, kb, vb_t, gb_t, qb_f8, kb_f8, kmax_b)
    return _out_proj(oa, ob, wo[:WIDTH_A], wo[WIDTH_A:], x, mod3, final_norm_w.reshape(1, d))
```

```python
import functools
import math

import numpy as np
import jax
import jax.numpy as jnp
from jax import lax
from jax.experimental import pallas as pl
from jax.experimental.pallas import tpu as pltpu

D_MODEL = 1024
N_HEADS_A = 4
DIFF_HEAD_DIM = 64
DIFF_V_DIM = 128
WIDTH_A = 512
N_KV_B = 2
GQA_GROUP = 4
HEAD_DIM_B = 64
WIDTH_B = 512
GRID_W = 64
ROPE_THETA = 10000.0
ROT_HALF = 32
NORM_EPS = 1e-6
LANE_CHUNK = 512
KEY_ALIGN = 256
KEY_CHUNK_A = 512
KEY_CHUNK_B = 256
ONES_ROWS = 16
LOG2E = 1.4426950408889634
NEG_BIG = -1e30
GUARD_LOG2 = 100.0
NORM_MARGIN = 1.01
F8 = jnp.float8_e4m3fn
F8_SAFE = 440.0
SPLIT = 4
QK_SCALE = LOG2E / math.sqrt(DIFF_HEAD_DIM)
QK_SCALE_ROOT = math.sqrt(QK_SCALE)

VMEM_LIMIT_BYTES = 56 * 1024 * 1024

PROJ_TILE = 512
TQ_A = 512
TK_A = 512
TQ_B = 512
TK_B = 512

assert DIFF_HEAD_DIM == HEAD_DIM_B and LANE_CHUNK == TQ_A


def _silu(v):
    return v * (1.0 / (1.0 + jnp.exp(-v)))


def _split_f8(v):
    hi = v.astype(F8)
    lo = (v - hi.astype(jnp.float32)).astype(F8)
    return hi, lo


def _mod_kernel(c_ref, w_ref, b_ref, o_ref):
    c = c_ref[...]
    o_ref[...] = jnp.dot(_silu(c), w_ref[...], preferred_element_type=jnp.float32,
                         precision=lax.Precision.HIGHEST) + b_ref[...]


def _adaln_mod(c, w_ada, b_ada):
    bsz, d = c.shape
    n = w_ada.shape[1]
    tn = 512
    return pl.pallas_call(
        _mod_kernel,
        grid=(n // tn,),
        in_specs=[pl.BlockSpec((bsz, d), lambda j: (0, 0)),
                  pl.BlockSpec((d, tn), lambda j: (0, j)),
                  pl.BlockSpec((1, tn), lambda j: (0, j))],
        out_specs=pl.BlockSpec((bsz, tn), lambda j: (0, j)),
        out_shape=jax.ShapeDtypeStruct((bsz, n), jnp.float32),
        name="adaln_mod",
    )(c, w_ada, b_ada.reshape(1, n))


def _store_keys_f8(k2, out_ref, first):
    lane = lax.broadcasted_iota(jnp.int32, k2.shape, 1)
    swapped = pltpu.roll(k2, HEAD_DIM_B, axis=1)
    for g, dup in enumerate((jnp.where(lane < HEAD_DIM_B, k2, swapped),
                             jnp.where(lane < HEAD_DIM_B, swapped, k2))):
        hi, lo = _split_f8(dup)
        out_ref[first + g, :, 0:128] = hi
        out_ref[first + g, :, 128:256] = lo


def _store_queries_f8(q, out_ref, g):
    hi, lo = _split_f8(q)
    for part, val in enumerate((hi, lo, hi, lo)):
        out_ref[g, part * 64:(part + 1) * 64, :] = val


def _in_proj_kernel(x_ref, mod_ref, nw_ref, wtok_ref, wfeat_ref, qnw_ref, knw_ref, knws_ref,
                    cosq_ref, sinq_ref, cosk_ref, sink_ref, grp_ref,
                    ka_ref, kb_ref, qa_ref, va_ref, ga_ref, qb_ref, vb_ref, gb_ref, kn_ref,
                    kaf_ref, kbf_ref, qaf_ref, qbf_ref):
    x = x_ref[...]
    shift = mod_ref[0:1, :]
    scale = mod_ref[1:2, :]
    ms = jnp.mean(x * x, axis=1, keepdims=True)
    h = (x * lax.rsqrt(ms + NORM_EPS)) * nw_ref[...] * (1.0 + scale) + shift
    hb = h.astype(jnp.bfloat16)

    tok = jnp.dot(hb, wtok_ref[...], preferred_element_type=jnp.float32)
    ka = tok[:, 0:WIDTH_A]
    ka_ref[...] = ka.astype(jnp.bfloat16)
    for pair in range(N_HEADS_A):
        _store_keys_f8(ka[:, pair * 128:(pair + 1) * 128] * QK_SCALE_ROOT, kaf_ref, 2 * pair)
    kn = jnp.dot(ka * ka, grp_ref[...], preferred_element_type=jnp.float32)
    kn_ref[...] = jnp.max(kn, axis=0, keepdims=True)
    kb = tok[:, 512:640]
    kbs = tok[:, 640:768]
    sq = kb * kb
    r0 = lax.rsqrt(jnp.sum(sq[:, 0:64], axis=1, keepdims=True) * (1.0 / HEAD_DIM_B) + NORM_EPS)
    r1 = lax.rsqrt(jnp.sum(sq[:, 64:128], axis=1, keepdims=True) * (1.0 / HEAD_DIM_B) + NORM_EPS)
    lane = lax.broadcasted_iota(jnp.int32, kb.shape, 1)
    r = jnp.where(lane < HEAD_DIM_B, r0, r1)
    kr = r * ((kb * knw_ref[...]) * cosk_ref[...] + (kbs * knws_ref[...]) * sink_ref[...])
    kb_ref[...] = kr.astype(jnp.bfloat16)
    _store_keys_f8(kr * QK_SCALE_ROOT, kbf_ref, 0)

    def feat(lo, hi):
        return lax.dot_general(wfeat_ref[lo:hi, :], hb, (((1,), (1,)), ((), ())),
                               preferred_element_type=jnp.float32)

    qa = feat(0, 512)
    qa_ref[...] = (qa * QK_SCALE).astype(jnp.bfloat16)
    for g in range(2 * N_HEADS_A):
        _store_queries_f8(qa[g * 64:(g + 1) * 64, :] * QK_SCALE_ROOT, qaf_ref, g)
    va = feat(512, 1024).astype(jnp.bfloat16)
    ones = jnp.ones((ONES_ROWS, va.shape[1]), jnp.bfloat16)
    for hd in range(N_HEADS_A):
        va_ref[hd, 0:DIFF_V_DIM, :] = va[hd * DIFF_V_DIM:(hd + 1) * DIFF_V_DIM, :]
        va_ref[hd, DIFF_V_DIM:DIFF_V_DIM + ONES_ROWS, :] = ones
    ga_ref[...] = _silu(feat(1024, 1536)).astype(jnp.bfloat16)
    qb = feat(1536, 2048)
    cq = cosq_ref[...]
    sq_ = sinq_ref[...]
    qnw = qnw_ref[...]
    for hd in range(N_KV_B * GQA_GROUP):
        q = qb[hd * 64:(hd + 1) * 64, :]
        rq = lax.rsqrt(jnp.mean(q * q, axis=0, keepdims=True) + NORM_EPS)
        qn = q * rq * qnw
        partner = jnp.concatenate([qn[16:32], qn[0:16], qn[48:64], qn[32:48]], axis=0)
        rot = qn * cq + partner * sq_
        qb_ref[hd * 64:(hd + 1) * 64, :] = (rot * QK_SCALE).astype(jnp.bfloat16)
        _store_queries_f8(rot * QK_SCALE_ROOT, qbf_ref, hd)
    vb = feat(2048, 2176).astype(jnp.bfloat16)
    for g in range(N_KV_B):
        vb_ref[g, 0:HEAD_DIM_B, :] = vb[g * HEAD_DIM_B:(g + 1) * HEAD_DIM_B, :]
        vb_ref[g, HEAD_DIM_B:HEAD_DIM_B + ONES_ROWS, :] = ones
    gb_ref[...] = _silu(feat(2176, 2688)).astype(jnp.bfloat16)


def _in_proj(x, mod3, norm_w, w_tok, w_feat_t, qnw_col, knw_row, knws_row, cosq, sinq, cosk, sink,
             grp):
    bsz, seq, d = x.shape
    ts = PROJ_TILE
    bf = jnp.bfloat16
    kdim = SPLIT * DIFF_HEAD_DIM
    const = lambda b, i: (0, 0)
    feat_spec = lambda rows: pl.BlockSpec((None, rows, ts), lambda b, i: (b, 0, i))
    return pl.pallas_call(
        _in_proj_kernel,
        grid=(bsz, seq // ts),
        in_specs=[
            pl.BlockSpec((None, ts, d), lambda b, i: (b, i, 0)),
            pl.BlockSpec((None, 3, d), lambda b, i: (b, 0, 0)),
            pl.BlockSpec((1, d), const),
            pl.BlockSpec(w_tok.shape, const),
            pl.BlockSpec(w_feat_t.shape, const),
            pl.BlockSpec((HEAD_DIM_B, 1), const),
            pl.BlockSpec((1, 128), const),
            pl.BlockSpec((1, 128), const),
            pl.BlockSpec((HEAD_DIM_B, ts), lambda b, i: (0, i)),
            pl.BlockSpec((HEAD_DIM_B, ts), lambda b, i: (0, i)),
            pl.BlockSpec((ts, 128), lambda b, i: (i, 0)),
            pl.BlockSpec((ts, 128), lambda b, i: (i, 0)),
            pl.BlockSpec(grp.shape, const),
        ],
        out_specs=[
            pl.BlockSpec((None, ts, WIDTH_A), lambda b, i: (b, i, 0)),
            pl.BlockSpec((None, ts, 128), lambda b, i: (b, i, 0)),
            feat_spec(512),
            pl.BlockSpec((None, N_HEADS_A, DIFF_V_DIM + ONES_ROWS, ts), lambda b, i: (b, 0, 0, i)),
            feat_spec(512), feat_spec(512),
            pl.BlockSpec((None, N_KV_B, HEAD_DIM_B + ONES_ROWS, ts), lambda b, i: (b, 0, 0, i)),
            feat_spec(512),
            pl.BlockSpec((None, None, 1, 128), lambda b, i: (b, i, 0, 0)),
            pl.BlockSpec((None, 2 * N_HEADS_A, ts, kdim), lambda b, i: (b, 0, i, 0)),
            pl.BlockSpec((None, N_KV_B, ts, kdim), lambda b, i: (b, 0, i, 0)),
            pl.BlockSpec((None, 2 * N_HEADS_A, kdim, ts), lambda b, i: (b, 0, 0, i)),
            pl.BlockSpec((None, N_KV_B * GQA_GROUP, kdim, ts), lambda b, i: (b, 0, 0, i)),
        ],
        out_shape=[
            jax.ShapeDtypeStruct((bsz, seq, WIDTH_A), bf),
            jax.ShapeDtypeStruct((bsz, seq, 128), bf),
            jax.ShapeDtypeStruct((bsz, 512, seq), bf),
            jax.ShapeDtypeStruct((bsz, N_HEADS_A, DIFF_V_DIM + ONES_ROWS, seq), bf),
            jax.ShapeDtypeStruct((bsz, 512, seq), bf),
            jax.ShapeDtypeStruct((bsz, 512, seq), bf),
            jax.ShapeDtypeStruct((bsz, N_KV_B, HEAD_DIM_B + ONES_ROWS, seq), bf),
            jax.ShapeDtypeStruct((bsz, 512, seq), bf),
            jax.ShapeDtypeStruct((bsz, seq // ts, 1, 128), jnp.float32),
            jax.ShapeDtypeStruct((bsz, 2 * N_HEADS_A, seq, kdim), F8),
            jax.ShapeDtypeStruct((bsz, N_KV_B, seq, kdim), F8),
            jax.ShapeDtypeStruct((bsz, 2 * N_HEADS_A, kdim, seq), F8),
            jax.ShapeDtypeStruct((bsz, N_KV_B * GQA_GROUP, kdim, seq), F8),
        ],
        compiler_params=pltpu.CompilerParams(
            dimension_semantics=("arbitrary", "arbitrary"), vmem_limit_bytes=VMEM_LIMIT_BYTES),
        name="in_proj",
    )(x, mod3, norm_w, w_tok, w_feat_t, qnw_col, knw_row, knws_row, cosq, sinq, cosk, sink, grp)


def _softmax_over_tiles(n_rest, first_tile, tile_fn, logits_fn, off_fn, vt_fn,
                        fast_logits_fn, fast_off_fn, fast_vt_fn,
                        f8_ok, bound_ok_fn, key_chunk, s_ref, p_refs, m_ref, acc_ref):
    chunks = [slice(c, c + LANE_CHUNK) for c in range(0, m_ref.shape[1], LANE_CHUNK)]
    tk = s_ref.shape[0]
    pieces = [(slice(r, r + key_chunk), cols) for r in range(0, tk, key_chunk) for cols in chunks]
    acc_ref[...] = jnp.zeros(acc_ref.shape, jnp.float32)

    off_first = off_fn(first_tile)

    for cols in chunks:
        u = logits_fn(first_tile, cols)
        s_ref[:, cols] = u
        m_ref[:, cols] = jnp.max(u, axis=0, keepdims=True) + off_first[:, cols]
    fixed_ok = bound_ok_fn(m_ref[...]) & f8_ok

    def first_probs(rows, cols):
        shift = m_ref[:, cols] - off_first[:, cols]
        return jnp.exp2((s_ref[rows, cols] - shift).astype(jnp.bfloat16))

    def probs(t, slot, rows, cols):
        j = tile_fn(t)
        shift = m_ref[:, cols] - fast_off_fn(j)[:, cols]
        p_refs[slot][rows, cols] = jnp.exp2(
            (fast_logits_fn(j, rows, cols) - shift).astype(jnp.bfloat16))

    def tile_values(t):
        if t < 0:
            return {rows.start: vt_fn(first_tile, rows) for rows, _ in pieces}
        return {rows.start: fast_vt_fn(tile_fn(t), rows) for rows, _ in pieces}

    def accumulate(slot, rows, cols, vts):
        acc_ref[:, cols] += jnp.dot(vts[rows.start], p_refs[slot][rows, cols],
                                    preferred_element_type=jnp.float32)

    @pl.when(fixed_ok)
    def _():
        for rows, cols in pieces:
            p_refs[1][rows, cols] = first_probs(rows, cols)
        for t in range(-1, n_rest - 1):
            vts = tile_values(t)
            for rows, cols in pieces:
                probs(t + 1, (t + 1) % 2, rows, cols)
                accumulate(t % 2, rows, cols, vts)
        vts = tile_values(n_rest - 1)
        for rows, cols in pieces:
            accumulate((n_rest - 1) % 2, rows, cols, vts)

    @pl.when(jnp.logical_not(fixed_ok))
    def _():
        vt = vt_fn(first_tile)
        for cols in chunks:
            acc_ref[:, cols] = jnp.dot(vt, first_probs(slice(0, tk), cols),
                                       preferred_element_type=jnp.float32)

        def body(t, carry):
            j = tile_fn(t)
            off = off_fn(j)
            vt = vt_fn(j)
            for cols in chunks:
                u = logits_fn(j, cols)
                m_old = m_ref[:, cols]
                m_new = jnp.maximum(m_old, jnp.max(u, axis=0, keepdims=True) + off[:, cols])
                alpha = jnp.exp2(m_old - m_new)
                p = jnp.exp2((u - (m_new - off[:, cols])).astype(jnp.bfloat16))
                acc_ref[:, cols] = alpha * acc_ref[:, cols] + jnp.dot(
                    vt, p, preferred_element_type=jnp.float32)
                m_ref[:, cols] = m_new
            return carry

        lax.fori_loop(0, n_rest, body, 0)


def _attn_diff_kernel(qt_ref, k_ref, vt_ref, gt_ref, bias_ref, qf_ref, kf_ref, slope_ref, kmax_ref,
                      sublnw_ref, lq1_ref, lk1_ref, lq2_ref, lk2_ref, o_ref,
                      qbd_ref, s_ref, p0_ref, p1_ref, m_ref, acc_ref, *, lam_init):
    tq, tk = TQ_A, TK_A
    b = pl.program_id(0)
    hd = pl.program_id(1)
    qi = pl.program_id(2)
    n_kv = k_ref.shape[0] // tk
    sigma = slope_ref[hd] * LOG2E

    zeros = jnp.zeros((DIFF_HEAD_DIM, tq), jnp.bfloat16)
    qbd_ref[0:64, 0:tq] = qt_ref[0:64, :]
    qbd_ref[64:128, 0:tq] = zeros
    qbd_ref[0:64, tq:2 * tq] = zeros
    qbd_ref[64:128, tq:2 * tq] = qt_ref[64:128, :]

    lane = lax.broadcasted_iota(jnp.int32, (1, 2 * tq), 1)
    il = jnp.where(lane >= tq, lane - tq, lane).astype(jnp.float32)

    whole = slice(0, tk)

    def bias(j, rows, cols):
        kind = jnp.where(j < qi, 0, jnp.where(j > qi, 1, 2))
        return bias_ref[kind, rows, cols]

    def key_rows(j, rows):
        return pl.ds(pl.multiple_of(j * tk + rows.start, KEY_ALIGN), rows.stop - rows.start)

    def logits(j, cols):
        t = jnp.dot(k_ref[key_rows(j, whole), :], qbd_ref[:, cols], preferred_element_type=jnp.float32)
        return t + bias(j, whole, cols)

    def offset(j):
        per_query = jnp.where(j > qi, 2.0 * sigma, 0.0)
        return per_query * il - sigma * (jnp.abs(j - qi) * tk).astype(jnp.float32)

    def values(j, rows=whole):
        return vt_ref[:, key_rows(j, rows)]

    def fast_logits(j, rows, cols):
        mp = cols.start // tq
        return jnp.dot(kf_ref[mp, key_rows(j, rows), :], qf_ref[mp], preferred_element_type=jnp.float32)

    def fast_offset(j):
        return offset(j) + jnp.where(j < qi, sigma * tk, 0.0)

    jl = lax.broadcasted_iota(jnp.int32, (1, tk), 1).astype(jnp.float32)
    factor_left = jnp.exp2(sigma * (jl - tk))
    factor_right = jnp.exp2(-sigma * jl)

    def fast_values(j, rows):
        factor = jnp.where(j < qi, factor_left[:, rows], factor_right[:, rows])
        return (values(j, rows).astype(jnp.float32) * factor).astype(jnp.bfloat16)

    q = qbd_ref[...].astype(jnp.float32)
    q_norm = jnp.sqrt(jnp.sum(q * q, axis=0, keepdims=True))
    k_norm0 = kmax_ref[(b * N_HEADS_A + hd) * 2]
    k_norm1 = kmax_ref[(b * N_HEADS_A + hd) * 2 + 1]
    bound = q_norm * jnp.where(lane < tq, k_norm0, k_norm1) + sigma * il
    f8_ok = ((jnp.max(q_norm) * (QK_SCALE_ROOT / QK_SCALE) < F8_SAFE)
             & (jnp.maximum(k_norm0, k_norm1) * QK_SCALE_ROOT < F8_SAFE))

    _softmax_over_tiles(n_kv - 1, qi, lambda t: jnp.where(t >= qi, t + 1, t), logits, offset, values,
                        fast_logits, fast_offset, fast_values,
                        f8_ok, lambda m1: jnp.max(bound - m1) <= GUARD_LOG2,
                        KEY_CHUNK_A, s_ref, (p0_ref, p1_ref), m_ref, acc_ref)

    lam = (jnp.exp(jnp.sum(lq1_ref[...] * lk1_ref[...], axis=1, keepdims=True))
           - jnp.exp(jnp.sum(lq2_ref[...] * lk2_ref[...], axis=1, keepdims=True)) + lam_init)
    inv_l = 1.0 / acc_ref[DIFF_V_DIM:DIFF_V_DIM + 1, :]
    o1 = acc_ref[0:DIFF_V_DIM, 0:tq] * inv_l[:, 0:tq]
    o2 = acc_ref[0:DIFF_V_DIM, tq:2 * tq] * inv_l[:, tq:2 * tq]
    diff = o1 - lam * o2
    ms = jnp.mean(diff * diff, axis=0, keepdims=True)
    y = diff * lax.rsqrt(ms + NORM_EPS) * sublnw_ref[...] * (1.0 - lam_init)
    gated = y * gt_ref[...].astype(jnp.float32)
    o_ref[...] = gated.T.astype(o_ref.dtype)


def _attn_diff(qa_t, ka, va_t, ga_t, bias, qa_f8, ka_f8, slopes, kmax, subln_col, lq1, lk1, lq2, lk2,
               lam_init):
    bsz, seq, _ = ka.shape
    tq = TQ_A
    n = 2 * tq
    kdim = SPLIT * DIFF_HEAD_DIM
    vec = pl.BlockSpec((1, DIFF_HEAD_DIM), lambda b, h, i: (0, 0))
    smem = pl.BlockSpec(memory_space=pltpu.SMEM)
    return pl.pallas_call(
        functools.partial(_attn_diff_kernel, lam_init=lam_init),
        grid=(bsz, N_HEADS_A, seq // tq),
        in_specs=[
            pl.BlockSpec((None, 128, tq), lambda b, h, i: (b, h, i)),
            pl.BlockSpec((None, seq, 128), lambda b, h, i: (b, 0, h)),
            pl.BlockSpec((None, None, DIFF_V_DIM + ONES_ROWS, seq), lambda b, h, i: (b, h, 0, 0)),
            pl.BlockSpec((None, 128, tq), lambda b, h, i: (b, h, i)),
            pl.BlockSpec((None, 3, TK_A, n), lambda b, h, i: (h, 0, 0, 0)),
            pl.BlockSpec((None, None, 2, kdim, tq), lambda b, h, i: (b, h, 0, 0, i)),
            pl.BlockSpec((None, None, 2, seq, kdim), lambda b, h, i: (b, h, 0, 0, 0)),
            smem, smem,
            pl.BlockSpec((DIFF_V_DIM, 1), lambda b, h, i: (0, 0)),
            vec, vec, vec, vec,
        ],
        out_specs=pl.BlockSpec((None, tq, 128), lambda b, h, i: (b, i, h)),
        out_shape=jax.ShapeDtypeStruct((bsz, seq, WIDTH_A), jnp.bfloat16),
        scratch_shapes=[
            pltpu.VMEM((128, n), jnp.bfloat16),
            pltpu.VMEM((TK_A, n), jnp.float32),
            pltpu.VMEM((TK_A, n), jnp.bfloat16),
            pltpu.VMEM((TK_A, n), jnp.bfloat16),
            pltpu.VMEM((1, n), jnp.float32),
            pltpu.VMEM((DIFF_V_DIM + ONES_ROWS, n), jnp.float32),
        ],
        compiler_params=pltpu.CompilerParams(
            dimension_semantics=("arbitrary", "arbitrary", "arbitrary"),
            vmem_limit_bytes=VMEM_LIMIT_BYTES),
        name="attn_diff",
    )(qa_t, ka, va_t, ga_t, bias, qa_f8, ka_f8, slopes, kmax, subln_col, lq1, lk1, lq2, lk2)


def _attn_gqa_kernel(qt_ref, k_ref, vt_ref, gt_ref, qf_ref, kf_ref, kmax_ref, o_ref,
                     qp_ref, qpf_ref, s_ref, p0_ref, p1_ref, m_ref, acc_ref):
    tq, tk = TQ_B, TK_B
    g = pl.program_id(1)
    n_kv = k_ref.shape[0] // tk
    n = GQA_GROUP * tq

    row = lax.broadcasted_iota(jnp.int32, (128, tq), 0)
    lo = g * HEAD_DIM_B
    mine = (row >= lo) & (row < lo + HEAD_DIM_B)
    for r in range(GQA_GROUP):
        q = qt_ref[r * 64:(r + 1) * 64, :].astype(jnp.float32)
        q2 = jnp.concatenate([q, q], axis=0)
        qp_ref[:, r * tq:(r + 1) * tq] = jnp.where(mine, q2, 0.0).astype(jnp.bfloat16)
        qpf_ref[:, r * tq:(r + 1) * tq] = qf_ref[r]

    whole = slice(0, tk)

    def key_rows(j, rows):
        return pl.ds(pl.multiple_of(j * tk + rows.start, KEY_ALIGN), rows.stop - rows.start)

    def logits(j, cols):
        return jnp.dot(k_ref[key_rows(j, whole), :], qp_ref[:, cols], preferred_element_type=jnp.float32)

    def fast_logits(j, rows, cols):
        return jnp.dot(kf_ref[key_rows(j, rows), :], qpf_ref[:, cols], preferred_element_type=jnp.float32)

    def values(j, rows=whole):
        return vt_ref[:, key_rows(j, rows)]

    no_offset = jnp.zeros((1, n), jnp.float32)
    q = qp_ref[...].astype(jnp.float32)
    q_norm = jnp.sqrt(jnp.sum(q * q, axis=0, keepdims=True))
    bound = q_norm * kmax_ref[0]
    f8_ok = ((jnp.max(q_norm) * (QK_SCALE_ROOT / QK_SCALE) < F8_SAFE)
             & (kmax_ref[0] * QK_SCALE_ROOT < F8_SAFE))

    _softmax_over_tiles(n_kv - 1, jnp.int32(0), lambda t: jnp.asarray(t, jnp.int32) + 1,
                        logits, lambda j: no_offset, values,
                        fast_logits, lambda j: no_offset, values,
                        f8_ok, lambda m1: jnp.max(bound - m1) <= GUARD_LOG2,
                        KEY_CHUNK_B, s_ref, (p0_ref, p1_ref), m_ref, acc_ref)

    o = acc_ref[0:HEAD_DIM_B, :] * (1.0 / acc_ref[HEAD_DIM_B:HEAD_DIM_B + 1, :])
    o = jnp.concatenate([o[:, r * tq:(r + 1) * tq] for r in range(GQA_GROUP)], axis=0)
    gated = o * gt_ref[...].astype(jnp.float32)
    o_ref[...] = gated.T.astype(o_ref.dtype)


def _attn_gqa(qb_t, kb, vb_t, gb_t, qb_f8, kb_f8, kmax):
    bsz, seq, _ = kb.shape
    tq = TQ_B
    n = GQA_GROUP * tq
    kdim = SPLIT * HEAD_DIM_B
    return pl.pallas_call(
        _attn_gqa_kernel,
        grid=(bsz, N_KV_B, seq // tq),
        in_specs=[
            pl.BlockSpec((None, 256, tq), lambda b, g, i: (b, g, i)),
            pl.BlockSpec((None, seq, 128), lambda b, g, i: (b, 0, 0)),
            pl.BlockSpec((None, None, HEAD_DIM_B + ONES_ROWS, seq), lambda b, g, i: (b, g, 0, 0)),
            pl.BlockSpec((None, 256, tq), lambda b, g, i: (b, g, i)),
            pl.BlockSpec((None, GQA_GROUP, kdim, tq), lambda b, g, i: (b, g, 0, i)),
            pl.BlockSpec((None, None, seq, kdim), lambda b, g, i: (b, g, 0, 0)),
            pl.BlockSpec(memory_space=pltpu.SMEM),
        ],
        out_specs=pl.BlockSpec((None, tq, 256), lambda b, g, i: (b, i, g)),
        out_shape=jax.ShapeDtypeStruct((bsz, seq, WIDTH_B), jnp.bfloat16),
        scratch_shapes=[
            pltpu.VMEM((128, n), jnp.bfloat16),
            pltpu.VMEM((kdim, n), F8),
            pltpu.VMEM((TK_B, n), jnp.float32),
            pltpu.VMEM((TK_B, n), jnp.bfloat16),
            pltpu.VMEM((TK_B, n), jnp.bfloat16),
            pltpu.VMEM((1, n), jnp.float32),
            pltpu.VMEM((HEAD_DIM_B + ONES_ROWS, n), jnp.float32),
        ],
        compiler_params=pltpu.CompilerParams(
            dimension_semantics=("arbitrary", "arbitrary", "arbitrary"),
            vmem_limit_bytes=VMEM_LIMIT_BYTES),
        name="attn_gqa",
    )(qb_t, kb, vb_t, gb_t, qb_f8, kb_f8, kmax)


def _out_proj_kernel(ga_ref, gb_ref, wa_ref, wb_ref, x_ref, mod_ref, fw_ref, o_ref):
    y = jnp.dot(ga_ref[...], wa_ref[...], preferred_element_type=jnp.float32)
    y = y + jnp.dot(gb_ref[...], wb_ref[...], preferred_element_type=jnp.float32)
    z = x_ref[...] + mod_ref[2:3, :] * y
    ms = jnp.mean(z * z, axis=1, keepdims=True)
    o_ref[...] = z * lax.rsqrt(ms + NORM_EPS) * fw_ref[...]


def _out_proj(ga, gb, w_a, w_b, x, mod3, final_w):
    bsz, seq, d = x.shape
    ts = PROJ_TILE
    const = lambda b, i: (0, 0)
    return pl.pallas_call(
        _out_proj_kernel,
        grid=(bsz, seq // ts),
        in_specs=[
            pl.BlockSpec((None, ts, WIDTH_A), lambda b, i: (b, i, 0)),
            pl.BlockSpec((None, ts, WIDTH_B), lambda b, i: (b, i, 0)),
            pl.BlockSpec(w_a.shape, const),
            pl.BlockSpec(w_b.shape, const),
            pl.BlockSpec((None, ts, d), lambda b, i: (b, i, 0)),
            pl.BlockSpec((None, 3, d), lambda b, i: (b, 0, 0)),
            pl.BlockSpec((1, d), const),
        ],
        out_specs=pl.BlockSpec((None, ts, d), lambda b, i: (b, i, 0)),
        out_shape=jax.ShapeDtypeStruct((bsz, seq, d), jnp.float32),
        compiler_params=pltpu.CompilerParams(
            dimension_semantics=("arbitrary", "arbitrary"), vmem_limit_bytes=VMEM_LIMIT_BYTES),
        name="out_proj",
    )(ga, gb, w_a, w_b, x, mod3, final_w)


def _rope_tables(seq):
    pos = np.arange(seq)
    row = (pos // GRID_W).astype(np.float32)
    col = (pos % GRID_W).astype(np.float32)
    n_freq = ROT_HALF // 2
    freqs = (1.0 / (ROPE_THETA ** (np.arange(n_freq, dtype=np.float32) * 2.0 / ROT_HALF))).astype(np.float32)
    row, col, freqs = jnp.asarray(row), jnp.asarray(col), jnp.asarray(freqs)
    ang_r = row[:, None] * freqs[None, :]
    ang_c = col[:, None] * freqs[None, :]
    cos = jnp.concatenate([jnp.cos(ang_r), jnp.cos(ang_r), jnp.cos(ang_c), jnp.cos(ang_c)], axis=1)
    sin = jnp.concatenate([-jnp.sin(ang_r), jnp.sin(ang_r), -jnp.sin(ang_c), jnp.sin(ang_c)], axis=1)
    d = np.arange(HEAD_DIM_B)
    partner = np.where((d // n_freq) % 2 == 0, d + n_freq, d - n_freq)
    return cos, sin, partner


def _alibi_tables(slopes):
    sig = (slopes * LOG2E)[:, None, None]
    jl = jnp.arange(TK_A, dtype=jnp.float32)[None, :, None]
    il = jnp.tile(jnp.arange(TQ_A, dtype=jnp.float32), 2)[None, None, :]
    colb = jnp.broadcast_to(sig * jl, (N_HEADS_A, TK_A, 2 * TQ_A))
    diagb = sig * (il - jnp.abs(il - jl))
    return jnp.stack([colb, -colb, diagb], axis=1)


def kernel(x, c, w_ada, b_ada, norm_w, w_in, lambda_q1, lambda_k1, lambda_q2, lambda_k2,
           subln_w, q_norm_w, k_norm_w, w_out, final_norm_w):
    assert w_ada.shape[0] == 1, "single-layer problem: the final norm is fused into the output projection"
    bsz, seq, d = x.shape
    assert seq % TQ_A == 0 and seq % TK_B == 0 and seq % PROJ_TILE == 0
    assert (seq // TK_A) % 2 == 0 and (seq // TK_B) % 2 == 0
    bf = jnp.bfloat16
    kdim = SPLIT * DIFF_HEAD_DIM
    cos, sin, partner = _rope_tables(seq)
    cosq, sinq = cos.T, sin.T
    cosk, sink = jnp.tile(cos, (1, N_KV_B)), jnp.tile(sin, (1, N_KV_B))
    slopes = jnp.asarray(2.0 ** (-8.0 * np.arange(1, N_HEADS_A + 1) / N_HEADS_A), jnp.float32)
    alibi = _alibi_tables(slopes)
    lam_init = 0.8 - 0.6 * math.exp(-0.3 * 0)
    grp = jnp.asarray(np.arange(WIDTH_A)[:, None] // DIFF_HEAD_DIM == np.arange(128)[None, :], jnp.float32)

    w = w_in[0]
    q_a, k_a, v_a, g_a = w[:, 0:512], w[:, 512:1024], w[:, 1024:1536], w[:, 1536:2048]
    q_b, k_b, v_b, g_b = w[:, 2048:2560], w[:, 2560:2688], w[:, 2688:2816], w[:, 2816:3328]
    partner2 = np.concatenate([partner, partner + HEAD_DIM_B])
    w_tok = jnp.concatenate([k_a, k_b, k_b[:, partner2]], axis=1).astype(bf)
    w_feat_t = jnp.concatenate([q_a, v_a, g_a, q_b, v_b, g_b], axis=1).T.astype(bf)
    knw = jnp.tile(k_norm_w[0], N_KV_B)
    wo = w_out[0].astype(bf)

    mod3 = _adaln_mod(c, w_ada[0], b_ada[0]).reshape(bsz, 3, d)
    ka, kb, qa_t, va_t, ga_t, qb_t, vb_t, gb_t, kn, ka_f8, kb_f8, qa_f8, qb_f8 = _in_proj(
        x, mod3, norm_w[0].reshape(1, d), w_tok, w_feat_t,
        q_norm_w[0].reshape(HEAD_DIM_B, 1), knw.reshape(1, 128), knw[partner2].reshape(1, 128),
        cosq, sinq, cosk, sink, grp)
    kmax_a = (jnp.sqrt(jnp.max(kn[:, :, 0, :2 * N_HEADS_A], axis=1)) * NORM_MARGIN).reshape(-1)
    kmax_b = (math.sqrt(HEAD_DIM_B) * NORM_MARGIN * jnp.max(jnp.abs(k_norm_w[0]))).reshape(1)
    oa = _attn_diff(qa_t, ka, va_t, ga_t, alibi,
                    qa_f8.reshape(bsz, N_HEADS_A, 2, kdim, seq), ka_f8.reshape(bsz, N_HEADS_A, 2, seq, kdim),
                    slopes, kmax_a, subln_w[0].reshape(DIFF_V_DIM, 1),
                    lambda_q1[0].reshape(1, -1), lambda_k1[0].reshape(1, -1),
                    lambda_q2[0].reshape(1, -1), lambda_k2[0].reshape(1, -1), lam_init)
    ob = _attn_gqa(qb_t, kb, vb_t, gb_t, qb_f8, kb_f8, kmax_b)
    return _out_proj(oa, ob, wo[:WIDTH_A], wo[WIDTH_A:], x, mod3, final_norm_w.reshape(1, d))
```

```python
import functools
import math

import numpy as np
import jax
import jax.numpy as jnp
from jax import lax
from jax.experimental import pallas as pl
from jax.experimental.pallas import tpu as pltpu

D_MODEL = 1024
N_HEADS_A = 4
DIFF_HEAD_DIM = 64
DIFF_V_DIM = 128
WIDTH_A = 512
N_KV_B = 2
GQA_GROUP = 4
HEAD_DIM_B = 64
WIDTH_B = 512
GRID_W = 64
ROPE_THETA = 10000.0
ROT_HALF = 32
NORM_EPS = 1e-6
LANE_CHUNK = 512
KEY_ALIGN = 256
KEY_CHUNK_A = 512
KEY_CHUNK_B = 256
ONES_ROWS = 16
LOG2E = 1.4426950408889634
NEG_BIG = -1e30
GUARD_LOG2 = 100.0
NORM_MARGIN = 1.01
F8 = jnp.float8_e4m3fn
F8_SAFE = 440.0
SPLIT = 4
QK_SCALE = LOG2E / math.sqrt(DIFF_HEAD_DIM)
QK_SCALE_ROOT = math.sqrt(QK_SCALE)

VMEM_LIMIT_BYTES = 56 * 1024 * 1024

PROJ_TILE = 1024
TQ_A = 512
TK_A = 512
TQ_B = 512
TK_B = 512

assert DIFF_HEAD_DIM == HEAD_DIM_B and LANE_CHUNK == TQ_A


def _silu(v):
    return v * (1.0 / (1.0 + jnp.exp(-v)))


def _split_f8(v):
    hi = v.astype(F8)
    lo = (v - hi.astype(jnp.float32)).astype(F8)
    return hi, lo


def _mod_kernel(c_ref, w_ref, b_ref, o_ref):
    c = c_ref[...]
    o_ref[...] = jnp.dot(_silu(c), w_ref[...], preferred_element_type=jnp.float32,
                         precision=lax.Precision.HIGHEST) + b_ref[...]


def _adaln_mod(c, w_ada, b_ada):
    bsz, d = c.shape
    n = w_ada.shape[1]
    tn = 512
    return pl.pallas_call(
        _mod_kernel,
        grid=(n // tn,),
        in_specs=[pl.BlockSpec((bsz, d), lambda j: (0, 0)),
                  pl.BlockSpec((d, tn), lambda j: (0, j)),
                  pl.BlockSpec((1, tn), lambda j: (0, j))],
        out_specs=pl.BlockSpec((bsz, tn), lambda j: (0, j)),
        out_shape=jax.ShapeDtypeStruct((bsz, n), jnp.float32),
        name="adaln_mod",
    )(c, w_ada, b_ada.reshape(1, n))


def _store_keys_f8(k2, out_ref, first):
    lane = lax.broadcasted_iota(jnp.int32, k2.shape, 1)
    swapped = pltpu.roll(k2, HEAD_DIM_B, axis=1)
    for g, dup in enumerate((jnp.where(lane < HEAD_DIM_B, k2, swapped),
                             jnp.where(lane < HEAD_DIM_B, swapped, k2))):
        hi, lo = _split_f8(dup)
        out_ref[first + g, :, 0:128] = hi
        out_ref[first + g, :, 128:256] = lo


def _store_queries_f8(q, out_ref, g):
    hi, lo = _split_f8(q)
    for part, val in enumerate((hi, lo, hi, lo)):
        out_ref[g, part * 64:(part + 1) * 64, :] = val


def _in_proj_kernel(x_ref, mod_ref, nw_ref, wtok_ref, wfeat_ref, qnw_ref, knw_ref, knws_ref,
                    cosq_ref, sinq_ref, cosk_ref, sink_ref, grp_ref,
                    ka_ref, kb_ref, qa_ref, va_ref, ga_ref, qb_ref, vb_ref, gb_ref, kn_ref,
                    kaf_ref, kbf_ref, qaf_ref, qbf_ref):
    x = x_ref[...]
    shift = mod_ref[0:1, :]
    scale = mod_ref[1:2, :]
    ms = jnp.mean(x * x, axis=1, keepdims=True)
    h = (x * lax.rsqrt(ms + NORM_EPS)) * nw_ref[...] * (1.0 + scale) + shift
    hb = h.astype(jnp.bfloat16)

    tok = jnp.dot(hb, wtok_ref[...], preferred_element_type=jnp.float32)
    ka = tok[:, 0:WIDTH_A]
    ka_ref[...] = ka.astype(jnp.bfloat16)
    for pair in range(N_HEADS_A):
        _store_keys_f8(ka[:, pair * 128:(pair + 1) * 128] * QK_SCALE_ROOT, kaf_ref, 2 * pair)
    kn = jnp.dot(ka * ka, grp_ref[...], preferred_element_type=jnp.float32)
    kn_ref[...] = jnp.max(kn, axis=0, keepdims=True)
    kb = tok[:, 512:640]
    kbs = tok[:, 640:768]
    sq = kb * kb
    r0 = lax.rsqrt(jnp.sum(sq[:, 0:64], axis=1, keepdims=True) * (1.0 / HEAD_DIM_B) + NORM_EPS)
    r1 = lax.rsqrt(jnp.sum(sq[:, 64:128], axis=1, keepdims=True) * (1.0 / HEAD_DIM_B) + NORM_EPS)
    lane = lax.broadcasted_iota(jnp.int32, kb.shape, 1)
    r = jnp.where(lane < HEAD_DIM_B, r0, r1)
    kr = r * ((kb * knw_ref[...]) * cosk_ref[...] + (kbs * knws_ref[...]) * sink_ref[...])
    kb_ref[...] = kr.astype(jnp.bfloat16)
    _store_keys_f8(kr * QK_SCALE_ROOT, kbf_ref, 0)

    def feat(lo, hi):
        return lax.dot_general(wfeat_ref[lo:hi, :], hb, (((1,), (1,)), ((), ())),
                               preferred_element_type=jnp.float32)

    qa = feat(0, 512)
    qa_ref[...] = (qa * QK_SCALE).astype(jnp.bfloat16)
    for g in range(2 * N_HEADS_A):
        _store_queries_f8(qa[g * 64:(g + 1) * 64, :] * QK_SCALE_ROOT, qaf_ref, g)
    va = feat(512, 1024).astype(jnp.bfloat16)
    ones = jnp.ones((ONES_ROWS, va.shape[1]), jnp.bfloat16)
    for hd in range(N_HEADS_A):
        va_ref[hd, 0:DIFF_V_DIM, :] = va[hd * DIFF_V_DIM:(hd + 1) * DIFF_V_DIM, :]
        va_ref[hd, DIFF_V_DIM:DIFF_V_DIM + ONES_ROWS, :] = ones
    ga_ref[...] = _silu(feat(1024, 1536)).astype(jnp.bfloat16)
    qb = feat(1536, 2048)
    cq = cosq_ref[...]
    sq_ = sinq_ref[...]
    qnw = qnw_ref[...]
    for hd in range(N_KV_B * GQA_GROUP):
        q = qb[hd * 64:(hd + 1) * 64, :]
        rq = lax.rsqrt(jnp.mean(q * q, axis=0, keepdims=True) + NORM_EPS)
        qn = q * rq * qnw
        partner = jnp.concatenate([qn[16:32], qn[0:16], qn[48:64], qn[32:48]], axis=0)
        rot = qn * cq + partner * sq_
        qb_ref[hd * 64:(hd + 1) * 64, :] = (rot * QK_SCALE).astype(jnp.bfloat16)
        _store_queries_f8(rot * QK_SCALE_ROOT, qbf_ref, hd)
    vb = feat(2048, 2176).astype(jnp.bfloat16)
    for g in range(N_KV_B):
        vb_ref[g, 0:HEAD_DIM_B, :] = vb[g * HEAD_DIM_B:(g + 1) * HEAD_DIM_B, :]
        vb_ref[g, HEAD_DIM_B:HEAD_DIM_B + ONES_ROWS, :] = ones
    gb_ref[...] = _silu(feat(2176, 2688)).astype(jnp.bfloat16)


def _in_proj(x, mod3, norm_w, w_tok, w_feat_t, qnw_col, knw_row, knws_row, cosq, sinq, cosk, sink,
             grp):
    bsz, seq, d = x.shape
    ts = PROJ_TILE
    bf = jnp.bfloat16
    kdim = SPLIT * DIFF_HEAD_DIM
    const = lambda b, i: (0, 0)
    feat_spec = lambda rows: pl.BlockSpec((None, rows, ts), lambda b, i: (b, 0, i))
    return pl.pallas_call(
        _in_proj_kernel,
        grid=(bsz, seq // ts),
        in_specs=[
            pl.BlockSpec((None, ts, d), lambda b, i: (b, i, 0)),
            pl.BlockSpec((None, 3, d), lambda b, i: (b, 0, 0)),
            pl.BlockSpec((1, d), const),
            pl.BlockSpec(w_tok.shape, const),
            pl.BlockSpec(w_feat_t.shape, const),
            pl.BlockSpec((HEAD_DIM_B, 1), const),
            pl.BlockSpec((1, 128), const),
            pl.BlockSpec((1, 128), const),
            pl.BlockSpec((HEAD_DIM_B, ts), lambda b, i: (0, i)),
            pl.BlockSpec((HEAD_DIM_B, ts), lambda b, i: (0, i)),
            pl.BlockSpec((ts, 128), lambda b, i: (i, 0)),
            pl.BlockSpec((ts, 128), lambda b, i: (i, 0)),
            pl.BlockSpec(grp.shape, const),
        ],
        out_specs=[
            pl.BlockSpec((None, ts, WIDTH_A), lambda b, i: (b, i, 0)),
            pl.BlockSpec((None, ts, 128), lambda b, i: (b, i, 0)),
            feat_spec(512),
            pl.BlockSpec((None, N_HEADS_A, DIFF_V_DIM + ONES_ROWS, ts), lambda b, i: (b, 0, 0, i)),
            feat_spec(512), feat_spec(512),
            pl.BlockSpec((None, N_KV_B, HEAD_DIM_B + ONES_ROWS, ts), lambda b, i: (b, 0, 0, i)),
            feat_spec(512),
            pl.BlockSpec((None, None, 1, 128), lambda b, i: (b, i, 0, 0)),
            pl.BlockSpec((None, 2 * N_HEADS_A, ts, kdim), lambda b, i: (b, 0, i, 0)),
            pl.BlockSpec((None, N_KV_B, ts, kdim), lambda b, i: (b, 0, i, 0)),
            pl.BlockSpec((None, 2 * N_HEADS_A, kdim, ts), lambda b, i: (b, 0, 0, i)),
            pl.BlockSpec((None, N_KV_B * GQA_GROUP, kdim, ts), lambda b, i: (b, 0, 0, i)),
        ],
        out_shape=[
            jax.ShapeDtypeStruct((bsz, seq, WIDTH_A), bf),
            jax.ShapeDtypeStruct((bsz, seq, 128), bf),
            jax.ShapeDtypeStruct((bsz, 512, seq), bf),
            jax.ShapeDtypeStruct((bsz, N_HEADS_A, DIFF_V_DIM + ONES_ROWS, seq), bf),
            jax.ShapeDtypeStruct((bsz, 512, seq), bf),
            jax.ShapeDtypeStruct((bsz, 512, seq), bf),
            jax.ShapeDtypeStruct((bsz, N_KV_B, HEAD_DIM_B + ONES_ROWS, seq), bf),
            jax.ShapeDtypeStruct((bsz, 512, seq), bf),
            jax.ShapeDtypeStruct((bsz, seq // ts, 1, 128), jnp.float32),
            jax.ShapeDtypeStruct((bsz, 2 * N_HEADS_A, seq, kdim), F8),
            jax.ShapeDtypeStruct((bsz, N_KV_B, seq, kdim), F8),
            jax.ShapeDtypeStruct((bsz, 2 * N_HEADS_A, kdim, seq), F8),
            jax.ShapeDtypeStruct((bsz, N_KV_B * GQA_GROUP, kdim, seq), F8),
        ],
        compiler_params=pltpu.CompilerParams(
            dimension_semantics=("arbitrary", "arbitrary"), vmem_limit_bytes=VMEM_LIMIT_BYTES),
        name="in_proj",
    )(x, mod3, norm_w, w_tok, w_feat_t, qnw_col, knw_row, knws_row, cosq, sinq, cosk, sink, grp)


def _softmax_over_tiles(n_rest, first_tile, tile_fn, first_logits_fn, logits_fn, off_fn, vt_fn,
                        fast_logits_fn, fast_off_fn, fast_vt_fn,
                        f8_ok, bound_ok_fn, key_chunk, s_ref, p_refs, m_ref, acc_ref):
    chunks = [slice(c, c + LANE_CHUNK) for c in range(0, m_ref.shape[1], LANE_CHUNK)]
    tk = s_ref.shape[0]
    pieces = [(slice(r, r + key_chunk), cols) for r in range(0, tk, key_chunk) for cols in chunks]
    acc_ref[...] = jnp.zeros(acc_ref.shape, jnp.float32)

    off_first = off_fn(first_tile)

    for cols in chunks:
        u = first_logits_fn(cols)
        s_ref[:, cols] = u
        m_ref[:, cols] = jnp.max(u, axis=0, keepdims=True) + off_first[:, cols]
    fixed_ok = bound_ok_fn(m_ref[...]) & f8_ok

    def first_probs(rows, cols):
        shift = m_ref[:, cols] - off_first[:, cols]
        return jnp.exp2((s_ref[rows, cols] - shift).astype(jnp.bfloat16))

    def probs(t, slot, rows, cols):
        j = tile_fn(t)
        shift = m_ref[:, cols] - fast_off_fn(j)[:, cols]
        p_refs[slot][rows, cols] = jnp.exp2(
            (fast_logits_fn(j, rows, cols) - shift).astype(jnp.bfloat16))

    def tile_values(t):
        if t < 0:
            return {rows.start: vt_fn(first_tile, rows) for rows, _ in pieces}
        return {rows.start: fast_vt_fn(tile_fn(t), rows) for rows, _ in pieces}

    def accumulate(slot, rows, cols, vts):
        acc_ref[:, cols] += jnp.dot(vts[rows.start], p_refs[slot][rows, cols],
                                    preferred_element_type=jnp.float32)

    @pl.when(fixed_ok)
    def _():
        for rows, cols in pieces:
            p_refs[1][rows, cols] = first_probs(rows, cols)
        for t in range(-1, n_rest - 1):
            vts = tile_values(t)
            for rows, cols in pieces:
                probs(t + 1, (t + 1) % 2, rows, cols)
                accumulate(t % 2, rows, cols, vts)
        vts = tile_values(n_rest - 1)
        for rows, cols in pieces:
            accumulate((n_rest - 1) % 2, rows, cols, vts)

    @pl.when(jnp.logical_not(fixed_ok))
    def _():
        vt = vt_fn(first_tile)
        for cols in chunks:
            acc_ref[:, cols] = jnp.dot(vt, first_probs(slice(0, tk), cols),
                                       preferred_element_type=jnp.float32)

        def body(t, carry):
            j = tile_fn(t)
            off = off_fn(j)
            vt = vt_fn(j)
            for cols in chunks:
                u = logits_fn(j, cols)
                m_old = m_ref[:, cols]
                m_new = jnp.maximum(m_old, jnp.max(u, axis=0, keepdims=True) + off[:, cols])
                alpha = jnp.exp2(m_old - m_new)
                p = jnp.exp2((u - (m_new - off[:, cols])).astype(jnp.bfloat16))
                acc_ref[:, cols] = alpha * acc_ref[:, cols] + jnp.dot(
                    vt, p, preferred_element_type=jnp.float32)
                m_ref[:, cols] = m_new
            return carry

        lax.fori_loop(0, n_rest, body, 0)


def _attn_diff_kernel(qt_ref, k_ref, vt_ref, gt_ref, bias_ref, qf_ref, kf_ref, slope_ref, kmax_ref,
                      sublnw_ref, lq1_ref, lk1_ref, lq2_ref, lk2_ref, o_ref,
                      qbd_ref, s_ref, p0_ref, p1_ref, m_ref, acc_ref, *, lam_init):
    tq, tk = TQ_A, TK_A
    b = pl.program_id(0)
    hd = pl.program_id(1)
    qi = pl.program_id(2)
    n_kv = k_ref.shape[0] // tk
    sigma = slope_ref[hd] * LOG2E

    zeros = jnp.zeros((DIFF_HEAD_DIM, tq), jnp.bfloat16)
    qbd_ref[0:64, 0:tq] = qt_ref[0:64, :]
    qbd_ref[64:128, 0:tq] = zeros
    qbd_ref[0:64, tq:2 * tq] = zeros
    qbd_ref[64:128, tq:2 * tq] = qt_ref[64:128, :]

    lane = lax.broadcasted_iota(jnp.int32, (1, 2 * tq), 1)
    il = jnp.where(lane >= tq, lane - tq, lane).astype(jnp.float32)

    whole = slice(0, tk)

    def key_rows(j, rows):
        return pl.ds(pl.multiple_of(j * tk + rows.start, KEY_ALIGN), rows.stop - rows.start)

    def scores(j, cols):
        return jnp.dot(k_ref[key_rows(j, whole), :], qbd_ref[:, cols], preferred_element_type=jnp.float32)

    def first_logits(cols):
        return scores(qi, cols) + bias_ref[:, cols]

    def logits(j, cols):
        key = lax.broadcasted_iota(jnp.int32, (tk, LANE_CHUNK), 0).astype(jnp.float32)
        return scores(j, cols) + jnp.where(j < qi, sigma, -sigma) * key

    def offset(j):
        per_query = jnp.where(j > qi, 2.0 * sigma, 0.0)
        return per_query * il - sigma * (jnp.abs(j - qi) * tk).astype(jnp.float32)

    def values(j, rows=whole):
        return vt_ref[:, key_rows(j, rows)]

    def fast_logits(j, rows, cols):
        mp = cols.start // tq
        return jnp.dot(kf_ref[mp, key_rows(j, rows), :], qf_ref[mp], preferred_element_type=jnp.float32)

    def fast_offset(j):
        return offset(j) + jnp.where(j < qi, sigma * tk, 0.0)

    jl = lax.broadcasted_iota(jnp.int32, (1, tk), 1).astype(jnp.float32)
    factor_left = jnp.exp2(sigma * (jl - tk))
    factor_right = jnp.exp2(-sigma * jl)

    def fast_values(j, rows):
        factor = jnp.where(j < qi, factor_left[:, rows], factor_right[:, rows])
        return (values(j, rows).astype(jnp.float32) * factor).astype(jnp.bfloat16)

    q = qbd_ref[...].astype(jnp.float32)
    q_norm = jnp.sqrt(jnp.sum(q * q, axis=0, keepdims=True))
    k_norm0 = kmax_ref[(b * N_HEADS_A + hd) * 2]
    k_norm1 = kmax_ref[(b * N_HEADS_A + hd) * 2 + 1]
    bound = q_norm * jnp.where(lane < tq, k_norm0, k_norm1) + sigma * il
    f8_ok = ((jnp.max(q_norm) * (QK_SCALE_ROOT / QK_SCALE) < F8_SAFE)
             & (jnp.maximum(k_norm0, k_norm1) * QK_SCALE_ROOT < F8_SAFE))

    _softmax_over_tiles(n_kv - 1, qi, lambda t: jnp.where(t >= qi, t + 1, t),
                        first_logits, logits, offset, values,
                        fast_logits, fast_offset, fast_values,
                        f8_ok, lambda m1: jnp.max(bound - m1) <= GUARD_LOG2,
                        KEY_CHUNK_A, s_ref, (p0_ref, p1_ref), m_ref, acc_ref)

    lam = (jnp.exp(jnp.sum(lq1_ref[...] * lk1_ref[...], axis=1, keepdims=True))
           - jnp.exp(jnp.sum(lq2_ref[...] * lk2_ref[...], axis=1, keepdims=True)) + lam_init)
    inv_l = 1.0 / acc_ref[DIFF_V_DIM:DIFF_V_DIM + 1, :]
    o1 = acc_ref[0:DIFF_V_DIM, 0:tq] * inv_l[:, 0:tq]
    o2 = acc_ref[0:DIFF_V_DIM, tq:2 * tq] * inv_l[:, tq:2 * tq]
    diff = o1 - lam * o2
    ms = jnp.mean(diff * diff, axis=0, keepdims=True)
    y = diff * lax.rsqrt(ms + NORM_EPS) * sublnw_ref[...] * (1.0 - lam_init)
    gated = y * gt_ref[...].astype(jnp.float32)
    o_ref[...] = gated.T.astype(o_ref.dtype)


def _attn_diff(qa_t, ka, va_t, ga_t, bias, qa_f8, ka_f8, slopes, kmax, subln_col, lq1, lk1, lq2, lk2,
               lam_init):
    bsz, seq, _ = ka.shape
    tq = TQ_A
    n = 2 * tq
    kdim = SPLIT * DIFF_HEAD_DIM
    vec = pl.BlockSpec((1, DIFF_HEAD_DIM), lambda b, h, i: (0, 0))
    smem = pl.BlockSpec(memory_space=pltpu.SMEM)
    return pl.pallas_call(
        functools.partial(_attn_diff_kernel, lam_init=lam_init),
        grid=(bsz, N_HEADS_A, seq // tq),
        in_specs=[
            pl.BlockSpec((None, 128, tq), lambda b, h, i: (b, h, i)),
            pl.BlockSpec((None, seq, 128), lambda b, h, i: (b, 0, h)),
            pl.BlockSpec((None, None, DIFF_V_DIM + ONES_ROWS, seq), lambda b, h, i: (b, h, 0, 0)),
            pl.BlockSpec((None, 128, tq), lambda b, h, i: (b, h, i)),
            pl.BlockSpec((None, TK_A, n), lambda b, h, i: (h, 0, 0)),
            pl.BlockSpec((None, None, 2, kdim, tq), lambda b, h, i: (b, h, 0, 0, i)),
            pl.BlockSpec((None, None, 2, seq, kdim), lambda b, h, i: (b, h, 0, 0, 0)),
            smem, smem,
            pl.BlockSpec((DIFF_V_DIM, 1), lambda b, h, i: (0, 0)),
            vec, vec, vec, vec,
        ],
        out_specs=pl.BlockSpec((None, tq, 128), lambda b, h, i: (b, i, h)),
        out_shape=jax.ShapeDtypeStruct((bsz, seq, WIDTH_A), jnp.bfloat16),
        scratch_shapes=[
            pltpu.VMEM((128, n), jnp.bfloat16),
            pltpu.VMEM((TK_A, n), jnp.float32),
            pltpu.VMEM((TK_A, n), jnp.bfloat16),
            pltpu.VMEM((TK_A, n), jnp.bfloat16),
            pltpu.VMEM((1, n), jnp.float32),
            pltpu.VMEM((DIFF_V_DIM + ONES_ROWS, n), jnp.float32),
        ],
        compiler_params=pltpu.CompilerParams(
            dimension_semantics=("arbitrary", "arbitrary", "arbitrary"),
            vmem_limit_bytes=VMEM_LIMIT_BYTES),
        name="attn_diff",
    )(qa_t, ka, va_t, ga_t, bias, qa_f8, ka_f8, slopes, kmax, subln_col, lq1, lk1, lq2, lk2)


def _attn_gqa_kernel(qt_ref, k_ref, vt_ref, gt_ref, qf_ref, kf_ref, kmax_ref, o_ref,
                     qp_ref, qpf_ref, s_ref, p0_ref, p1_ref, m_ref, acc_ref):
    tq, tk = TQ_B, TK_B
    g = pl.program_id(1)
    n_kv = k_ref.shape[0] // tk
    n = GQA_GROUP * tq

    row = lax.broadcasted_iota(jnp.int32, (128, tq), 0)
    lo = g * HEAD_DIM_B
    mine = (row >= lo) & (row < lo + HEAD_DIM_B)
    for r in range(GQA_GROUP):
        q = qt_ref[r * 64:(r + 1) * 64, :].astype(jnp.float32)
        q2 = jnp.concatenate([q, q], axis=0)
        qp_ref[:, r * tq:(r + 1) * tq] = jnp.where(mine, q2, 0.0).astype(jnp.bfloat16)
        qpf_ref[:, r * tq:(r + 1) * tq] = qf_ref[r]

    whole = slice(0, tk)

    def key_rows(j, rows):
        return pl.ds(pl.multiple_of(j * tk + rows.start, KEY_ALIGN), rows.stop - rows.start)

    def logits(j, cols):
        return jnp.dot(k_ref[key_rows(j, whole), :], qp_ref[:, cols], preferred_element_type=jnp.float32)

    def fast_logits(j, rows, cols):
        return jnp.dot(kf_ref[key_rows(j, rows), :], qpf_ref[:, cols], preferred_element_type=jnp.float32)

    def values(j, rows=whole):
        return vt_ref[:, key_rows(j, rows)]

    no_offset = jnp.zeros((1, n), jnp.float32)
    q = qp_ref[...].astype(jnp.float32)
    q_norm = jnp.sqrt(jnp.sum(q * q, axis=0, keepdims=True))
    bound = q_norm * kmax_ref[0]
    f8_ok = ((jnp.max(q_norm) * (QK_SCALE_ROOT / QK_SCALE) < F8_SAFE)
             & (kmax_ref[0] * QK_SCALE_ROOT < F8_SAFE))

    _softmax_over_tiles(n_kv - 1, jnp.int32(0), lambda t: jnp.asarray(t, jnp.int32) + 1,
                        lambda cols: logits(jnp.int32(0), cols), logits, lambda j: no_offset, values,
                        fast_logits, lambda j: no_offset, values,
                        f8_ok, lambda m1: jnp.max(bound - m1) <= GUARD_LOG2,
                        KEY_CHUNK_B, s_ref, (p0_ref, p1_ref), m_ref, acc_ref)

    o = acc_ref[0:HEAD_DIM_B, :] * (1.0 / acc_ref[HEAD_DIM_B:HEAD_DIM_B + 1, :])
    o = jnp.concatenate([o[:, r * tq:(r + 1) * tq] for r in range(GQA_GROUP)], axis=0)
    gated = o * gt_ref[...].astype(jnp.float32)
    o_ref[...] = gated.T.astype(o_ref.dtype)


def _attn_gqa(qb_t, kb, vb_t, gb_t, qb_f8, kb_f8, kmax):
    bsz, seq, _ = kb.shape
    tq = TQ_B
    n = GQA_GROUP * tq
    kdim = SPLIT * HEAD_DIM_B
    return pl.pallas_call(
        _attn_gqa_kernel,
        grid=(bsz, N_KV_B, seq // tq),
        in_specs=[
            pl.BlockSpec((None, 256, tq), lambda b, g, i: (b, g, i)),
            pl.BlockSpec((None, seq, 128), lambda b, g, i: (b, 0, 0)),
            pl.BlockSpec((None, None, HEAD_DIM_B + ONES_ROWS, seq), lambda b, g, i: (b, g, 0, 0)),
            pl.BlockSpec((None, 256, tq), lambda b, g, i: (b, g, i)),
            pl.BlockSpec((None, GQA_GROUP, kdim, tq), lambda b, g, i: (b, g, 0, i)),
            pl.BlockSpec((None, None, seq, kdim), lambda b, g, i: (b, g, 0, 0)),
            pl.BlockSpec(memory_space=pltpu.SMEM),
        ],
        out_specs=pl.BlockSpec((None, tq, 256), lambda b, g, i: (b, i, g)),
        out_shape=jax.ShapeDtypeStruct((bsz, seq, WIDTH_B), jnp.bfloat16),
        scratch_shapes=[
            pltpu.VMEM((128, n), jnp.bfloat16),
            pltpu.VMEM((kdim, n), F8),
            pltpu.VMEM((TK_B, n), jnp.float32),
            pltpu.VMEM((TK_B, n), jnp.bfloat16),
            pltpu.VMEM((TK_B, n), jnp.bfloat16),
            pltpu.VMEM((1, n), jnp.float32),
            pltpu.VMEM((HEAD_DIM_B + ONES_ROWS, n), jnp.float32),
        ],
        compiler_params=pltpu.CompilerParams(
            dimension_semantics=("arbitrary", "arbitrary", "arbitrary"),
            vmem_limit_bytes=VMEM_LIMIT_BYTES),
        name="attn_gqa",
    )(qb_t, kb, vb_t, gb_t, qb_f8, kb_f8, kmax)


def _out_proj_kernel(ga_ref, gb_ref, wa_ref, wb_ref, x_ref, mod_ref, fw_ref, o_ref):
    y = jnp.dot(ga_ref[...], wa_ref[...], preferred_element_type=jnp.float32)
    y = y + jnp.dot(gb_ref[...], wb_ref[...], preferred_element_type=jnp.float32)
    z = x_ref[...] + mod_ref[2:3, :] * y
    ms = jnp.mean(z * z, axis=1, keepdims=True)
    o_ref[...] = z * lax.rsqrt(ms + NORM_EPS) * fw_ref[...]


def _out_proj(ga, gb, w_a, w_b, x, mod3, final_w):
    bsz, seq, d = x.shape
    ts = PROJ_TILE
    const = lambda b, i: (0, 0)
    return pl.pallas_call(
        _out_proj_kernel,
        grid=(bsz, seq // ts),
        in_specs=[
            pl.BlockSpec((None, ts, WIDTH_A), lambda b, i: (b, i, 0)),
            pl.BlockSpec((None, ts, WIDTH_B), lambda b, i: (b, i, 0)),
            pl.BlockSpec(w_a.shape, const),
            pl.BlockSpec(w_b.shape, const),
            pl.BlockSpec((None, ts, d), lambda b, i: (b, i, 0)),
            pl.BlockSpec((None, 3, d), lambda b, i: (b, 0, 0)),
            pl.BlockSpec((1, d), const),
        ],
        out_specs=pl.BlockSpec((None, ts, d), lambda b, i: (b, i, 0)),
        out_shape=jax.ShapeDtypeStruct((bsz, seq, d), jnp.float32),
        compiler_params=pltpu.CompilerParams(
            dimension_semantics=("arbitrary", "arbitrary"), vmem_limit_bytes=VMEM_LIMIT_BYTES),
        name="out_proj",
    )(ga, gb, w_a, w_b, x, mod3, final_w)


def _rope_tables(seq):
    pos = np.arange(seq)
    row = (pos // GRID_W).astype(np.float32)
    col = (pos % GRID_W).astype(np.float32)
    n_freq = ROT_HALF // 2
    freqs = (1.0 / (ROPE_THETA ** (np.arange(n_freq, dtype=np.float32) * 2.0 / ROT_HALF))).astype(np.float32)
    row, col, freqs = jnp.asarray(row), jnp.asarray(col), jnp.asarray(freqs)
    ang_r = row[:, None] * freqs[None, :]
    ang_c = col[:, None] * freqs[None, :]
    cos = jnp.concatenate([jnp.cos(ang_r), jnp.cos(ang_r), jnp.cos(ang_c), jnp.cos(ang_c)], axis=1)
    sin = jnp.concatenate([-jnp.sin(ang_r), jnp.sin(ang_r), -jnp.sin(ang_c), jnp.sin(ang_c)], axis=1)
    d = np.arange(HEAD_DIM_B)
    partner = np.where((d // n_freq) % 2 == 0, d + n_freq, d - n_freq)
    return cos, sin, partner


def _alibi_diagonal(slopes):
    sig = (slopes * LOG2E)[:, None, None]
    jl = jnp.arange(TK_A, dtype=jnp.float32)[None, :, None]
    il = jnp.tile(jnp.arange(TQ_A, dtype=jnp.float32), 2)[None, None, :]
    return sig * (il - jnp.abs(il - jl))


def kernel(x, c, w_ada, b_ada, norm_w, w_in, lambda_q1, lambda_k1, lambda_q2, lambda_k2,
           subln_w, q_norm_w, k_norm_w, w_out, final_norm_w):
    assert w_ada.shape[0] == 1, "single-layer problem: the final norm is fused into the output projection"
    bsz, seq, d = x.shape
    assert seq % TQ_A == 0 and seq % TK_B == 0 and seq % PROJ_TILE == 0
    assert (seq // TK_A) % 2 == 0 and (seq // TK_B) % 2 == 0
    bf = jnp.bfloat16
    kdim = SPLIT * DIFF_HEAD_DIM
    cos, sin, partner = _rope_tables(seq)
    cosq, sinq = cos.T, sin.T
    cosk, sink = jnp.tile(cos, (1, N_KV_B)), jnp.tile(sin, (1, N_KV_B))
    slopes = jnp.asarray(2.0 ** (-8.0 * np.arange(1, N_HEADS_A + 1) / N_HEADS_A), jnp.float32)
    alibi = _alibi_diagonal(slopes)
    lam_init = 0.8 - 0.6 * math.exp(-0.3 * 0)
    grp = jnp.asarray(np.arange(WIDTH_A)[:, None] // DIFF_HEAD_DIM == np.arange(128)[None, :], jnp.float32)

    w = w_in[0]
    q_a, k_a, v_a, g_a = w[:, 0:512], w[:, 512:1024], w[:, 1024:1536], w[:, 1536:2048]
    q_b, k_b, v_b, g_b = w[:, 2048:2560], w[:, 2560:2688], w[:, 2688:2816], w[:, 2816:3328]
    partner2 = np.concatenate([partner, partner + HEAD_DIM_B])
    w_tok = jnp.concatenate([k_a, k_b, k_b[:, partner2]], axis=1).astype(bf)
    w_feat_t = jnp.concatenate([q_a, v_a, g_a, q_b, v_b, g_b], axis=1).T.astype(bf)
    knw = jnp.tile(k_norm_w[0], N_KV_B)
    wo = w_out[0].astype(bf)

    mod3 = _adaln_mod(c, w_ada[0], b_ada[0]).reshape(bsz, 3, d)
    ka, kb, qa_t, va_t, ga_t, qb_t, vb_t, gb_t, kn, ka_f8, kb_f8, qa_f8, qb_f8 = _in_proj(
        x, mod3, norm_w[0].reshape(1, d), w_tok, w_feat_t,
        q_norm_w[0].reshape(HEAD_DIM_B, 1), knw.reshape(1, 128), knw[partner2].reshape(1, 128),
        cosq, sinq, cosk, sink, grp)
    kmax_a = (jnp.sqrt(jnp.max(kn[:, :, 0, :2 * N_HEADS_A], axis=1)) * NORM_MARGIN).reshape(-1)
    kmax_b = (math.sqrt(HEAD_DIM_B) * NORM_MARGIN * jnp.max(jnp.abs(k_norm_w[0]))).reshape(1)
    oa = _attn_diff(qa_t, ka, va_t, ga_t, alibi,
                    qa_f8.reshape(bsz, N_HEADS_A, 2, kdim, seq), ka_f8.reshape(bsz, N_HEADS_A, 2, seq, kdim),
                    slopes, kmax_a, subln_w[0].reshape(DIFF_V_DIM, 1),
                    lambda_q1[0].reshape(1, -1), lambda_k1[0].reshape(1, -1),
                    lambda_q2[0].reshape(1, -1), lambda_k2[0].reshape(1, -1), lam_init)
    ob = _attn_gqa(qb_t, kb, vb_t, gb_t, qb_f8, kb_f8, kmax_b)
    return _out_proj(oa, ob, wo[:WIDTH_A], wo[WIDTH_A:], x, mod3, final_norm_w.reshape(1, d))
```

```python
import functools
import math

import numpy as np
import jax
import jax.numpy as jnp
from jax import lax
from jax.experimental import pallas as pl
from jax.experimental.pallas import tpu as pltpu

D_MODEL = 1024
N_HEADS_A = 4
DIFF_HEAD_DIM = 64
DIFF_V_DIM = 128
WIDTH_A = 512
N_KV_B = 2
GQA_GROUP = 4
HEAD_DIM_B = 64
WIDTH_B = 512
GRID_W = 64
ROPE_THETA = 10000.0
ROT_HALF = 32
NORM_EPS = 1e-6
MXU_TILE = 256
LANE_CHUNK_A = MXU_TILE
LANE_CHUNK_B = MXU_TILE
KEY_CHUNK_A = 512
KEY_CHUNK_B = 512
KEY_ALIGN = MXU_TILE
ONES_ROWS = 16
LOG2E = 1.4426950408889634
NEG_BIG = -1e30
GUARD_LOG2 = 100.0
NORM_MARGIN = 1.01
F8 = jnp.float8_e4m3fn
F8_SAFE = 440.0
SPLIT = 4
QK_SCALE = LOG2E / math.sqrt(DIFF_HEAD_DIM)
QK_SCALE_ROOT = math.sqrt(QK_SCALE)

VMEM_LIMIT_BYTES = 56 * 1024 * 1024

PROJ_TILE = 1024
TQ_A = 512
TK_A = 512
TQ_B = 512
TK_B = 512

assert DIFF_HEAD_DIM == HEAD_DIM_B and TQ_A % LANE_CHUNK_A == 0 and TQ_B % LANE_CHUNK_B == 0


def _silu(v):
    return v * (1.0 / (1.0 + jnp.exp(-v)))


def _split_f8(v):
    hi = v.astype(F8)
    lo = (v - hi.astype(jnp.float32)).astype(F8)
    return hi, lo


def _mod_kernel(c_ref, w_ref, b_ref, o_ref):
    c = c_ref[...]
    o_ref[...] = jnp.dot(_silu(c), w_ref[...], preferred_element_type=jnp.float32,
                         precision=lax.Precision.HIGHEST) + b_ref[...]


def _adaln_mod(c, w_ada, b_ada):
    bsz, d = c.shape
    n = w_ada.shape[1]
    tn = 512
    return pl.pallas_call(
        _mod_kernel,
        grid=(n // tn,),
        in_specs=[pl.BlockSpec((bsz, d), lambda j: (0, 0)),
                  pl.BlockSpec((d, tn), lambda j: (0, j)),
                  pl.BlockSpec((1, tn), lambda j: (0, j))],
        out_specs=pl.BlockSpec((bsz, tn), lambda j: (0, j)),
        out_shape=jax.ShapeDtypeStruct((bsz, n), jnp.float32),
        name="adaln_mod",
    )(c, w_ada, b_ada.reshape(1, n))


def _store_keys_f8(k2, out_ref, first):
    lane = lax.broadcasted_iota(jnp.int32, k2.shape, 1)
    swapped = pltpu.roll(k2, HEAD_DIM_B, axis=1)
    for g, dup in enumerate((jnp.where(lane < HEAD_DIM_B, k2, swapped),
                             jnp.where(lane < HEAD_DIM_B, swapped, k2))):
        hi, lo = _split_f8(dup)
        out_ref[first + g, :, 0:128] = hi
        out_ref[first + g, :, 128:256] = lo


def _store_queries_f8(q, out_ref, g):
    hi, lo = _split_f8(q)
    for part, val in enumerate((hi, lo, hi, lo)):
        out_ref[g, part * 64:(part + 1) * 64, :] = val


def _in_proj_kernel(x_ref, mod_ref, nw_ref, wtok_ref, wfeat_ref, qnw_ref, knw_ref, knws_ref,
                    cosq_ref, sinq_ref, cosk_ref, sink_ref, grp_ref,
                    ka_ref, kb_ref, qa_ref, va_ref, ga_ref, qb_ref, vb_ref, gb_ref, kn_ref,
                    kaf_ref, kbf_ref, qaf_ref, qbf_ref):
    x = x_ref[...]
    shift = mod_ref[0:1, :]
    scale = mod_ref[1:2, :]
    ms = jnp.mean(x * x, axis=1, keepdims=True)
    h = (x * lax.rsqrt(ms + NORM_EPS)) * nw_ref[...] * (1.0 + scale) + shift
    hb = h.astype(jnp.bfloat16)

    tok = jnp.dot(hb, wtok_ref[...], preferred_element_type=jnp.float32)
    ka = tok[:, 0:WIDTH_A]
    ka_ref[...] = ka.astype(jnp.bfloat16)
    for pair in range(N_HEADS_A):
        _store_keys_f8(ka[:, pair * 128:(pair + 1) * 128] * QK_SCALE_ROOT, kaf_ref, 2 * pair)
    kn = jnp.dot(ka * ka, grp_ref[...], preferred_element_type=jnp.float32)
    kn_ref[...] = jnp.max(kn, axis=0, keepdims=True)
    kb = tok[:, 512:640]
    kbs = tok[:, 640:768]
    sq = kb * kb
    r0 = lax.rsqrt(jnp.sum(sq[:, 0:64], axis=1, keepdims=True) * (1.0 / HEAD_DIM_B) + NORM_EPS)
    r1 = lax.rsqrt(jnp.sum(sq[:, 64:128], axis=1, keepdims=True) * (1.0 / HEAD_DIM_B) + NORM_EPS)
    lane = lax.broadcasted_iota(jnp.int32, kb.shape, 1)
    r = jnp.where(lane < HEAD_DIM_B, r0, r1)
    kr = r * ((kb * knw_ref[...]) * cosk_ref[...] + (kbs * knws_ref[...]) * sink_ref[...])
    kb_ref[...] = kr.astype(jnp.bfloat16)
    _store_keys_f8(kr * QK_SCALE_ROOT, kbf_ref, 0)

    def feat(lo, hi):
        return lax.dot_general(wfeat_ref[lo:hi, :], hb, (((1,), (1,)), ((), ())),
                               preferred_element_type=jnp.float32)

    qa = feat(0, 512)
    qa_ref[...] = (qa * QK_SCALE).astype(jnp.bfloat16)
    for g in range(2 * N_HEADS_A):
        _store_queries_f8(qa[g * 64:(g + 1) * 64, :] * QK_SCALE_ROOT, qaf_ref, g)
    va = feat(512, 1024).astype(jnp.bfloat16)
    ones = jnp.ones((ONES_ROWS, va.shape[1]), jnp.bfloat16)
    for hd in range(N_HEADS_A):
        va_ref[hd, 0:DIFF_V_DIM, :] = va[hd * DIFF_V_DIM:(hd + 1) * DIFF_V_DIM, :]
        va_ref[hd, DIFF_V_DIM:DIFF_V_DIM + ONES_ROWS, :] = ones
    ga_ref[...] = _silu(feat(1024, 1536)).astype(jnp.bfloat16)
    qb = feat(1536, 2048)
    cq = cosq_ref[...]
    sq_ = sinq_ref[...]
    qnw = qnw_ref[...]
    for hd in range(N_KV_B * GQA_GROUP):
        q = qb[hd * 64:(hd + 1) * 64, :]
        rq = lax.rsqrt(jnp.mean(q * q, axis=0, keepdims=True) + NORM_EPS)
        qn = q * rq * qnw
        partner = jnp.concatenate([qn[16:32], qn[0:16], qn[48:64], qn[32:48]], axis=0)
        rot = qn * cq + partner * sq_
        qb_ref[hd * 64:(hd + 1) * 64, :] = (rot * QK_SCALE).astype(jnp.bfloat16)
        _store_queries_f8(rot * QK_SCALE_ROOT, qbf_ref, hd)
    vb = feat(2048, 2176).astype(jnp.bfloat16)
    for g in range(N_KV_B):
        vb_ref[g, 0:HEAD_DIM_B, :] = vb[g * HEAD_DIM_B:(g + 1) * HEAD_DIM_B, :]
        vb_ref[g, HEAD_DIM_B:HEAD_DIM_B + ONES_ROWS, :] = ones
    gb_ref[...] = _silu(feat(2176, 2688)).astype(jnp.bfloat16)


def _in_proj(x, mod3, norm_w, w_tok, w_feat_t, qnw_col, knw_row, knws_row, cosq, sinq, cosk, sink,
             grp):
    bsz, seq, d = x.shape
    ts = PROJ_TILE
    bf = jnp.bfloat16
    kdim = SPLIT * DIFF_HEAD_DIM
    const = lambda b, i: (0, 0)
    feat_spec = lambda rows: pl.BlockSpec((None, rows, ts), lambda b, i: (b, 0, i))
    return pl.pallas_call(
        _in_proj_kernel,
        grid=(bsz, seq // ts),
        in_specs=[
            pl.BlockSpec((None, ts, d), lambda b, i: (b, i, 0)),
            pl.BlockSpec((None, 3, d), lambda b, i: (b, 0, 0)),
            pl.BlockSpec((1, d), const),
            pl.BlockSpec(w_tok.shape, const),
            pl.BlockSpec(w_feat_t.shape, const),
            pl.BlockSpec((HEAD_DIM_B, 1), const),
            pl.BlockSpec((1, 128), const),
            pl.BlockSpec((1, 128), const),
            pl.BlockSpec((HEAD_DIM_B, ts), lambda b, i: (0, i)),
            pl.BlockSpec((HEAD_DIM_B, ts), lambda b, i: (0, i)),
            pl.BlockSpec((ts, 128), lambda b, i: (i, 0)),
            pl.BlockSpec((ts, 128), lambda b, i: (i, 0)),
            pl.BlockSpec(grp.shape, const),
        ],
        out_specs=[
            pl.BlockSpec((None, ts, WIDTH_A), lambda b, i: (b, i, 0)),
            pl.BlockSpec((None, ts, 128), lambda b, i: (b, i, 0)),
            feat_spec(512),
            pl.BlockSpec((None, N_HEADS_A, DIFF_V_DIM + ONES_ROWS, ts), lambda b, i: (b, 0, 0, i)),
            feat_spec(512), feat_spec(512),
            pl.BlockSpec((None, N_KV_B, HEAD_DIM_B + ONES_ROWS, ts), lambda b, i: (b, 0, 0, i)),
            feat_spec(512),
            pl.BlockSpec((None, None, 1, 128), lambda b, i: (b, i, 0, 0)),
            pl.BlockSpec((None, 2 * N_HEADS_A, ts, kdim), lambda b, i: (b, 0, i, 0)),
            pl.BlockSpec((None, N_KV_B, ts, kdim), lambda b, i: (b, 0, i, 0)),
            pl.BlockSpec((None, 2 * N_HEADS_A, kdim, ts), lambda b, i: (b, 0, 0, i)),
            pl.BlockSpec((None, N_KV_B * GQA_GROUP, kdim, ts), lambda b, i: (b, 0, 0, i)),
        ],
        out_shape=[
            jax.ShapeDtypeStruct((bsz, seq, WIDTH_A), bf),
            jax.ShapeDtypeStruct((bsz, seq, 128), bf),
            jax.ShapeDtypeStruct((bsz, 512, seq), bf),
            jax.ShapeDtypeStruct((bsz, N_HEADS_A, DIFF_V_DIM + ONES_ROWS, seq), bf),
            jax.ShapeDtypeStruct((bsz, 512, seq), bf),
            jax.ShapeDtypeStruct((bsz, 512, seq), bf),
            jax.ShapeDtypeStruct((bsz, N_KV_B, HEAD_DIM_B + ONES_ROWS, seq), bf),
            jax.ShapeDtypeStruct((bsz, 512, seq), bf),
            jax.ShapeDtypeStruct((bsz, seq // ts, 1, 128), jnp.float32),
            jax.ShapeDtypeStruct((bsz, 2 * N_HEADS_A, seq, kdim), F8),
            jax.ShapeDtypeStruct((bsz, N_KV_B, seq, kdim), F8),
            jax.ShapeDtypeStruct((bsz, 2 * N_HEADS_A, kdim, seq), F8),
            jax.ShapeDtypeStruct((bsz, N_KV_B * GQA_GROUP, kdim, seq), F8),
        ],
        compiler_params=pltpu.CompilerParams(
            dimension_semantics=("arbitrary", "arbitrary"), vmem_limit_bytes=VMEM_LIMIT_BYTES),
        name="in_proj",
    )(x, mod3, norm_w, w_tok, w_feat_t, qnw_col, knw_row, knws_row, cosq, sinq, cosk, sink, grp)


def _softmax_over_tiles(n_rest, first_tile, tile_fn, first_logits_fn, logits_fn, off_fn, vt_fn,
                        fast_logits_fn, fast_off_fn, fast_vt_fn,
                        f8_ok, bound_ok_fn, key_chunk, lane_chunk, s_ref, p_refs, m_ref, acc_ref):
    chunks = [slice(c, c + lane_chunk) for c in range(0, m_ref.shape[1], lane_chunk)]
    tk = s_ref.shape[0]
    pieces = [(slice(r, r + key_chunk), cols) for r in range(0, tk, key_chunk) for cols in chunks]
    acc_ref[...] = jnp.zeros(acc_ref.shape, jnp.float32)

    off_first = off_fn(first_tile)

    for cols in chunks:
        u = first_logits_fn(cols)
        s_ref[:, cols] = u
        m_ref[:, cols] = jnp.max(u, axis=0, keepdims=True) + off_first[:, cols]
    fixed_ok = bound_ok_fn(m_ref[...]) & f8_ok

    def first_probs(rows, cols):
        shift = m_ref[:, cols] - off_first[:, cols]
        return jnp.exp2((s_ref[rows, cols] - shift).astype(jnp.bfloat16))

    def probs(t, slot, rows, cols):
        j = tile_fn(t)
        shift = m_ref[:, cols] - fast_off_fn(j)[:, cols]
        p_refs[slot][rows, cols] = jnp.exp2(
            (fast_logits_fn(j, rows, cols) - shift).astype(jnp.bfloat16))

    def tile_values(t):
        if t < 0:
            return {rows.start: vt_fn(first_tile, rows) for rows, _ in pieces}
        return {rows.start: fast_vt_fn(tile_fn(t), rows) for rows, _ in pieces}

    def accumulate(slot, rows, cols, vts):
        acc_ref[:, cols] += jnp.dot(vts[rows.start], p_refs[slot][rows, cols],
                                    preferred_element_type=jnp.float32)

    @pl.when(fixed_ok)
    def _():
        for rows, cols in pieces:
            p_refs[1][rows, cols] = first_probs(rows, cols)
        for t in range(-1, n_rest - 1):
            vts = tile_values(t)
            for rows, cols in pieces:
                probs(t + 1, (t + 1) % 2, rows, cols)
                accumulate(t % 2, rows, cols, vts)
        vts = tile_values(n_rest - 1)
        for rows, cols in pieces:
            accumulate((n_rest - 1) % 2, rows, cols, vts)

    @pl.when(jnp.logical_not(fixed_ok))
    def _():
        vt = vt_fn(first_tile)
        for cols in chunks:
            acc_ref[:, cols] = jnp.dot(vt, first_probs(slice(0, tk), cols),
                                       preferred_element_type=jnp.float32)

        def body(t, carry):
            j = tile_fn(t)
            off = off_fn(j)
            vt = vt_fn(j)
            for cols in chunks:
                u = logits_fn(j, cols)
                m_old = m_ref[:, cols]
                m_new = jnp.maximum(m_old, jnp.max(u, axis=0, keepdims=True) + off[:, cols])
                alpha = jnp.exp2(m_old - m_new)
                p = jnp.exp2((u - (m_new - off[:, cols])).astype(jnp.bfloat16))
                acc_ref[:, cols] = alpha * acc_ref[:, cols] + jnp.dot(
                    vt, p, preferred_element_type=jnp.float32)
                m_ref[:, cols] = m_new
            return carry

        lax.fori_loop(0, n_rest, body, 0)


def _attn_diff_kernel(qt_ref, k_ref, vt_ref, gt_ref, bias_ref, qf_ref, kf_ref, slope_ref, kmax_ref,
                      sublnw_ref, lq1_ref, lk1_ref, lq2_ref, lk2_ref, o_ref,
                      qbd_ref, s_ref, p0_ref, p1_ref, m_ref, acc_ref, *, lam_init):
    tq, tk = TQ_A, TK_A
    b = pl.program_id(0)
    hd = pl.program_id(1)
    qi = pl.program_id(2)
    n_kv = k_ref.shape[0] // tk
    sigma = slope_ref[hd] * LOG2E

    zeros = jnp.zeros((DIFF_HEAD_DIM, tq), jnp.bfloat16)
    qbd_ref[0:64, 0:tq] = qt_ref[0:64, :]
    qbd_ref[64:128, 0:tq] = zeros
    qbd_ref[0:64, tq:2 * tq] = zeros
    qbd_ref[64:128, tq:2 * tq] = qt_ref[64:128, :]

    lane = lax.broadcasted_iota(jnp.int32, (1, 2 * tq), 1)
    il = jnp.where(lane >= tq, lane - tq, lane).astype(jnp.float32)

    whole = slice(0, tk)

    def key_rows(j, rows):
        return pl.ds(pl.multiple_of(j * tk + rows.start, KEY_ALIGN), rows.stop - rows.start)

    def scores(j, cols):
        return jnp.dot(k_ref[key_rows(j, whole), :], qbd_ref[:, cols], preferred_element_type=jnp.float32)

    def first_logits(cols):
        return scores(qi, cols) + bias_ref[:, cols]

    def logits(j, cols):
        key = lax.broadcasted_iota(jnp.int32, (tk, LANE_CHUNK_A), 0).astype(jnp.float32)
        return scores(j, cols) + jnp.where(j < qi, sigma, -sigma) * key

    def offset(j):
        per_query = jnp.where(j > qi, 2.0 * sigma, 0.0)
        return per_query * il - sigma * (jnp.abs(j - qi) * tk).astype(jnp.float32)

    def values(j, rows=whole):
        return vt_ref[:, key_rows(j, rows)]

    def fast_logits(j, rows, cols):
        mp = cols.start // tq
        queries = slice(cols.start - mp * tq, cols.stop - mp * tq)
        return jnp.dot(kf_ref[mp, key_rows(j, rows), :], qf_ref[mp, :, queries],
                       preferred_element_type=jnp.float32)

    def fast_offset(j):
        return offset(j) + jnp.where(j < qi, sigma * tk, 0.0)

    jl = lax.broadcasted_iota(jnp.int32, (1, tk), 1).astype(jnp.float32)
    factor_left = jnp.exp2(sigma * (jl - tk))
    factor_right = jnp.exp2(-sigma * jl)

    def fast_values(j, rows):
        factor = jnp.where(j < qi, factor_left[:, rows], factor_right[:, rows])
        return (values(j, rows).astype(jnp.float32) * factor).astype(jnp.bfloat16)

    q = qbd_ref[...].astype(jnp.float32)
    q_norm = jnp.sqrt(jnp.sum(q * q, axis=0, keepdims=True))
    k_norm0 = kmax_ref[(b * N_HEADS_A + hd) * 2]
    k_norm1 = kmax_ref[(b * N_HEADS_A + hd) * 2 + 1]
    bound = q_norm * jnp.where(lane < tq, k_norm0, k_norm1) + sigma * il
    f8_ok = ((jnp.max(q_norm) * (QK_SCALE_ROOT / QK_SCALE) < F8_SAFE)
             & (jnp.maximum(k_norm0, k_norm1) * QK_SCALE_ROOT < F8_SAFE))

    _softmax_over_tiles(n_kv - 1, qi, lambda t: jnp.where(t >= qi, t + 1, t),
                        first_logits, logits, offset, values,
                        fast_logits, fast_offset, fast_values,
                        f8_ok, lambda m1: jnp.max(bound - m1) <= GUARD_LOG2,
                        KEY_CHUNK_A, LANE_CHUNK_A, s_ref, (p0_ref, p1_ref), m_ref, acc_ref)

    lam = (jnp.exp(jnp.sum(lq1_ref[...] * lk1_ref[...], axis=1, keepdims=True))
           - jnp.exp(jnp.sum(lq2_ref[...] * lk2_ref[...], axis=1, keepdims=True)) + lam_init)
    inv_l = 1.0 / acc_ref[DIFF_V_DIM:DIFF_V_DIM + 1, :]
    o1 = acc_ref[0:DIFF_V_DIM, 0:tq] * inv_l[:, 0:tq]
    o2 = acc_ref[0:DIFF_V_DIM, tq:2 * tq] * inv_l[:, tq:2 * tq]
    diff = o1 - lam * o2
    ms = jnp.mean(diff * diff, axis=0, keepdims=True)
    y = diff * lax.rsqrt(ms + NORM_EPS) * sublnw_ref[...] * (1.0 - lam_init)
    gated = y * gt_ref[...].astype(jnp.float32)
    o_ref[...] = gated.T.astype(o_ref.dtype)


def _attn_diff(qa_t, ka, va_t, ga_t, bias, qa_f8, ka_f8, slopes, kmax, subln_col, lq1, lk1, lq2, lk2,
               lam_init):
    bsz, seq, _ = ka.shape
    tq = TQ_A
    n = 2 * tq
    kdim = SPLIT * DIFF_HEAD_DIM
    vec = pl.BlockSpec((1, DIFF_HEAD_DIM), lambda b, h, i: (0, 0))
    smem = pl.BlockSpec(memory_space=pltpu.SMEM)
    return pl.pallas_call(
        functools.partial(_attn_diff_kernel, lam_init=lam_init),
        grid=(bsz, N_HEADS_A, seq // tq),
        in_specs=[
            pl.BlockSpec((None, 128, tq), lambda b, h, i: (b, h, i)),
            pl.BlockSpec((None, seq, 128), lambda b, h, i: (b, 0, h)),
            pl.BlockSpec((None, None, DIFF_V_DIM + ONES_ROWS, seq), lambda b, h, i: (b, h, 0, 0)),
            pl.BlockSpec((None, 128, tq), lambda b, h, i: (b, h, i)),
            pl.BlockSpec((None, TK_A, n), lambda b, h, i: (h, 0, 0)),
            pl.BlockSpec((None, None, 2, kdim, tq), lambda b, h, i: (b, h, 0, 0, i)),
            pl.BlockSpec((None, None, 2, seq, kdim), lambda b, h, i: (b, h, 0, 0, 0)),
            smem, smem,
            pl.BlockSpec((DIFF_V_DIM, 1), lambda b, h, i: (0, 0)),
            vec, vec, vec, vec,
        ],
        out_specs=pl.BlockSpec((None, tq, 128), lambda b, h, i: (b, i, h)),
        out_shape=jax.ShapeDtypeStruct((bsz, seq, WIDTH_A), jnp.bfloat16),
        scratch_shapes=[
            pltpu.VMEM((128, n), jnp.bfloat16),
            pltpu.VMEM((TK_A, n), jnp.float32),
            pltpu.VMEM((TK_A, n), jnp.bfloat16),
            pltpu.VMEM((TK_A, n), jnp.bfloat16),
            pltpu.VMEM((1, n), jnp.float32),
            pltpu.VMEM((DIFF_V_DIM + ONES_ROWS, n), jnp.float32),
        ],
        compiler_params=pltpu.CompilerParams(
            dimension_semantics=("arbitrary", "arbitrary", "arbitrary"),
            vmem_limit_bytes=VMEM_LIMIT_BYTES),
        name="attn_diff",
    )(qa_t, ka, va_t, ga_t, bias, qa_f8, ka_f8, slopes, kmax, subln_col, lq1, lk1, lq2, lk2)


def _attn_gqa_kernel(qt_ref, k_ref, vt_ref, gt_ref, qf_ref, kf_ref, kmax_ref, o_ref,
                     qp_ref, qpf_ref, s_ref, p0_ref, p1_ref, m_ref, acc_ref):
    tq, tk = TQ_B, TK_B
    g = pl.program_id(1)
    n_kv = k_ref.shape[0] // tk
    n = GQA_GROUP * tq

    row = lax.broadcasted_iota(jnp.int32, (128, tq), 0)
    lo = g * HEAD_DIM_B
    mine = (row >= lo) & (row < lo + HEAD_DIM_B)
    for r in range(GQA_GROUP):
        q = qt_ref[r * 64:(r + 1) * 64, :].astype(jnp.float32)
        q2 = jnp.concatenate([q, q], axis=0)
        qp_ref[:, r * tq:(r + 1) * tq] = jnp.where(mine, q2, 0.0).astype(jnp.bfloat16)
        qpf_ref[:, r * tq:(r + 1) * tq] = qf_ref[r]

    whole = slice(0, tk)

    def key_rows(j, rows):
        return pl.ds(pl.multiple_of(j * tk + rows.start, KEY_ALIGN), rows.stop - rows.start)

    def logits(j, cols):
        return jnp.dot(k_ref[key_rows(j, whole), :], qp_ref[:, cols], preferred_element_type=jnp.float32)

    def fast_logits(j, rows, cols):
        return jnp.dot(kf_ref[key_rows(j, rows), :], qpf_ref[:, cols], preferred_element_type=jnp.float32)

    def values(j, rows=whole):
        return vt_ref[:, key_rows(j, rows)]

    no_offset = jnp.zeros((1, n), jnp.float32)
    q = qp_ref[...].astype(jnp.float32)
    q_norm = jnp.sqrt(jnp.sum(q * q, axis=0, keepdims=True))
    bound = q_norm * kmax_ref[0]
    f8_ok = ((jnp.max(q_norm) * (QK_SCALE_ROOT / QK_SCALE) < F8_SAFE)
             & (kmax_ref[0] * QK_SCALE_ROOT < F8_SAFE))

    _softmax_over_tiles(n_kv - 1, jnp.int32(0), lambda t: jnp.asarray(t, jnp.int32) + 1,
                        lambda cols: logits(jnp.int32(0), cols), logits, lambda j: no_offset, values,
                        fast_logits, lambda j: no_offset, values,
                        f8_ok, lambda m1: jnp.max(bound - m1) <= GUARD_LOG2,
                        KEY_CHUNK_B, LANE_CHUNK_B, s_ref, (p0_ref, p1_ref), m_ref, acc_ref)

    o = acc_ref[0:HEAD_DIM_B, :] * (1.0 / acc_ref[HEAD_DIM_B:HEAD_DIM_B + 1, :])
    o = jnp.concatenate([o[:, r * tq:(r + 1) * tq] for r in range(GQA_GROUP)], axis=0)
    gated = o * gt_ref[...].astype(jnp.float32)
    o_ref[...] = gated.T.astype(o_ref.dtype)


def _attn_gqa(qb_t, kb, vb_t, gb_t, qb_f8, kb_f8, kmax):
    bsz, seq, _ = kb.shape
    tq = TQ_B
    n = GQA_GROUP * tq
    kdim = SPLIT * HEAD_DIM_B
    return pl.pallas_call(
        _attn_gqa_kernel,
        grid=(bsz, N_KV_B, seq // tq),
        in_specs=[
            pl.BlockSpec((None, 256, tq), lambda b, g, i: (b, g, i)),
            pl.BlockSpec((None, seq, 128), lambda b, g, i: (b, 0, 0)),
            pl.BlockSpec((None, None, HEAD_DIM_B + ONES_ROWS, seq), lambda b, g, i: (b, g, 0, 0)),
            pl.BlockSpec((None, 256, tq), lambda b, g, i: (b, g, i)),
            pl.BlockSpec((None, GQA_GROUP, kdim, tq), lambda b, g, i: (b, g, 0, i)),
            pl.BlockSpec((None, None, seq, kdim), lambda b, g, i: (b, g, 0, 0)),
            pl.BlockSpec(memory_space=pltpu.SMEM),
        ],
        out_specs=pl.BlockSpec((None, tq, 256), lambda b, g, i: (b, i, g)),
        out_shape=jax.ShapeDtypeStruct((bsz, seq, WIDTH_B), jnp.bfloat16),
        scratch_shapes=[
            pltpu.VMEM((128, n), jnp.bfloat16),
            pltpu.VMEM((kdim, n), F8),
            pltpu.VMEM((TK_B, n), jnp.float32),
            pltpu.VMEM((TK_B, n), jnp.bfloat16),
            pltpu.VMEM((TK_B, n), jnp.bfloat16),
            pltpu.VMEM((1, n), jnp.float32),
            pltpu.VMEM((HEAD_DIM_B + ONES_ROWS, n), jnp.float32),
        ],
        compiler_params=pltpu.CompilerParams(
            dimension_semantics=("arbitrary", "arbitrary", "arbitrary"),
            vmem_limit_bytes=VMEM_LIMIT_BYTES),
        name="attn_gqa",
    )(qb_t, kb, vb_t, gb_t, qb_f8, kb_f8, kmax)


def _out_proj_kernel(ga_ref, gb_ref, wa_ref, wb_ref, x_ref, mod_ref, fw_ref, o_ref):
    y = jnp.dot(ga_ref[...], wa_ref[...], preferred_element_type=jnp.float32)
    y = y + jnp.dot(gb_ref[...], wb_ref[...], preferred_element_type=jnp.float32)
    z = x_ref[...] + mod_ref[2:3, :] * y
    ms = jnp.mean(z * z, axis=1, keepdims=True)
    o_ref[...] = z * lax.rsqrt(ms + NORM_EPS) * fw_ref[...]


def _out_proj(ga, gb, w_a, w_b, x, mod3, final_w):
    bsz, seq, d = x.shape
    ts = PROJ_TILE
    const = lambda b, i: (0, 0)
    return pl.pallas_call(
        _out_proj_kernel,
        grid=(bsz, seq // ts),
        in_specs=[
            pl.BlockSpec((None, ts, WIDTH_A), lambda b, i: (b, i, 0)),
            pl.BlockSpec((None, ts, WIDTH_B), lambda b, i: (b, i, 0)),
            pl.BlockSpec(w_a.shape, const),
            pl.BlockSpec(w_b.shape, const),
            pl.BlockSpec((None, ts, d), lambda b, i: (b, i, 0)),
            pl.BlockSpec((None, 3, d), lambda b, i: (b, 0, 0)),
            pl.BlockSpec((1, d), const),
        ],
        out_specs=pl.BlockSpec((None, ts, d), lambda b, i: (b, i, 0)),
        out_shape=jax.ShapeDtypeStruct((bsz, seq, d), jnp.float32),
        compiler_params=pltpu.CompilerParams(
            dimension_semantics=("arbitrary", "arbitrary"), vmem_limit_bytes=VMEM_LIMIT_BYTES),
        name="out_proj",
    )(ga, gb, w_a, w_b, x, mod3, final_w)


def _rope_tables(seq):
    pos = np.arange(seq)
    row = (pos // GRID_W).astype(np.float32)
    col = (pos % GRID_W).astype(np.float32)
    n_freq = ROT_HALF // 2
    freqs = (1.0 / (ROPE_THETA ** (np.arange(n_freq, dtype=np.float32) * 2.0 / ROT_HALF))).astype(np.float32)
    row, col, freqs = jnp.asarray(row), jnp.asarray(col), jnp.asarray(freqs)
    ang_r = row[:, None] * freqs[None, :]
    ang_c = col[:, None] * freqs[None, :]
    cos = jnp.concatenate([jnp.cos(ang_r), jnp.cos(ang_r), jnp.cos(ang_c), jnp.cos(ang_c)], axis=1)
    sin = jnp.concatenate([-jnp.sin(ang_r), jnp.sin(ang_r), -jnp.sin(ang_c), jnp.sin(ang_c)], axis=1)
    d = np.arange(HEAD_DIM_B)
    partner = np.where((d // n_freq) % 2 == 0, d + n_freq, d - n_freq)
    return cos, sin, partner


def _alibi_diagonal(slopes):
    sig = (slopes * LOG2E)[:, None, None]
    jl = jnp.arange(TK_A, dtype=jnp.float32)[None, :, None]
    il = jnp.tile(jnp.arange(TQ_A, dtype=jnp.float32), 2)[None, None, :]
    return sig * (il - jnp.abs(il - jl))


def kernel(x, c, w_ada, b_ada, norm_w, w_in, lambda_q1, lambda_k1, lambda_q2, lambda_k2,
           subln_w, q_norm_w, k_norm_w, w_out, final_norm_w):
    assert w_ada.shape[0] == 1, "single-layer problem: the final norm is fused into the output projection"
    bsz, seq, d = x.shape
    assert seq % TQ_A == 0 and seq % TK_B == 0 and seq % PROJ_TILE == 0
    assert (seq // TK_A) % 2 == 0 and (seq // TK_B) % 2 == 0
    bf = jnp.bfloat16
    kdim = SPLIT * DIFF_HEAD_DIM
    cos, sin, partner = _rope_tables(seq)
    cosq, sinq = cos.T, sin.T
    cosk, sink = jnp.tile(cos, (1, N_KV_B)), jnp.tile(sin, (1, N_KV_B))
    slopes = jnp.asarray(2.0 ** (-8.0 * np.arange(1, N_HEADS_A + 1) / N_HEADS_A), jnp.float32)
    alibi = _alibi_diagonal(slopes)
    lam_init = 0.8 - 0.6 * math.exp(-0.3 * 0)
    grp = jnp.asarray(np.arange(WIDTH_A)[:, None] // DIFF_HEAD_DIM == np.arange(128)[None, :], jnp.float32)

    w = w_in[0]
    q_a, k_a, v_a, g_a = w[:, 0:512], w[:, 512:1024], w[:, 1024:1536], w[:, 1536:2048]
    q_b, k_b, v_b, g_b = w[:, 2048:2560], w[:, 2560:2688], w[:, 2688:2816], w[:, 2816:3328]
    partner2 = np.concatenate([partner, partner + HEAD_DIM_B])
    w_tok = jnp.concatenate([k_a, k_b, k_b[:, partner2]], axis=1).astype(bf)
    w_feat_t = jnp.concatenate([q_a, v_a, g_a, q_b, v_b, g_b], axis=1).T.astype(bf)
    knw = jnp.tile(k_norm_w[0], N_KV_B)
    wo = w_out[0].astype(bf)

    mod3 = _adaln_mod(c, w_ada[0], b_ada[0]).reshape(bsz, 3, d)
    ka, kb, qa_t, va_t, ga_t, qb_t, vb_t, gb_t, kn, ka_f8, kb_f8, qa_f8, qb_f8 = _in_proj(
        x, mod3, norm_w[0].reshape(1, d), w_tok, w_feat_t,
        q_norm_w[0].reshape(HEAD_DIM_B, 1), knw.reshape(1, 128), knw[partner2].reshape(1, 128),
        cosq, sinq, cosk, sink, grp)
    kmax_a = (jnp.sqrt(jnp.max(kn[:, :, 0, :2 * N_HEADS_A], axis=1)) * NORM_MARGIN).reshape(-1)
    kmax_b = (math.sqrt(HEAD_DIM_B) * NORM_MARGIN * jnp.max(jnp.abs(k_norm_w[0]))).reshape(1)
    oa = _attn_diff(qa_t, ka, va_t, ga_t, alibi,
                    qa_f8.reshape(bsz, N_HEADS_A, 2, kdim, seq), ka_f8.reshape(bsz, N_HEADS_A, 2, seq, kdim),
                    slopes, kmax_a, subln_w[0].reshape(DIFF_V_DIM, 1),
                    lambda_q1[0].reshape(1, -1), lambda_k1[0].reshape(1, -1),
                    lambda_q2[0].reshape(1, -1), lambda_k2[0].reshape(1, -1), lam_init)
    ob = _attn_gqa(qb_t, kb, vb_t, gb_t, qb_f8, kb_f8, kmax_b)
    return _out_proj(oa, ob, wo[:WIDTH_A], wo[WIDTH_A:], x, mod3, final_norm_w.reshape(1, d))
```

```python
import functools
import math

import numpy as np
import jax
import jax.numpy as jnp
from jax import lax
from jax.experimental import pallas as pl
from jax.experimental.pallas import tpu as pltpu

D_MODEL = 1024
N_HEADS_A = 4
DIFF_HEAD_DIM = 64
DIFF_V_DIM = 128
WIDTH_A = 512
N_KV_B = 2
GQA_GROUP = 4
HEAD_DIM_B = 64
WIDTH_B = 512
GRID_W = 64
ROPE_THETA = 10000.0
ROT_HALF = 32
NORM_EPS = 1e-6
MXU_TILE = 256
LANE_CHUNK_A = MXU_TILE
LANE_CHUNK_B = MXU_TILE
KEY_CHUNK_A = 512
KEY_CHUNK_B = 512
KEY_ALIGN = MXU_TILE
ONES_ROWS = 16
LOG2E = 1.4426950408889634
NEG_BIG = -1e30
GUARD_LOG2 = 100.0
NORM_MARGIN = 1.01
F8 = jnp.float8_e4m3fn
F8_SAFE = 440.0
SPLIT = 4
QK_SCALE = LOG2E / math.sqrt(DIFF_HEAD_DIM)
QK_SCALE_ROOT = math.sqrt(QK_SCALE)

VMEM_LIMIT_BYTES = 56 * 1024 * 1024

PROJ_TILE = 1024
TQ_A = 512
TK_A = 512
TQ_B = 512
TK_B = 512

assert DIFF_HEAD_DIM == HEAD_DIM_B and TQ_A % LANE_CHUNK_A == 0 and TQ_B % LANE_CHUNK_B == 0


def _silu(v):
    return v * (1.0 / (1.0 + jnp.exp(-v)))


def _split_f8(v):
    hi = v.astype(F8)
    lo = (v - hi.astype(jnp.float32)).astype(F8)
    return hi, lo


def _mod_kernel(c_ref, w_ref, b_ref, o_ref):
    c = c_ref[...]
    o_ref[...] = jnp.dot(_silu(c), w_ref[...], preferred_element_type=jnp.float32,
                         precision=lax.Precision.HIGHEST) + b_ref[...]


def _adaln_mod(c, w_ada, b_ada):
    bsz, d = c.shape
    n = w_ada.shape[1]
    tn = 1024
    return pl.pallas_call(
        _mod_kernel,
        grid=(n // tn,),
        in_specs=[pl.BlockSpec((bsz, d), lambda j: (0, 0)),
                  pl.BlockSpec((d, tn), lambda j: (0, j)),
                  pl.BlockSpec((1, tn), lambda j: (0, j))],
        out_specs=pl.BlockSpec((bsz, tn), lambda j: (0, j)),
        out_shape=jax.ShapeDtypeStruct((bsz, n), jnp.float32),
        name="adaln_mod",
    )(c, w_ada, b_ada.reshape(1, n))


def _store_keys_f8(k2, out_ref, first):
    lane = lax.broadcasted_iota(jnp.int32, k2.shape, 1)
    swapped = pltpu.roll(k2, HEAD_DIM_B, axis=1)
    for g, dup in enumerate((jnp.where(lane < HEAD_DIM_B, k2, swapped),
                             jnp.where(lane < HEAD_DIM_B, swapped, k2))):
        hi, lo = _split_f8(dup)
        out_ref[first + g, :, 0:128] = hi
        out_ref[first + g, :, 128:256] = lo


def _store_queries_f8(q, out_ref, g):
    hi, lo = _split_f8(q)
    for part, val in enumerate((hi, lo, hi, lo)):
        out_ref[g, part * 64:(part + 1) * 64, :] = val


def _in_proj_kernel(x_ref, mod_ref, nw_ref, wtok_ref, wfeat_ref, qnw_ref, knw_ref, knws_ref,
                    cosq_ref, sinq_ref, cosk_ref, sink_ref, grp_ref,
                    ka_ref, kb_ref, qa_ref, va_ref, ga_ref, qb_ref, vb_ref, gb_ref, kn_ref,
                    kaf_ref, kbf_ref, qaf_ref, qbf_ref):
    x = x_ref[...]
    shift = mod_ref[0:1, :]
    scale = mod_ref[1:2, :]
    ms = jnp.mean(x * x, axis=1, keepdims=True)
    h = (x * lax.rsqrt(ms + NORM_EPS)) * nw_ref[...] * (1.0 + scale) + shift
    hb = h.astype(jnp.bfloat16)

    tok = jnp.dot(hb, wtok_ref[...], preferred_element_type=jnp.float32)
    ka = tok[:, 0:WIDTH_A]
    ka_ref[...] = ka.astype(jnp.bfloat16)
    for pair in range(N_HEADS_A):
        _store_keys_f8(ka[:, pair * 128:(pair + 1) * 128] * QK_SCALE_ROOT, kaf_ref, 2 * pair)
    kn = jnp.dot(ka * ka, grp_ref[...], preferred_element_type=jnp.float32)
    kn_ref[...] = jnp.max(kn, axis=0, keepdims=True)
    kb = tok[:, 512:640]
    kbs = tok[:, 640:768]
    sq = kb * kb
    r0 = lax.rsqrt(jnp.sum(sq[:, 0:64], axis=1, keepdims=True) * (1.0 / HEAD_DIM_B) + NORM_EPS)
    r1 = lax.rsqrt(jnp.sum(sq[:, 64:128], axis=1, keepdims=True) * (1.0 / HEAD_DIM_B) + NORM_EPS)
    lane = lax.broadcasted_iota(jnp.int32, kb.shape, 1)
    r = jnp.where(lane < HEAD_DIM_B, r0, r1)
    kr = r * ((kb * knw_ref[...]) * cosk_ref[...] + (kbs * knws_ref[...]) * sink_ref[...])
    kb_ref[...] = kr.astype(jnp.bfloat16)
    _store_keys_f8(kr * QK_SCALE_ROOT, kbf_ref, 0)

    def feat(lo, hi):
        return lax.dot_general(wfeat_ref[lo:hi, :], hb, (((1,), (1,)), ((), ())),
                               preferred_element_type=jnp.float32)

    qa = feat(0, 512)
    qa_ref[...] = (qa * QK_SCALE).astype(jnp.bfloat16)
    for g in range(2 * N_HEADS_A):
        _store_queries_f8(qa[g * 64:(g + 1) * 64, :] * QK_SCALE_ROOT, qaf_ref, g)
    va = feat(512, 1024).astype(jnp.bfloat16)
    ones = jnp.ones((ONES_ROWS, va.shape[1]), jnp.bfloat16)
    for hd in range(N_HEADS_A):
        va_ref[hd, 0:DIFF_V_DIM, :] = va[hd * DIFF_V_DIM:(hd + 1) * DIFF_V_DIM, :]
        va_ref[hd, DIFF_V_DIM:DIFF_V_DIM + ONES_ROWS, :] = ones
    ga_ref[...] = _silu(feat(1024, 1536)).astype(jnp.bfloat16)
    qb = feat(1536, 2048)
    cq = cosq_ref[...]
    sq_ = sinq_ref[...]
    qnw = qnw_ref[...]
    for hd in range(N_KV_B * GQA_GROUP):
        q = qb[hd * 64:(hd + 1) * 64, :]
        rq = lax.rsqrt(jnp.mean(q * q, axis=0, keepdims=True) + NORM_EPS)
        qn = q * rq * qnw
        partner = jnp.concatenate([qn[16:32], qn[0:16], qn[48:64], qn[32:48]], axis=0)
        rot = qn * cq + partner * sq_
        qb_ref[hd * 64:(hd + 1) * 64, :] = (rot * QK_SCALE).astype(jnp.bfloat16)
        _store_queries_f8(rot * QK_SCALE_ROOT, qbf_ref, hd)
    vb = feat(2048, 2176).astype(jnp.bfloat16)
    for g in range(N_KV_B):
        vb_ref[g, 0:HEAD_DIM_B, :] = vb[g * HEAD_DIM_B:(g + 1) * HEAD_DIM_B, :]
        vb_ref[g, HEAD_DIM_B:HEAD_DIM_B + ONES_ROWS, :] = ones
    gb_ref[...] = _silu(feat(2176, 2688)).astype(jnp.bfloat16)


def _in_proj(x, mod3, norm_w, w_tok, w_feat_t, qnw_col, knw_row, knws_row, cosq, sinq, cosk, sink,
             grp):
    bsz, seq, d = x.shape
    ts = PROJ_TILE
    bf = jnp.bfloat16
    kdim = SPLIT * DIFF_HEAD_DIM
    const = lambda b, i: (0, 0)
    feat_spec = lambda rows: pl.BlockSpec((None, rows, ts), lambda b, i: (b, 0, i))
    return pl.pallas_call(
        _in_proj_kernel,
        grid=(bsz, seq // ts),
        in_specs=[
            pl.BlockSpec((None, ts, d), lambda b, i: (b, i, 0)),
            pl.BlockSpec((None, 3, d), lambda b, i: (b, 0, 0)),
            pl.BlockSpec((1, d), const),
            pl.BlockSpec(w_tok.shape, const),
            pl.BlockSpec(w_feat_t.shape, const),
            pl.BlockSpec((HEAD_DIM_B, 1), const),
            pl.BlockSpec((1, 128), const),
            pl.BlockSpec((1, 128), const),
            pl.BlockSpec((HEAD_DIM_B, ts), lambda b, i: (0, i)),
            pl.BlockSpec((HEAD_DIM_B, ts), lambda b, i: (0, i)),
            pl.BlockSpec((ts, 128), lambda b, i: (i, 0)),
            pl.BlockSpec((ts, 128), lambda b, i: (i, 0)),
            pl.BlockSpec(grp.shape, const),
        ],
        out_specs=[
            pl.BlockSpec((None, ts, WIDTH_A), lambda b, i: (b, i, 0)),
            pl.BlockSpec((None, ts, 128), lambda b, i: (b, i, 0)),
            feat_spec(512),
            pl.BlockSpec((None, N_HEADS_A, DIFF_V_DIM + ONES_ROWS, ts), lambda b, i: (b, 0, 0, i)),
            feat_spec(512), feat_spec(512),
            pl.BlockSpec((None, N_KV_B, HEAD_DIM_B + ONES_ROWS, ts), lambda b, i: (b, 0, 0, i)),
            feat_spec(512),
            pl.BlockSpec((None, None, 1, 128), lambda b, i: (b, i, 0, 0)),
            pl.BlockSpec((None, 2 * N_HEADS_A, ts, kdim), lambda b, i: (b, 0, i, 0)),
            pl.BlockSpec((None, N_KV_B, ts, kdim), lambda b, i: (b, 0, i, 0)),
            pl.BlockSpec((None, 2 * N_HEADS_A, kdim, ts), lambda b, i: (b, 0, 0, i)),
            pl.BlockSpec((None, N_KV_B * GQA_GROUP, kdim, ts), lambda b, i: (b, 0, 0, i)),
        ],
        out_shape=[
            jax.ShapeDtypeStruct((bsz, seq, WIDTH_A), bf),
            jax.ShapeDtypeStruct((bsz, seq, 128), bf),
            jax.ShapeDtypeStruct((bsz, 512, seq), bf),
            jax.ShapeDtypeStruct((bsz, N_HEADS_A, DIFF_V_DIM + ONES_ROWS, seq), bf),
            jax.ShapeDtypeStruct((bsz, 512, seq), bf),
            jax.ShapeDtypeStruct((bsz, 512, seq), bf),
            jax.ShapeDtypeStruct((bsz, N_KV_B, HEAD_DIM_B + ONES_ROWS, seq), bf),
            jax.ShapeDtypeStruct((bsz, 512, seq), bf),
            jax.ShapeDtypeStruct((bsz, seq // ts, 1, 128), jnp.float32),
            jax.ShapeDtypeStruct((bsz, 2 * N_HEADS_A, seq, kdim), F8),
            jax.ShapeDtypeStruct((bsz, N_KV_B, seq, kdim), F8),
            jax.ShapeDtypeStruct((bsz, 2 * N_HEADS_A, kdim, seq), F8),
            jax.ShapeDtypeStruct((bsz, N_KV_B * GQA_GROUP, kdim, seq), F8),
        ],
        compiler_params=pltpu.CompilerParams(
            dimension_semantics=("arbitrary", "arbitrary"), vmem_limit_bytes=VMEM_LIMIT_BYTES),
        name="in_proj",
    )(x, mod3, norm_w, w_tok, w_feat_t, qnw_col, knw_row, knws_row, cosq, sinq, cosk, sink, grp)


def _softmax_over_tiles(n_rest, first_tile, tile_fn, first_logits_fn, logits_fn, off_fn, vt_fn,
                        fast_logits_fn, fast_off_fn, fast_vt_fn,
                        f8_ok, bound_ok_fn, key_chunk, lane_chunk, s_ref, p_refs, m_ref, acc_ref):
    chunks = [slice(c, c + lane_chunk) for c in range(0, m_ref.shape[1], lane_chunk)]
    tk = s_ref.shape[0]
    pieces = [(slice(r, r + key_chunk), cols) for r in range(0, tk, key_chunk) for cols in chunks]
    acc_ref[...] = jnp.zeros(acc_ref.shape, jnp.float32)

    off_first = off_fn(first_tile)

    def first_scores(fast):
        for cols in chunks:
            u = first_logits_fn(cols, fast)
            s_ref[:, cols] = u
            m_ref[:, cols] = jnp.max(u, axis=0, keepdims=True) + off_first[:, cols]

    first_scores(True)

    @pl.when(jnp.logical_not(f8_ok))
    def _():
        first_scores(False)

    fixed_ok = bound_ok_fn(m_ref[...]) & f8_ok

    def first_probs(rows, cols):
        shift = m_ref[:, cols] - off_first[:, cols]
        return jnp.exp2((s_ref[rows, cols] - shift).astype(jnp.bfloat16))

    def probs(t, slot, rows, cols):
        j = tile_fn(t)
        shift = m_ref[:, cols] - fast_off_fn(j)[:, cols]
        p_refs[slot][rows, cols] = jnp.exp2(
            (fast_logits_fn(j, rows, cols) - shift).astype(jnp.bfloat16))

    def tile_values(t):
        if t < 0:
            return {rows.start: vt_fn(first_tile, rows) for rows, _ in pieces}
        return {rows.start: fast_vt_fn(tile_fn(t), rows) for rows, _ in pieces}

    def accumulate(slot, rows, cols, vts):
        acc_ref[:, cols] += jnp.dot(vts[rows.start], p_refs[slot][rows, cols],
                                    preferred_element_type=jnp.float32)

    @pl.when(fixed_ok)
    def _():
        for rows, cols in pieces:
            p_refs[1][rows, cols] = first_probs(rows, cols)
        for t in range(-1, n_rest - 1):
            vts = tile_values(t)
            for rows, cols in pieces:
                probs(t + 1, (t + 1) % 2, rows, cols)
                accumulate(t % 2, rows, cols, vts)
        vts = tile_values(n_rest - 1)
        for rows, cols in pieces:
            accumulate((n_rest - 1) % 2, rows, cols, vts)

    @pl.when(jnp.logical_not(fixed_ok))
    def _():
        vt = vt_fn(first_tile)
        for cols in chunks:
            acc_ref[:, cols] = jnp.dot(vt, first_probs(slice(0, tk), cols),
                                       preferred_element_type=jnp.float32)

        def body(t, carry):
            j = tile_fn(t)
            off = off_fn(j)
            vt = vt_fn(j)
            for cols in chunks:
                u = logits_fn(j, cols)
                m_old = m_ref[:, cols]
                m_new = jnp.maximum(m_old, jnp.max(u, axis=0, keepdims=True) + off[:, cols])
                alpha = jnp.exp2(m_old - m_new)
                p = jnp.exp2((u - (m_new - off[:, cols])).astype(jnp.bfloat16))
                acc_ref[:, cols] = alpha * acc_ref[:, cols] + jnp.dot(
                    vt, p, preferred_element_type=jnp.float32)
                m_ref[:, cols] = m_new
            return carry

        lax.fori_loop(0, n_rest, body, 0)


def _attn_diff_kernel(qt_ref, k_ref, vt_ref, gt_ref, bias_ref, qf_ref, kf_ref, slope_ref, kmax_ref,
                      sublnw_ref, lq1_ref, lk1_ref, lq2_ref, lk2_ref, o_ref,
                      qbd_ref, s_ref, p0_ref, p1_ref, m_ref, acc_ref, *, lam_init):
    tq, tk = TQ_A, TK_A
    b = pl.program_id(0)
    hd = pl.program_id(1)
    qi = pl.program_id(2)
    n_kv = k_ref.shape[0] // tk
    sigma = slope_ref[hd] * LOG2E

    zeros = jnp.zeros((DIFF_HEAD_DIM, tq), jnp.bfloat16)
    qbd_ref[0:64, 0:tq] = qt_ref[0:64, :]
    qbd_ref[64:128, 0:tq] = zeros
    qbd_ref[0:64, tq:2 * tq] = zeros
    qbd_ref[64:128, tq:2 * tq] = qt_ref[64:128, :]

    lane = lax.broadcasted_iota(jnp.int32, (1, 2 * tq), 1)
    il = jnp.where(lane >= tq, lane - tq, lane).astype(jnp.float32)

    whole = slice(0, tk)

    def key_rows(j, rows):
        return pl.ds(pl.multiple_of(j * tk + rows.start, KEY_ALIGN), rows.stop - rows.start)

    def scores(j, cols):
        return jnp.dot(k_ref[key_rows(j, whole), :], qbd_ref[:, cols], preferred_element_type=jnp.float32)

    def first_logits(cols, fast):
        return (fast_logits(qi, whole, cols) if fast else scores(qi, cols)) + bias_ref[:, cols]

    def logits(j, cols):
        key = lax.broadcasted_iota(jnp.int32, (tk, LANE_CHUNK_A), 0).astype(jnp.float32)
        return scores(j, cols) + jnp.where(j < qi, sigma, -sigma) * key

    def offset(j):
        per_query = jnp.where(j > qi, 2.0 * sigma, 0.0)
        return per_query * il - sigma * (jnp.abs(j - qi) * tk).astype(jnp.float32)

    def values(j, rows=whole):
        return vt_ref[:, key_rows(j, rows)]

    def fast_logits(j, rows, cols):
        mp = cols.start // tq
        queries = slice(cols.start - mp * tq, cols.stop - mp * tq)
        return jnp.dot(kf_ref[mp, key_rows(j, rows), :], qf_ref[mp, :, queries],
                       preferred_element_type=jnp.float32)

    def fast_offset(j):
        return offset(j) + jnp.where(j < qi, sigma * tk, 0.0)

    jl = lax.broadcasted_iota(jnp.int32, (1, tk), 1).astype(jnp.float32)
    factor_left = jnp.exp2(sigma * (jl - tk))
    factor_right = jnp.exp2(-sigma * jl)

    def fast_values(j, rows):
        factor = jnp.where(j < qi, factor_left[:, rows], factor_right[:, rows])
        return (values(j, rows).astype(jnp.float32) * factor).astype(jnp.bfloat16)

    q = qbd_ref[...].astype(jnp.float32)
    q_norm = jnp.sqrt(jnp.sum(q * q, axis=0, keepdims=True))
    k_norm0 = kmax_ref[(b * N_HEADS_A + hd) * 2]
    k_norm1 = kmax_ref[(b * N_HEADS_A + hd) * 2 + 1]
    bound = q_norm * jnp.where(lane < tq, k_norm0, k_norm1) + sigma * il
    f8_ok = ((jnp.max(q_norm) * (QK_SCALE_ROOT / QK_SCALE) < F8_SAFE)
             & (jnp.maximum(k_norm0, k_norm1) * QK_SCALE_ROOT < F8_SAFE))

    _softmax_over_tiles(n_kv - 1, qi, lambda t: jnp.where(t >= qi, t + 1, t),
                        first_logits, logits, offset, values,
                        fast_logits, fast_offset, fast_values,
                        f8_ok, lambda m1: jnp.max(bound - m1) <= GUARD_LOG2,
                        KEY_CHUNK_A, LANE_CHUNK_A, s_ref, (p0_ref, p1_ref), m_ref, acc_ref)

    lam = (jnp.exp(jnp.sum(lq1_ref[...] * lk1_ref[...], axis=1, keepdims=True))
           - jnp.exp(jnp.sum(lq2_ref[...] * lk2_ref[...], axis=1, keepdims=True)) + lam_init)
    inv_l = 1.0 / acc_ref[DIFF_V_DIM:DIFF_V_DIM + 1, :]
    o1 = acc_ref[0:DIFF_V_DIM, 0:tq] * inv_l[:, 0:tq]
    o2 = acc_ref[0:DIFF_V_DIM, tq:2 * tq] * inv_l[:, tq:2 * tq]
    diff = o1 - lam * o2
    ms = jnp.mean(diff * diff, axis=0, keepdims=True)
    y = diff * lax.rsqrt(ms + NORM_EPS) * sublnw_ref[...] * (1.0 - lam_init)
    gated = y * gt_ref[...].astype(jnp.float32)
    o_ref[...] = gated.T.astype(o_ref.dtype)


def _attn_diff(qa_t, ka, va_t, ga_t, bias, qa_f8, ka_f8, slopes, kmax, subln_col, lq1, lk1, lq2, lk2,
               lam_init):
    bsz, seq, _ = ka.shape
    tq = TQ_A
    n = 2 * tq
    kdim = SPLIT * DIFF_HEAD_DIM
    vec = pl.BlockSpec((1, DIFF_HEAD_DIM), lambda b, h, i: (0, 0))
    smem = pl.BlockSpec(memory_space=pltpu.SMEM)
    return pl.pallas_call(
        functools.partial(_attn_diff_kernel, lam_init=lam_init),
        grid=(bsz, N_HEADS_A, seq // tq),
        in_specs=[
            pl.BlockSpec((None, 128, tq), lambda b, h, i: (b, h, i)),
            pl.BlockSpec((None, seq, 128), lambda b, h, i: (b, 0, h)),
            pl.BlockSpec((None, None, DIFF_V_DIM + ONES_ROWS, seq), lambda b, h, i: (b, h, 0, 0)),
            pl.BlockSpec((None, 128, tq), lambda b, h, i: (b, h, i)),
            pl.BlockSpec((None, TK_A, n), lambda b, h, i: (h, 0, 0)),
            pl.BlockSpec((None, None, 2, kdim, tq), lambda b, h, i: (b, h, 0, 0, i)),
            pl.BlockSpec((None, None, 2, seq, kdim), lambda b, h, i: (b, h, 0, 0, 0)),
            smem, smem,
            pl.BlockSpec((DIFF_V_DIM, 1), lambda b, h, i: (0, 0)),
            vec, vec, vec, vec,
        ],
        out_specs=pl.BlockSpec((None, tq, 128), lambda b, h, i: (b, i, h)),
        out_shape=jax.ShapeDtypeStruct((bsz, seq, WIDTH_A), jnp.bfloat16),
        scratch_shapes=[
            pltpu.VMEM((128, n), jnp.bfloat16),
            pltpu.VMEM((TK_A, n), jnp.float32),
            pltpu.VMEM((TK_A, n), jnp.bfloat16),
            pltpu.VMEM((TK_A, n), jnp.bfloat16),
            pltpu.VMEM((1, n), jnp.float32),
            pltpu.VMEM((DIFF_V_DIM + ONES_ROWS, n), jnp.float32),
        ],
        compiler_params=pltpu.CompilerParams(
            dimension_semantics=("arbitrary", "arbitrary", "arbitrary"),
            vmem_limit_bytes=VMEM_LIMIT_BYTES),
        name="attn_diff",
    )(qa_t, ka, va_t, ga_t, bias, qa_f8, ka_f8, slopes, kmax, subln_col, lq1, lk1, lq2, lk2)


def _attn_gqa_kernel(qt_ref, k_ref, vt_ref, gt_ref, qf_ref, kf_ref, kmax_ref, o_ref,
                     qp_ref, qpf_ref, s_ref, p0_ref, p1_ref, m_ref, acc_ref):
    tq, tk = TQ_B, TK_B
    g = pl.program_id(1)
    n_kv = k_ref.shape[0] // tk
    n = GQA_GROUP * tq

    row = lax.broadcasted_iota(jnp.int32, (128, tq), 0)
    lo = g * HEAD_DIM_B
    mine = (row >= lo) & (row < lo + HEAD_DIM_B)
    for r in range(GQA_GROUP):
        q = qt_ref[r * 64:(r + 1) * 64, :].astype(jnp.float32)
        q2 = jnp.concatenate([q, q], axis=0)
        qp_ref[:, r * tq:(r + 1) * tq] = jnp.where(mine, q2, 0.0).astype(jnp.bfloat16)
        qpf_ref[:, r * tq:(r + 1) * tq] = qf_ref[r]

    whole = slice(0, tk)

    def key_rows(j, rows):
        return pl.ds(pl.multiple_of(j * tk + rows.start, KEY_ALIGN), rows.stop - rows.start)

    def logits(j, cols):
        return jnp.dot(k_ref[key_rows(j, whole), :], qp_ref[:, cols], preferred_element_type=jnp.float32)

    def fast_logits(j, rows, cols):
        return jnp.dot(kf_ref[key_rows(j, rows), :], qpf_ref[:, cols], preferred_element_type=jnp.float32)

    def first_logits(cols, fast):
        return fast_logits(jnp.int32(0), whole, cols) if fast else logits(jnp.int32(0), cols)

    def values(j, rows=whole):
        return vt_ref[:, key_rows(j, rows)]

    no_offset = jnp.zeros((1, n), jnp.float32)
    q = qp_ref[...].astype(jnp.float32)
    q_norm = jnp.sqrt(jnp.sum(q * q, axis=0, keepdims=True))
    bound = q_norm * kmax_ref[0]
    f8_ok = ((jnp.max(q_norm) * (QK_SCALE_ROOT / QK_SCALE) < F8_SAFE)
             & (kmax_ref[0] * QK_SCALE_ROOT < F8_SAFE))

    _softmax_over_tiles(n_kv - 1, jnp.int32(0), lambda t: jnp.asarray(t, jnp.int32) + 1,
                        first_logits, logits, lambda j: no_offset, values,
                        fast_logits, lambda j: no_offset, values,
                        f8_ok, lambda m1: jnp.max(bound - m1) <= GUARD_LOG2,
                        KEY_CHUNK_B, LANE_CHUNK_B, s_ref, (p0_ref, p1_ref), m_ref, acc_ref)

    o = acc_ref[0:HEAD_DIM_B, :] * (1.0 / acc_ref[HEAD_DIM_B:HEAD_DIM_B + 1, :])
    o = jnp.concatenate([o[:, r * tq:(r + 1) * tq] for r in range(GQA_GROUP)], axis=0)
    gated = o * gt_ref[...].astype(jnp.float32)
    o_ref[...] = gated.T.astype(o_ref.dtype)


def _attn_gqa(qb_t, kb, vb_t, gb_t, qb_f8, kb_f8, kmax):
    bsz, seq, _ = kb.shape
    tq = TQ_B
    n = GQA_GROUP * tq
    kdim = SPLIT * HEAD_DIM_B
    return pl.pallas_call(
        _attn_gqa_kernel,
        grid=(bsz, N_KV_B, seq // tq),
        in_specs=[
            pl.BlockSpec((None, 256, tq), lambda b, g, i: (b, g, i)),
            pl.BlockSpec((None, seq, 128), lambda b, g, i: (b, 0, 0)),
            pl.BlockSpec((None, None, HEAD_DIM_B + ONES_ROWS, seq), lambda b, g, i: (b, g, 0, 0)),
            pl.BlockSpec((None, 256, tq), lambda b, g, i: (b, g, i)),
            pl.BlockSpec((None, GQA_GROUP, kdim, tq), lambda b, g, i: (b, g, 0, i)),
            pl.BlockSpec((None, None, seq, kdim), lambda b, g, i: (b, g, 0, 0)),
            pl.BlockSpec(memory_space=pltpu.SMEM),
        ],
        out_specs=pl.BlockSpec((None, tq, 256), lambda b, g, i: (b, i, g)),
        out_shape=jax.ShapeDtypeStruct((bsz, seq, WIDTH_B), jnp.bfloat16),
        scratch_shapes=[
            pltpu.VMEM((128, n), jnp.bfloat16),
            pltpu.VMEM((kdim, n), F8),
            pltpu.VMEM((TK_B, n), jnp.float32),
            pltpu.VMEM((TK_B, n), jnp.bfloat16),
            pltpu.VMEM((TK_B, n), jnp.bfloat16),
            pltpu.VMEM((1, n), jnp.float32),
            pltpu.VMEM((HEAD_DIM_B + ONES_ROWS, n), jnp.float32),
        ],
        compiler_params=pltpu.CompilerParams(
            dimension_semantics=("arbitrary", "arbitrary", "arbitrary"),
            vmem_limit_bytes=VMEM_LIMIT_BYTES),
        name="attn_gqa",
    )(qb_t, kb, vb_t, gb_t, qb_f8, kb_f8, kmax)


def _out_proj_kernel(ga_ref, gb_ref, wa_ref, wb_ref, x_ref, mod_ref, fw_ref, o_ref):
    y = jnp.dot(ga_ref[...], wa_ref[...], preferred_element_type=jnp.float32)
    y = y + jnp.dot(gb_ref[...], wb_ref[...], preferred_element_type=jnp.float32)
    z = x_ref[...] + mod_ref[2:3, :] * y
    ms = jnp.mean(z * z, axis=1, keepdims=True)
    o_ref[...] = z * lax.rsqrt(ms + NORM_EPS) * fw_ref[...]


def _out_proj(ga, gb, w_a, w_b, x, mod3, final_w):
    bsz, seq, d = x.shape
    ts = PROJ_TILE
    const = lambda b, i: (0, 0)
    return pl.pallas_call(
        _out_proj_kernel,
        grid=(bsz, seq // ts),
        in_specs=[
            pl.BlockSpec((None, ts, WIDTH_A), lambda b, i: (b, i, 0)),
            pl.BlockSpec((None, ts, WIDTH_B), lambda b, i: (b, i, 0)),
            pl.BlockSpec(w_a.shape, const),
            pl.BlockSpec(w_b.shape, const),
            pl.BlockSpec((None, ts, d), lambda b, i: (b, i, 0)),
            pl.BlockSpec((None, 3, d), lambda b, i: (b, 0, 0)),
            pl.BlockSpec((1, d), const),
        ],
        out_specs=pl.BlockSpec((None, ts, d), lambda b, i: (b, i, 0)),
        out_shape=jax.ShapeDtypeStruct((bsz, seq, d), jnp.float32),
        compiler_params=pltpu.CompilerParams(
            dimension_semantics=("arbitrary", "arbitrary"), vmem_limit_bytes=VMEM_LIMIT_BYTES),
        name="out_proj",
    )(ga, gb, w_a, w_b, x, mod3, final_w)


def _rope_tables(seq):
    pos = np.arange(seq)
    row = (pos // GRID_W).astype(np.float32)
    col = (pos % GRID_W).astype(np.float32)
    n_freq = ROT_HALF // 2
    freqs = (1.0 / (ROPE_THETA ** (np.arange(n_freq, dtype=np.float32) * 2.0 / ROT_HALF))).astype(np.float32)
    row, col, freqs = jnp.asarray(row), jnp.asarray(col), jnp.asarray(freqs)
    ang_r = row[:, None] * freqs[None, :]
    ang_c = col[:, None] * freqs[None, :]
    cos = jnp.concatenate([jnp.cos(ang_r), jnp.cos(ang_r), jnp.cos(ang_c), jnp.cos(ang_c)], axis=1)
    sin = jnp.concatenate([-jnp.sin(ang_r), jnp.sin(ang_r), -jnp.sin(ang_c), jnp.sin(ang_c)], axis=1)
    d = np.arange(HEAD_DIM_B)
    partner = np.where((d // n_freq) % 2 == 0, d + n_freq, d - n_freq)
    return cos, sin, partner


def _alibi_diagonal(slopes):
    sig = (slopes * LOG2E)[:, None, None]
    jl = jnp.arange(TK_A, dtype=jnp.float32)[None, :, None]
    il = jnp.tile(jnp.arange(TQ_A, dtype=jnp.float32), 2)[None, None, :]
    return sig * (il - jnp.abs(il - jl))


def kernel(x, c, w_ada, b_ada, norm_w, w_in, lambda_q1, lambda_k1, lambda_q2, lambda_k2,
           subln_w, q_norm_w, k_norm_w, w_out, final_norm_w):
    assert w_ada.shape[0] == 1, "single-layer problem: the final norm is fused into the output projection"
    bsz, seq, d = x.shape
    assert seq % TQ_A == 0 and seq % TK_B == 0 and seq % PROJ_TILE == 0
    assert (seq // TK_A) % 2 == 0 and (seq // TK_B) % 2 == 0
    bf = jnp.bfloat16
    kdim = SPLIT * DIFF_HEAD_DIM
    cos, sin, partner = _rope_tables(seq)
    cosq, sinq = cos.T, sin.T
    cosk, sink = jnp.tile(cos, (1, N_KV_B)), jnp.tile(sin, (1, N_KV_B))
    slopes = jnp.asarray(2.0 ** (-8.0 * np.arange(1, N_HEADS_A + 1) / N_HEADS_A), jnp.float32)
    alibi = _alibi_diagonal(slopes)
    lam_init = 0.8 - 0.6 * math.exp(-0.3 * 0)
    grp = jnp.asarray(np.arange(WIDTH_A)[:, None] // DIFF_HEAD_DIM == np.arange(128)[None, :], jnp.float32)

    w = w_in[0]
    q_a, k_a, v_a, g_a = w[:, 0:512], w[:, 512:1024], w[:, 1024:1536], w[:, 1536:2048]
    q_b, k_b, v_b, g_b = w[:, 2048:2560], w[:, 2560:2688], w[:, 2688:2816], w[:, 2816:3328]
    partner2 = np.concatenate([partner, partner + HEAD_DIM_B])
    w_tok = jnp.concatenate([k_a, k_b, k_b[:, partner2]], axis=1).astype(bf)
    w_feat_t = jnp.concatenate([q_a, v_a, g_a, q_b, v_b, g_b], axis=1).T.astype(bf)
    knw = jnp.tile(k_norm_w[0], N_KV_B)
    wo = w_out[0].astype(bf)

    mod3 = _adaln_mod(c, w_ada[0], b_ada[0]).reshape(bsz, 3, d)
    ka, kb, qa_t, va_t, ga_t, qb_t, vb_t, gb_t, kn, ka_f8, kb_f8, qa_f8, qb_f8 = _in_proj(
        x, mod3, norm_w[0].reshape(1, d), w_tok, w_feat_t,
        q_norm_w[0].reshape(HEAD_DIM_B, 1), knw.reshape(1, 128), knw[partner2].reshape(1, 128),
        cosq, sinq, cosk, sink, grp)
    kmax_a = (jnp.sqrt(jnp.max(kn[:, :, 0, :2 * N_HEADS_A], axis=1)) * NORM_MARGIN).reshape(-1)
    kmax_b = (math.sqrt(HEAD_DIM_B) * NORM_MARGIN * jnp.max(jnp.abs(k_norm_w[0]))).reshape(1)
    oa = _attn_diff(qa_t, ka, va_t, ga_t, alibi,
                    qa_f8.reshape(bsz, N_HEADS_A, 2, kdim, seq), ka_f8.reshape(bsz, N_HEADS_A, 2, seq, kdim),
                    slopes, kmax_a, subln_w[0].reshape(DIFF_V_DIM, 1),
                    lambda_q1[0].reshape(1, -1), lambda_k1[0].reshape(1, -1),
                    lambda_q2[0].reshape(1, -1), lambda_k2[0].reshape(1, -1), lam_init)
    ob = _attn_gqa(qb_t, kb, vb_t, gb_t, qb_f8, kb_f8, kmax_b)
    return _out_proj(oa, ob, wo[:WIDTH_A], wo[WIDTH_A:], x, mod3, final_norm_w.reshape(1, d))
```

```python
import functools
import math

import numpy as np
import jax
import jax.numpy as jnp
from jax import lax
from jax.experimental import pallas as pl
from jax.experimental.pallas import tpu as pltpu

D_MODEL = 1024
N_HEADS_A = 4
DIFF_HEAD_DIM = 64
DIFF_V_DIM = 128
WIDTH_A = 512
N_KV_B = 2
GQA_GROUP = 4
HEAD_DIM_B = 64
WIDTH_B = 512
GRID_W = 64
ROPE_THETA = 10000.0
ROT_HALF = 32
NORM_EPS = 1e-6
MXU_TILE = 256
LANE_CHUNK_A = MXU_TILE
LANE_CHUNK_B = MXU_TILE
KEY_CHUNK_A = 512
KEY_CHUNK_B = 512
KEY_ALIGN = MXU_TILE
ONES_ROWS = 16
LOG2E = 1.4426950408889634
NEG_BIG = -1e30
GUARD_LOG2 = 100.0
NORM_MARGIN = 1.01
F8 = jnp.float8_e4m3fn
F8_SAFE = 440.0
SPLIT = 4
QK_SCALE = LOG2E / math.sqrt(DIFF_HEAD_DIM)
QK_SCALE_ROOT = math.sqrt(QK_SCALE)

VMEM_LIMIT_BYTES = 56 * 1024 * 1024

PROJ_TILE = 1024
OUT_ROW_CHUNK = 256
TQ_A = 512
TK_A = 512
TQ_B = 512
TK_B = 512

assert DIFF_HEAD_DIM == HEAD_DIM_B and TQ_A % LANE_CHUNK_A == 0 and TQ_B % LANE_CHUNK_B == 0


def _silu(v):
    return v * (1.0 / (1.0 + jnp.exp(-v)))


def _split_f8(v):
    hi = v.astype(F8)
    lo = (v - hi.astype(jnp.float32)).astype(F8)
    return hi, lo


def _mod_kernel(c_ref, w_ref, b_ref, o_ref):
    c = c_ref[...]
    o_ref[...] = jnp.dot(_silu(c), w_ref[...], preferred_element_type=jnp.float32,
                         precision=lax.Precision.HIGHEST) + b_ref[...]


def _adaln_mod(c, w_ada, b_ada):
    bsz, d = c.shape
    n = w_ada.shape[1]
    tn = 1024
    return pl.pallas_call(
        _mod_kernel,
        grid=(n // tn,),
        in_specs=[pl.BlockSpec((bsz, d), lambda j: (0, 0)),
                  pl.BlockSpec((d, tn), lambda j: (0, j)),
                  pl.BlockSpec((1, tn), lambda j: (0, j))],
        out_specs=pl.BlockSpec((bsz, tn), lambda j: (0, j)),
        out_shape=jax.ShapeDtypeStruct((bsz, n), jnp.float32),
        name="adaln_mod",
    )(c, w_ada, b_ada.reshape(1, n))


def _store_keys_f8(k2, out_ref, first):
    lane = lax.broadcasted_iota(jnp.int32, k2.shape, 1)
    swapped = pltpu.roll(k2, HEAD_DIM_B, axis=1)
    for g, dup in enumerate((jnp.where(lane < HEAD_DIM_B, k2, swapped),
                             jnp.where(lane < HEAD_DIM_B, swapped, k2))):
        hi, lo = _split_f8(dup)
        out_ref[first + g, :, 0:128] = hi
        out_ref[first + g, :, 128:256] = lo


def _store_queries_f8(q, out_ref, g):
    hi, lo = _split_f8(q)
    for part, val in enumerate((hi, lo, hi, lo)):
        out_ref[g, part * 64:(part + 1) * 64, :] = val


def _in_proj_kernel(x_ref, mod_ref, nw_ref, wtok_ref, wfeat_ref, qnw_ref, knw_ref, knws_ref,
                    cosq_ref, sinq_ref, cosk_ref, sink_ref, grp_ref,
                    ka_ref, kb_ref, qa_ref, va_ref, ga_ref, qb_ref, vb_ref, gb_ref, kn_ref,
                    kaf_ref, kbf_ref, qaf_ref, qbf_ref):
    x = x_ref[...]
    shift = mod_ref[0:1, :]
    scale = mod_ref[1:2, :]
    ms = jnp.mean(x * x, axis=1, keepdims=True)
    h = (x * lax.rsqrt(ms + NORM_EPS)) * nw_ref[...] * (1.0 + scale) + shift
    hb = h.astype(jnp.bfloat16)

    tok = jnp.dot(hb, wtok_ref[...], preferred_element_type=jnp.float32)
    ka = tok[:, 0:WIDTH_A]
    ka_ref[...] = ka.astype(jnp.bfloat16)
    for pair in range(N_HEADS_A):
        _store_keys_f8(ka[:, pair * 128:(pair + 1) * 128] * QK_SCALE_ROOT, kaf_ref, 2 * pair)
    kn = jnp.dot(ka * ka, grp_ref[...], preferred_element_type=jnp.float32)
    kn_ref[...] = jnp.max(kn, axis=0, keepdims=True)
    kb = tok[:, 512:640]
    kbs = tok[:, 640:768]
    sq = kb * kb
    r0 = lax.rsqrt(jnp.sum(sq[:, 0:64], axis=1, keepdims=True) * (1.0 / HEAD_DIM_B) + NORM_EPS)
    r1 = lax.rsqrt(jnp.sum(sq[:, 64:128], axis=1, keepdims=True) * (1.0 / HEAD_DIM_B) + NORM_EPS)
    lane = lax.broadcasted_iota(jnp.int32, kb.shape, 1)
    r = jnp.where(lane < HEAD_DIM_B, r0, r1)
    kr = r * ((kb * knw_ref[...]) * cosk_ref[...] + (kbs * knws_ref[...]) * sink_ref[...])
    kb_ref[...] = kr.astype(jnp.bfloat16)
    _store_keys_f8(kr * QK_SCALE_ROOT, kbf_ref, 0)

    def feat(lo, hi):
        return lax.dot_general(wfeat_ref[lo:hi, :], hb, (((1,), (1,)), ((), ())),
                               preferred_element_type=jnp.float32)

    qa = feat(0, 512)
    qa_ref[...] = (qa * QK_SCALE).astype(jnp.bfloat16)
    for g in range(2 * N_HEADS_A):
        _store_queries_f8(qa[g * 64:(g + 1) * 64, :] * QK_SCALE_ROOT, qaf_ref, g)
    va = feat(512, 1024).astype(jnp.bfloat16)
    ones = jnp.ones((ONES_ROWS, va.shape[1]), jnp.bfloat16)
    for hd in range(N_HEADS_A):
        va_ref[hd, 0:DIFF_V_DIM, :] = va[hd * DIFF_V_DIM:(hd + 1) * DIFF_V_DIM, :]
        va_ref[hd, DIFF_V_DIM:DIFF_V_DIM + ONES_ROWS, :] = ones
    ga_ref[...] = _silu(feat(1024, 1536)).astype(jnp.bfloat16)
    qb = feat(1536, 2048)
    cq = cosq_ref[...]
    sq_ = sinq_ref[...]
    qnw = qnw_ref[...]
    for hd in range(N_KV_B * GQA_GROUP):
        q = qb[hd * 64:(hd + 1) * 64, :]
        rq = lax.rsqrt(jnp.mean(q * q, axis=0, keepdims=True) + NORM_EPS)
        qn = q * rq * qnw
        partner = jnp.concatenate([qn[16:32], qn[0:16], qn[48:64], qn[32:48]], axis=0)
        rot = qn * cq + partner * sq_
        qb_ref[hd * 64:(hd + 1) * 64, :] = (rot * QK_SCALE).astype(jnp.bfloat16)
        _store_queries_f8(rot * QK_SCALE_ROOT, qbf_ref, hd)
    vb = feat(2048, 2176).astype(jnp.bfloat16)
    for g in range(N_KV_B):
        vb_ref[g, 0:HEAD_DIM_B, :] = vb[g * HEAD_DIM_B:(g + 1) * HEAD_DIM_B, :]
        vb_ref[g, HEAD_DIM_B:HEAD_DIM_B + ONES_ROWS, :] = ones
    gb_ref[...] = _silu(feat(2176, 2688)).astype(jnp.bfloat16)


def _in_proj(x, mod3, norm_w, w_tok, w_feat_t, qnw_col, knw_row, knws_row, cosq, sinq, cosk, sink,
             grp):
    bsz, seq, d = x.shape
    ts = PROJ_TILE
    bf = jnp.bfloat16
    kdim = SPLIT * DIFF_HEAD_DIM
    const = lambda b, i: (0, 0)
    feat_spec = lambda rows: pl.BlockSpec((None, rows, ts), lambda b, i: (b, 0, i))
    return pl.pallas_call(
        _in_proj_kernel,
        grid=(bsz, seq // ts),
        in_specs=[
            pl.BlockSpec((None, ts, d), lambda b, i: (b, i, 0)),
            pl.BlockSpec((None, 3, d), lambda b, i: (b, 0, 0)),
            pl.BlockSpec((1, d), const),
            pl.BlockSpec(w_tok.shape, const),
            pl.BlockSpec(w_feat_t.shape, const),
            pl.BlockSpec((HEAD_DIM_B, 1), const),
            pl.BlockSpec((1, 128), const),
            pl.BlockSpec((1, 128), const),
            pl.BlockSpec((HEAD_DIM_B, ts), lambda b, i: (0, i)),
            pl.BlockSpec((HEAD_DIM_B, ts), lambda b, i: (0, i)),
            pl.BlockSpec((ts, 128), lambda b, i: (i, 0)),
            pl.BlockSpec((ts, 128), lambda b, i: (i, 0)),
            pl.BlockSpec(grp.shape, const),
        ],
        out_specs=[
            pl.BlockSpec((None, ts, WIDTH_A), lambda b, i: (b, i, 0)),
            pl.BlockSpec((None, ts, 128), lambda b, i: (b, i, 0)),
            feat_spec(512),
            pl.BlockSpec((None, N_HEADS_A, DIFF_V_DIM + ONES_ROWS, ts), lambda b, i: (b, 0, 0, i)),
            feat_spec(512), feat_spec(512),
            pl.BlockSpec((None, N_KV_B, HEAD_DIM_B + ONES_ROWS, ts), lambda b, i: (b, 0, 0, i)),
            feat_spec(512),
            pl.BlockSpec((None, None, 1, 128), lambda b, i: (b, i, 0, 0)),
            pl.BlockSpec((None, 2 * N_HEADS_A, ts, kdim), lambda b, i: (b, 0, i, 0)),
            pl.BlockSpec((None, N_KV_B, ts, kdim), lambda b, i: (b, 0, i, 0)),
            pl.BlockSpec((None, 2 * N_HEADS_A, kdim, ts), lambda b, i: (b, 0, 0, i)),
            pl.BlockSpec((None, N_KV_B * GQA_GROUP, kdim, ts), lambda b, i: (b, 0, 0, i)),
        ],
        out_shape=[
            jax.ShapeDtypeStruct((bsz, seq, WIDTH_A), bf),
            jax.ShapeDtypeStruct((bsz, seq, 128), bf),
            jax.ShapeDtypeStruct((bsz, 512, seq), bf),
            jax.ShapeDtypeStruct((bsz, N_HEADS_A, DIFF_V_DIM + ONES_ROWS, seq), bf),
            jax.ShapeDtypeStruct((bsz, 512, seq), bf),
            jax.ShapeDtypeStruct((bsz, 512, seq), bf),
            jax.ShapeDtypeStruct((bsz, N_KV_B, HEAD_DIM_B + ONES_ROWS, seq), bf),
            jax.ShapeDtypeStruct((bsz, 512, seq), bf),
            jax.ShapeDtypeStruct((bsz, seq // ts, 1, 128), jnp.float32),
            jax.ShapeDtypeStruct((bsz, 2 * N_HEADS_A, seq, kdim), F8),
            jax.ShapeDtypeStruct((bsz, N_KV_B, seq, kdim), F8),
            jax.ShapeDtypeStruct((bsz, 2 * N_HEADS_A, kdim, seq), F8),
            jax.ShapeDtypeStruct((bsz, N_KV_B * GQA_GROUP, kdim, seq), F8),
        ],
        compiler_params=pltpu.CompilerParams(
            dimension_semantics=("arbitrary", "arbitrary"), vmem_limit_bytes=VMEM_LIMIT_BYTES),
        name="in_proj",
    )(x, mod3, norm_w, w_tok, w_feat_t, qnw_col, knw_row, knws_row, cosq, sinq, cosk, sink, grp)


def _softmax_over_tiles(n_rest, first_tile, tile_fn, first_logits_fn, logits_fn, off_fn, vt_fn,
                        fast_logits_fn, fast_off_fn, fast_vt_fn,
                        f8_ok, bound_ok_fn, key_chunk, lane_chunk, s_ref, p_refs, m_ref, acc_ref):
    chunks = [slice(c, c + lane_chunk) for c in range(0, m_ref.shape[1], lane_chunk)]
    tk = s_ref.shape[0]
    pieces = [(slice(r, r + key_chunk), cols) for r in range(0, tk, key_chunk) for cols in chunks]
    acc_ref[...] = jnp.zeros(acc_ref.shape, jnp.float32)

    off_first = off_fn(first_tile)

    def first_scores(fast):
        for cols in chunks:
            u = first_logits_fn(cols, fast)
            s_ref[:, cols] = u
            m_ref[:, cols] = jnp.max(u, axis=0, keepdims=True) + off_first[:, cols]

    first_scores(True)

    @pl.when(jnp.logical_not(f8_ok))
    def _():
        first_scores(False)

    fixed_ok = bound_ok_fn(m_ref[...]) & f8_ok

    def first_probs(rows, cols):
        shift = m_ref[:, cols] - off_first[:, cols]
        return jnp.exp2((s_ref[rows, cols] - shift).astype(jnp.bfloat16))

    def probs(t, slot, rows, cols):
        j = tile_fn(t)
        shift = m_ref[:, cols] - fast_off_fn(j)[:, cols]
        p_refs[slot][rows, cols] = jnp.exp2(
            (fast_logits_fn(j, rows, cols) - shift).astype(jnp.bfloat16))

    def tile_values(t):
        if t < 0:
            return {rows.start: vt_fn(first_tile, rows) for rows, _ in pieces}
        return {rows.start: fast_vt_fn(tile_fn(t), rows) for rows, _ in pieces}

    def accumulate(slot, rows, cols, vts):
        acc_ref[:, cols] += jnp.dot(vts[rows.start], p_refs[slot][rows, cols],
                                    preferred_element_type=jnp.float32)

    @pl.when(fixed_ok)
    def _():
        for rows, cols in pieces:
            p_refs[1][rows, cols] = first_probs(rows, cols)
        for t in range(-1, n_rest - 1):
            vts = tile_values(t)
            for rows, cols in pieces:
                probs(t + 1, (t + 1) % 2, rows, cols)
                accumulate(t % 2, rows, cols, vts)
        vts = tile_values(n_rest - 1)
        for rows, cols in pieces:
            accumulate((n_rest - 1) % 2, rows, cols, vts)

    @pl.when(jnp.logical_not(fixed_ok))
    def _():
        vt = vt_fn(first_tile)
        for cols in chunks:
            acc_ref[:, cols] = jnp.dot(vt, first_probs(slice(0, tk), cols),
                                       preferred_element_type=jnp.float32)

        def body(t, carry):
            j = tile_fn(t)
            off = off_fn(j)
            vt = vt_fn(j)
            for cols in chunks:
                u = logits_fn(j, cols)
                m_old = m_ref[:, cols]
                m_new = jnp.maximum(m_old, jnp.max(u, axis=0, keepdims=True) + off[:, cols])
                alpha = jnp.exp2(m_old - m_new)
                p = jnp.exp2((u - (m_new - off[:, cols])).astype(jnp.bfloat16))
                acc_ref[:, cols] = alpha * acc_ref[:, cols] + jnp.dot(
                    vt, p, preferred_element_type=jnp.float32)
                m_ref[:, cols] = m_new
            return carry

        lax.fori_loop(0, n_rest, body, 0)


def _attn_diff_kernel(qt_ref, k_ref, vt_ref, gt_ref, qf_ref, kf_ref, slope_ref, kmax_ref,
                      sublnw_ref, lq1_ref, lk1_ref, lq2_ref, lk2_ref, o_ref,
                      bias_ref, qbd_ref, s_ref, p0_ref, p1_ref, m_ref, acc_ref, *, lam_init):
    tq, tk = TQ_A, TK_A
    b = pl.program_id(0)
    hd = pl.program_id(1)
    qi = pl.program_id(2)
    n_kv = k_ref.shape[0] // tk
    sigma = slope_ref[hd] * LOG2E

    @pl.when(qi == 0)
    def _():
        key = lax.broadcasted_iota(jnp.int32, (tk, tq), 0).astype(jnp.float32)
        query = lax.broadcasted_iota(jnp.int32, (tk, tq), 1).astype(jnp.float32)
        bias_ref[...] = sigma * (query - jnp.abs(query - key))

    zeros = jnp.zeros((DIFF_HEAD_DIM, tq), jnp.bfloat16)
    qbd_ref[0:64, 0:tq] = qt_ref[0:64, :]
    qbd_ref[64:128, 0:tq] = zeros
    qbd_ref[0:64, tq:2 * tq] = zeros
    qbd_ref[64:128, tq:2 * tq] = qt_ref[64:128, :]

    lane = lax.broadcasted_iota(jnp.int32, (1, 2 * tq), 1)
    il = jnp.where(lane >= tq, lane - tq, lane).astype(jnp.float32)

    whole = slice(0, tk)

    def key_rows(j, rows):
        return pl.ds(pl.multiple_of(j * tk + rows.start, KEY_ALIGN), rows.stop - rows.start)

    def scores(j, cols):
        return jnp.dot(k_ref[key_rows(j, whole), :], qbd_ref[:, cols], preferred_element_type=jnp.float32)

    def first_logits(cols, fast):
        queries = slice(cols.start % tq, cols.start % tq + cols.stop - cols.start)
        return (fast_logits(qi, whole, cols) if fast else scores(qi, cols)) + bias_ref[:, queries]

    def logits(j, cols):
        key = lax.broadcasted_iota(jnp.int32, (tk, LANE_CHUNK_A), 0).astype(jnp.float32)
        return scores(j, cols) + jnp.where(j < qi, sigma, -sigma) * key

    def offset(j):
        per_query = jnp.where(j > qi, 2.0 * sigma, 0.0)
        return per_query * il - sigma * (jnp.abs(j - qi) * tk).astype(jnp.float32)

    def values(j, rows=whole):
        return vt_ref[:, key_rows(j, rows)]

    def fast_logits(j, rows, cols):
        mp = cols.start // tq
        queries = slice(cols.start - mp * tq, cols.stop - mp * tq)
        return jnp.dot(kf_ref[mp, key_rows(j, rows), :], qf_ref[mp, :, queries],
                       preferred_element_type=jnp.float32)

    def fast_offset(j):
        return offset(j) + jnp.where(j < qi, sigma * tk, 0.0)

    jl = lax.broadcasted_iota(jnp.int32, (1, tk), 1).astype(jnp.float32)
    factor_left = jnp.exp2(sigma * (jl - tk))
    factor_right = jnp.exp2(-sigma * jl)

    def fast_values(j, rows):
        factor = jnp.where(j < qi, factor_left[:, rows], factor_right[:, rows])
        return (values(j, rows).astype(jnp.float32) * factor).astype(jnp.bfloat16)

    q = qbd_ref[...].astype(jnp.float32)
    q_norm = jnp.sqrt(jnp.sum(q * q, axis=0, keepdims=True))
    k_norm0 = kmax_ref[(b * N_HEADS_A + hd) * 2]
    k_norm1 = kmax_ref[(b * N_HEADS_A + hd) * 2 + 1]
    bound = q_norm * jnp.where(lane < tq, k_norm0, k_norm1) + sigma * il
    f8_ok = ((jnp.max(q_norm) * (QK_SCALE_ROOT / QK_SCALE) < F8_SAFE)
             & (jnp.maximum(k_norm0, k_norm1) * QK_SCALE_ROOT < F8_SAFE))

    _softmax_over_tiles(n_kv - 1, qi, lambda t: jnp.where(t >= qi, t + 1, t),
                        first_logits, logits, offset, values,
                        fast_logits, fast_offset, fast_values,
                        f8_ok, lambda m1: jnp.max(bound - m1) <= GUARD_LOG2,
                        KEY_CHUNK_A, LANE_CHUNK_A, s_ref, (p0_ref, p1_ref), m_ref, acc_ref)

    lam = (jnp.exp(jnp.sum(lq1_ref[...] * lk1_ref[...], axis=1, keepdims=True))
           - jnp.exp(jnp.sum(lq2_ref[...] * lk2_ref[...], axis=1, keepdims=True)) + lam_init)
    inv_l = 1.0 / acc_ref[DIFF_V_DIM:DIFF_V_DIM + 1, :]
    o1 = acc_ref[0:DIFF_V_DIM, 0:tq] * inv_l[:, 0:tq]
    o2 = acc_ref[0:DIFF_V_DIM, tq:2 * tq] * inv_l[:, tq:2 * tq]
    diff = o1 - lam * o2
    ms = jnp.mean(diff * diff, axis=0, keepdims=True)
    y = diff * lax.rsqrt(ms + NORM_EPS) * sublnw_ref[...] * (1.0 - lam_init)
    gated = y * gt_ref[...].astype(jnp.float32)
    o_ref[...] = gated.T.astype(o_ref.dtype)


def _attn_diff(qa_t, ka, va_t, ga_t, qa_f8, ka_f8, slopes, kmax, subln_col, lq1, lk1, lq2, lk2,
               lam_init):
    bsz, seq, _ = ka.shape
    tq = TQ_A
    n = 2 * tq
    kdim = SPLIT * DIFF_HEAD_DIM
    vec = pl.BlockSpec((1, DIFF_HEAD_DIM), lambda b, h, i: (0, 0))
    smem = pl.BlockSpec(memory_space=pltpu.SMEM)
    return pl.pallas_call(
        functools.partial(_attn_diff_kernel, lam_init=lam_init),
        grid=(bsz, N_HEADS_A, seq // tq),
        in_specs=[
            pl.BlockSpec((None, 128, tq), lambda b, h, i: (b, h, i)),
            pl.BlockSpec((None, seq, 128), lambda b, h, i: (b, 0, h)),
            pl.BlockSpec((None, None, DIFF_V_DIM + ONES_ROWS, seq), lambda b, h, i: (b, h, 0, 0)),
            pl.BlockSpec((None, 128, tq), lambda b, h, i: (b, h, i)),
            pl.BlockSpec((None, None, 2, kdim, tq), lambda b, h, i: (b, h, 0, 0, i)),
            pl.BlockSpec((None, None, 2, seq, kdim), lambda b, h, i: (b, h, 0, 0, 0)),
            smem, smem,
            pl.BlockSpec((DIFF_V_DIM, 1), lambda b, h, i: (0, 0)),
            vec, vec, vec, vec,
        ],
        out_specs=pl.BlockSpec((None, tq, 128), lambda b, h, i: (b, i, h)),
        out_shape=jax.ShapeDtypeStruct((bsz, seq, WIDTH_A), jnp.bfloat16),
        scratch_shapes=[
            pltpu.VMEM((TK_A, tq), jnp.float32),
            pltpu.VMEM((128, n), jnp.bfloat16),
            pltpu.VMEM((TK_A, n), jnp.float32),
            pltpu.VMEM((TK_A, n), jnp.bfloat16),
            pltpu.VMEM((TK_A, n), jnp.bfloat16),
            pltpu.VMEM((1, n), jnp.float32),
            pltpu.VMEM((DIFF_V_DIM + ONES_ROWS, n), jnp.float32),
        ],
        compiler_params=pltpu.CompilerParams(
            dimension_semantics=("arbitrary", "arbitrary", "arbitrary"),
            vmem_limit_bytes=VMEM_LIMIT_BYTES),
        name="attn_diff",
    )(qa_t, ka, va_t, ga_t, qa_f8, ka_f8, slopes, kmax, subln_col, lq1, lk1, lq2, lk2)


def _attn_gqa_kernel(qt_ref, k_ref, vt_ref, gt_ref, qf_ref, kf_ref, kmax_ref, o_ref,
                     qp_ref, qpf_ref, s_ref, p0_ref, p1_ref, m_ref, acc_ref):
    tq, tk = TQ_B, TK_B
    g = pl.program_id(1)
    n_kv = k_ref.shape[0] // tk
    n = GQA_GROUP * tq

    row = lax.broadcasted_iota(jnp.int32, (128, tq), 0)
    lo = g * HEAD_DIM_B
    mine = (row >= lo) & (row < lo + HEAD_DIM_B)
    for r in range(GQA_GROUP):
        q = qt_ref[r * 64:(r + 1) * 64, :].astype(jnp.float32)
        q2 = jnp.concatenate([q, q], axis=0)
        qp_ref[:, r * tq:(r + 1) * tq] = jnp.where(mine, q2, 0.0).astype(jnp.bfloat16)
        qpf_ref[:, r * tq:(r + 1) * tq] = qf_ref[r]

    whole = slice(0, tk)

    def key_rows(j, rows):
        return pl.ds(pl.multiple_of(j * tk + rows.start, KEY_ALIGN), rows.stop - rows.start)

    def logits(j, cols):
        return jnp.dot(k_ref[key_rows(j, whole), :], qp_ref[:, cols], preferred_element_type=jnp.float32)

    def fast_logits(j, rows, cols):
        return jnp.dot(kf_ref[key_rows(j, rows), :], qpf_ref[:, cols], preferred_element_type=jnp.float32)

    def first_logits(cols, fast):
        return fast_logits(jnp.int32(0), whole, cols) if fast else logits(jnp.int32(0), cols)

    def values(j, rows=whole):
        return vt_ref[:, key_rows(j, rows)]

    no_offset = jnp.zeros((1, n), jnp.float32)
    q = qp_ref[...].astype(jnp.float32)
    q_norm = jnp.sqrt(jnp.sum(q * q, axis=0, keepdims=True))
    bound = q_norm * kmax_ref[0]
    f8_ok = ((jnp.max(q_norm) * (QK_SCALE_ROOT / QK_SCALE) < F8_SAFE)
             & (kmax_ref[0] * QK_SCALE_ROOT < F8_SAFE))

    _softmax_over_tiles(n_kv - 1, jnp.int32(0), lambda t: jnp.asarray(t, jnp.int32) + 1,
                        first_logits, logits, lambda j: no_offset, values,
                        fast_logits, lambda j: no_offset, values,
                        f8_ok, lambda m1: jnp.max(bound - m1) <= GUARD_LOG2,
                        KEY_CHUNK_B, LANE_CHUNK_B, s_ref, (p0_ref, p1_ref), m_ref, acc_ref)

    o = acc_ref[0:HEAD_DIM_B, :] * (1.0 / acc_ref[HEAD_DIM_B:HEAD_DIM_B + 1, :])
    o = jnp.concatenate([o[:, r * tq:(r + 1) * tq] for r in range(GQA_GROUP)], axis=0)
    gated = o * gt_ref[...].astype(jnp.float32)
    o_ref[...] = gated.T.astype(o_ref.dtype)


def _attn_gqa(qb_t, kb, vb_t, gb_t, qb_f8, kb_f8, kmax):
    bsz, seq, _ = kb.shape
    tq = TQ_B
    n = GQA_GROUP * tq
    kdim = SPLIT * HEAD_DIM_B
    return pl.pallas_call(
        _attn_gqa_kernel,
        grid=(bsz, N_KV_B, seq // tq),
        in_specs=[
            pl.BlockSpec((None, 256, tq), lambda b, g, i: (b, g, i)),
            pl.BlockSpec((None, seq, 128), lambda b, g, i: (b, 0, 0)),
            pl.BlockSpec((None, None, HEAD_DIM_B + ONES_ROWS, seq), lambda b, g, i: (b, g, 0, 0)),
            pl.BlockSpec((None, 256, tq), lambda b, g, i: (b, g, i)),
            pl.BlockSpec((None, GQA_GROUP, kdim, tq), lambda b, g, i: (b, g, 0, i)),
            pl.BlockSpec((None, None, seq, kdim), lambda b, g, i: (b, g, 0, 0)),
            pl.BlockSpec(memory_space=pltpu.SMEM),
        ],
        out_specs=pl.BlockSpec((None, tq, 256), lambda b, g, i: (b, i, g)),
        out_shape=jax.ShapeDtypeStruct((bsz, seq, WIDTH_B), jnp.bfloat16),
        scratch_shapes=[
            pltpu.VMEM((128, n), jnp.bfloat16),
            pltpu.VMEM((kdim, n), F8),
            pltpu.VMEM((TK_B, n), jnp.float32),
            pltpu.VMEM((TK_B, n), jnp.bfloat16),
            pltpu.VMEM((TK_B, n), jnp.bfloat16),
            pltpu.VMEM((1, n), jnp.float32),
            pltpu.VMEM((HEAD_DIM_B + ONES_ROWS, n), jnp.float32),
        ],
        compiler_params=pltpu.CompilerParams(
            dimension_semantics=("arbitrary", "arbitrary", "arbitrary"),
            vmem_limit_bytes=VMEM_LIMIT_BYTES),
        name="attn_gqa",
    )(qb_t, kb, vb_t, gb_t, qb_f8, kb_f8, kmax)


def _out_proj_kernel(ga_ref, gb_ref, wa_ref, wb_ref, x_ref, mod_ref, fw_ref, o_ref):
    for r in range(0, x_ref.shape[0], OUT_ROW_CHUNK):
        rows = slice(r, r + OUT_ROW_CHUNK)
        y = jnp.dot(ga_ref[rows, :], wa_ref[...], preferred_element_type=jnp.float32)
        y = y + jnp.dot(gb_ref[rows, :], wb_ref[...], preferred_element_type=jnp.float32)
        z = x_ref[rows, :] + mod_ref[2:3, :] * y
        ms = jnp.mean(z * z, axis=1, keepdims=True)
        o_ref[rows, :] = z * lax.rsqrt(ms + NORM_EPS) * fw_ref[...]


def _out_proj(ga, gb, w_a, w_b, x, mod3, final_w):
    bsz, seq, d = x.shape
    ts = PROJ_TILE
    const = lambda b, i: (0, 0)
    return pl.pallas_call(
        _out_proj_kernel,
        grid=(bsz, seq // ts),
        in_specs=[
            pl.BlockSpec((None, ts, WIDTH_A), lambda b, i: (b, i, 0)),
            pl.BlockSpec((None, ts, WIDTH_B), lambda b, i: (b, i, 0)),
            pl.BlockSpec(w_a.shape, const),
            pl.BlockSpec(w_b.shape, const),
            pl.BlockSpec((None, ts, d), lambda b, i: (b, i, 0)),
            pl.BlockSpec((None, 3, d), lambda b, i: (b, 0, 0)),
            pl.BlockSpec((1, d), const),
        ],
        out_specs=pl.BlockSpec((None, ts, d), lambda b, i: (b, i, 0)),
        out_shape=jax.ShapeDtypeStruct((bsz, seq, d), jnp.float32),
        compiler_params=pltpu.CompilerParams(
            dimension_semantics=("arbitrary", "arbitrary"), vmem_limit_bytes=VMEM_LIMIT_BYTES),
        name="out_proj",
    )(ga, gb, w_a, w_b, x, mod3, final_w)


def _rope_tables(seq):
    pos = np.arange(seq)
    row = (pos // GRID_W).astype(np.float32)
    col = (pos % GRID_W).astype(np.float32)
    n_freq = ROT_HALF // 2
    freqs = (1.0 / (ROPE_THETA ** (np.arange(n_freq, dtype=np.float32) * 2.0 / ROT_HALF))).astype(np.float32)
    row, col, freqs = jnp.asarray(row), jnp.asarray(col), jnp.asarray(freqs)
    ang_r = row[:, None] * freqs[None, :]
    ang_c = col[:, None] * freqs[None, :]
    cos = jnp.concatenate([jnp.cos(ang_r), jnp.cos(ang_r), jnp.cos(ang_c), jnp.cos(ang_c)], axis=1)
    sin = jnp.concatenate([-jnp.sin(ang_r), jnp.sin(ang_r), -jnp.sin(ang_c), jnp.sin(ang_c)], axis=1)
    d = np.arange(HEAD_DIM_B)
    partner = np.where((d // n_freq) % 2 == 0, d + n_freq, d - n_freq)
    return cos, sin, partner


def kernel(x, c, w_ada, b_ada, norm_w, w_in, lambda_q1, lambda_k1, lambda_q2, lambda_k2,
           subln_w, q_norm_w, k_norm_w, w_out, final_norm_w):
    assert w_ada.shape[0] == 1, "single-layer problem: the final norm is fused into the output projection"
    bsz, seq, d = x.shape
    assert seq % TQ_A == 0 and seq % TK_B == 0 and seq % PROJ_TILE == 0
    assert (seq // TK_A) % 2 == 0 and (seq // TK_B) % 2 == 0
    bf = jnp.bfloat16
    kdim = SPLIT * DIFF_HEAD_DIM
    cos, sin, partner = _rope_tables(seq)
    cosq, sinq = cos.T, sin.T
    cosk, sink = jnp.tile(cos, (1, N_KV_B)), jnp.tile(sin, (1, N_KV_B))
    slopes = jnp.asarray(2.0 ** (-8.0 * np.arange(1, N_HEADS_A + 1) / N_HEADS_A), jnp.float32)
    lam_init = 0.8 - 0.6 * math.exp(-0.3 * 0)
    grp = jnp.asarray(np.arange(WIDTH_A)[:, None] // DIFF_HEAD_DIM == np.arange(128)[None, :], jnp.float32)

    w = w_in[0]
    q_a, k_a, v_a, g_a = w[:, 0:512], w[:, 512:1024], w[:, 1024:1536], w[:, 1536:2048]
    q_b, k_b, v_b, g_b = w[:, 2048:2560], w[:, 2560:2688], w[:, 2688:2816], w[:, 2816:3328]
    partner2 = np.concatenate([partner, partner + HEAD_DIM_B])
    w_tok = jnp.concatenate([k_a, k_b, k_b[:, partner2]], axis=1).astype(bf)
    w_feat_t = jnp.concatenate([q_a, v_a, g_a, q_b, v_b, g_b], axis=1).T.astype(bf)
    knw = jnp.tile(k_norm_w[0], N_KV_B)
    wo = w_out[0].astype(bf)

    mod3 = _adaln_mod(c, w_ada[0], b_ada[0]).reshape(bsz, 3, d)
    ka, kb, qa_t, va_t, ga_t, qb_t, vb_t, gb_t, kn, ka_f8, kb_f8, qa_f8, qb_f8 = _in_proj(
        x, mod3, norm_w[0].reshape(1, d), w_tok, w_feat_t,
        q_norm_w[0].reshape(HEAD_DIM_B, 1), knw.reshape(1, 128), knw[partner2].reshape(1, 128),
        cosq, sinq, cosk, sink, grp)
    kmax_a = (jnp.sqrt(jnp.max(kn[:, :, 0, :2 * N_HEADS_A], axis=1)) * NORM_MARGIN).reshape(-1)
    kmax_b = (math.sqrt(HEAD_DIM_B) * NORM_MARGIN * jnp.max(jnp.abs(k_norm_w[0]))).reshape(1)
    oa = _attn_diff(qa_t, ka, va_t, ga_t,
                    qa_f8.reshape(bsz, N_HEADS_A, 2, kdim, seq), ka_f8.reshape(bsz, N_HEADS_A, 2, seq, kdim),
                    slopes, kmax_a, subln_w[0].reshape(DIFF_V_DIM, 1),
                    lambda_q1[0].reshape(1, -1), lambda_k1[0].reshape(1, -1),
                    lambda_q2[0].reshape(1, -1), lambda_k2[0].reshape(1, -1), lam_init)
    ob = _attn_gqa(qb_t, kb, vb_t, gb_t, qb_f8, kb_f8, kmax_b)
    return _out_proj(oa, ob, wo[:WIDTH_A], wo[WIDTH_A:], x, mod3, final_norm_w.reshape(1, d))
```

```python
import functools
import math

import numpy as np
import jax
import jax.numpy as jnp
from jax import lax
from jax.experimental import pallas as pl
from jax.experimental.pallas import tpu as pltpu

D_MODEL = 1024
N_HEADS_A = 4
DIFF_HEAD_DIM = 64
DIFF_V_DIM = 128
WIDTH_A = 512
N_KV_B = 2
GQA_GROUP = 4
HEAD_DIM_B = 64
WIDTH_B = 512
GRID_W = 64
ROPE_THETA = 10000.0
ROT_HALF = 32
NORM_EPS = 1e-6
MXU_TILE = 256
LANE_CHUNK_A = MXU_TILE
LANE_CHUNK_B = MXU_TILE
KEY_CHUNK_A = 512
KEY_CHUNK_B = 512
KEY_ALIGN = MXU_TILE
ONES_ROWS = 16
LOG2E = 1.4426950408889634
NEG_BIG = -1e30
GUARD_LOG2 = 100.0
NORM_MARGIN = 1.01
F8 = jnp.float8_e4m3fn
F8_SAFE = 440.0
SPLIT = 4
QK_SCALE = LOG2E / math.sqrt(DIFF_HEAD_DIM)
QK_SCALE_ROOT = math.sqrt(QK_SCALE)

VMEM_LIMIT_BYTES = 56 * 1024 * 1024

PROJ_TILE = 1024
OUT_ROW_CHUNK = 256
TQ_A = 512
TK_A = 512
TQ_B = 512
TK_B = 512

assert DIFF_HEAD_DIM == HEAD_DIM_B and TQ_A % LANE_CHUNK_A == 0 and TQ_B % LANE_CHUNK_B == 0


def _silu(v):
    return v * (1.0 / (1.0 + jnp.exp(-v)))


def _split_f8(v):
    hi = v.astype(F8)
    lo = (v - hi.astype(jnp.float32)).astype(F8)
    return hi, lo


def _mod_kernel(c_ref, w_ref, b_ref, o_ref):
    c = c_ref[...]
    o_ref[...] = jnp.dot(_silu(c), w_ref[...], preferred_element_type=jnp.float32,
                         precision=lax.Precision.HIGHEST) + b_ref[...]


def _adaln_mod(c, w_ada, b_ada):
    bsz, d = c.shape
    n = w_ada.shape[1]
    tn = 1024
    return pl.pallas_call(
        _mod_kernel,
        grid=(n // tn,),
        in_specs=[pl.BlockSpec((bsz, d), lambda j: (0, 0)),
                  pl.BlockSpec((d, tn), lambda j: (0, j)),
                  pl.BlockSpec((1, tn), lambda j: (0, j))],
        out_specs=pl.BlockSpec((bsz, tn), lambda j: (0, j)),
        out_shape=jax.ShapeDtypeStruct((bsz, n), jnp.float32),
        name="adaln_mod",
    )(c, w_ada, b_ada.reshape(1, n))


def _store_keys_f8(k2, out_ref, first):
    lane = lax.broadcasted_iota(jnp.int32, k2.shape, 1)
    swapped = pltpu.roll(k2, HEAD_DIM_B, axis=1)
    for g, dup in enumerate((jnp.where(lane < HEAD_DIM_B, k2, swapped),
                             jnp.where(lane < HEAD_DIM_B, swapped, k2))):
        hi, lo = _split_f8(dup)
        out_ref[first + g, :, 0:128] = hi
        out_ref[first + g, :, 128:256] = lo


def _store_queries_f8(q, out_ref, g):
    hi, lo = _split_f8(q)
    for part, val in enumerate((hi, lo, hi, lo)):
        out_ref[g, part * 64:(part + 1) * 64, :] = val


def _in_proj_kernel(x_ref, mod_ref, nw_ref, wtok_ref, wfeat_ref, qnw_ref, knw_ref, knws_ref,
                    cosq_ref, sinq_ref, cosk_ref, sink_ref, grp_ref,
                    ka_ref, kb_ref, qa_ref, va_ref, ga_ref, qb_ref, vb_ref, gb_ref, kn_ref,
                    kaf_ref, kbf_ref, qaf_ref, qbf_ref):
    x = x_ref[...]
    shift = mod_ref[0:1, :]
    scale = mod_ref[1:2, :]
    ms = jnp.mean(x * x, axis=1, keepdims=True)
    h = (x * lax.rsqrt(ms + NORM_EPS)) * nw_ref[...] * (1.0 + scale) + shift
    hb = h.astype(jnp.bfloat16)

    tok = jnp.dot(hb, wtok_ref[...], preferred_element_type=jnp.float32)
    ka = tok[:, 0:WIDTH_A]
    ka_ref[...] = ka.astype(jnp.bfloat16)
    for pair in range(N_HEADS_A):
        _store_keys_f8(ka[:, pair * 128:(pair + 1) * 128] * QK_SCALE_ROOT, kaf_ref, 2 * pair)
    kn = jnp.dot(ka * ka, grp_ref[...], preferred_element_type=jnp.float32)
    kn_ref[...] = jnp.max(kn, axis=0, keepdims=True)
    kb = tok[:, 512:640]
    kbs = tok[:, 640:768]
    sq = kb * kb
    r0 = lax.rsqrt(jnp.sum(sq[:, 0:64], axis=1, keepdims=True) * (1.0 / HEAD_DIM_B) + NORM_EPS)
    r1 = lax.rsqrt(jnp.sum(sq[:, 64:128], axis=1, keepdims=True) * (1.0 / HEAD_DIM_B) + NORM_EPS)
    lane = lax.broadcasted_iota(jnp.int32, kb.shape, 1)
    r = jnp.where(lane < HEAD_DIM_B, r0, r1)
    kr = r * ((kb * knw_ref[...]) * cosk_ref[...] + (kbs * knws_ref[...]) * sink_ref[...])
    kb_ref[...] = kr.astype(jnp.bfloat16)
    _store_keys_f8(kr * QK_SCALE_ROOT, kbf_ref, 0)

    def feat(lo, hi):
        return lax.dot_general(wfeat_ref[lo:hi, :], hb, (((1,), (1,)), ((), ())),
                               preferred_element_type=jnp.float32)

    qa = feat(0, 512)
    qa_ref[...] = (qa * QK_SCALE).astype(jnp.bfloat16)
    for g in range(2 * N_HEADS_A):
        _store_queries_f8(qa[g * 64:(g + 1) * 64, :] * QK_SCALE_ROOT, qaf_ref, g)
    va = feat(512, 1024).astype(jnp.bfloat16)
    ones = jnp.ones((ONES_ROWS, va.shape[1]), jnp.bfloat16)
    for hd in range(N_HEADS_A):
        va_ref[hd, 0:DIFF_V_DIM, :] = va[hd * DIFF_V_DIM:(hd + 1) * DIFF_V_DIM, :]
        va_ref[hd, DIFF_V_DIM:DIFF_V_DIM + ONES_ROWS, :] = ones
    ga_ref[...] = _silu(feat(1024, 1536)).astype(jnp.bfloat16)
    qb = feat(1536, 2048)
    cq = cosq_ref[...]
    sq_ = sinq_ref[...]
    qnw = qnw_ref[...]
    for hd in range(N_KV_B * GQA_GROUP):
        q = qb[hd * 64:(hd + 1) * 64, :]
        rq = lax.rsqrt(jnp.mean(q * q, axis=0, keepdims=True) + NORM_EPS)
        qn = q * rq * qnw
        partner = jnp.concatenate([qn[16:32], qn[0:16], qn[48:64], qn[32:48]], axis=0)
        rot = qn * cq + partner * sq_
        qb_ref[hd * 64:(hd + 1) * 64, :] = (rot * QK_SCALE).astype(jnp.bfloat16)
        _store_queries_f8(rot * QK_SCALE_ROOT, qbf_ref, hd)
    vb = feat(2048, 2176).astype(jnp.bfloat16)
    for g in range(N_KV_B):
        vb_ref[g, 0:HEAD_DIM_B, :] = vb[g * HEAD_DIM_B:(g + 1) * HEAD_DIM_B, :]
        vb_ref[g, HEAD_DIM_B:HEAD_DIM_B + ONES_ROWS, :] = ones
    gb_ref[...] = _silu(feat(2176, 2688)).astype(jnp.bfloat16)


def _in_proj(x, mod3, norm_w, w_tok, w_feat_t, qnw_col, knw_row, knws_row, cosq, sinq, cosk, sink,
             grp):
    bsz, seq, d = x.shape
    ts = PROJ_TILE
    bf = jnp.bfloat16
    kdim = SPLIT * DIFF_HEAD_DIM
    const = lambda b, i: (0, 0)
    feat_spec = lambda rows: pl.BlockSpec((None, rows, ts), lambda b, i: (b, 0, i))
    return pl.pallas_call(
        _in_proj_kernel,
        grid=(bsz, seq // ts),
        in_specs=[
            pl.BlockSpec((None, ts, d), lambda b, i: (b, i, 0)),
            pl.BlockSpec((None, 3, d), lambda b, i: (b, 0, 0)),
            pl.BlockSpec((1, d), const),
            pl.BlockSpec(w_tok.shape, const),
            pl.BlockSpec(w_feat_t.shape, const),
            pl.BlockSpec((HEAD_DIM_B, 1), const),
            pl.BlockSpec((1, 128), const),
            pl.BlockSpec((1, 128), const),
            pl.BlockSpec((HEAD_DIM_B, ts), lambda b, i: (0, i)),
            pl.BlockSpec((HEAD_DIM_B, ts), lambda b, i: (0, i)),
            pl.BlockSpec((ts, 128), lambda b, i: (i, 0)),
            pl.BlockSpec((ts, 128), lambda b, i: (i, 0)),
            pl.BlockSpec(grp.shape, const),
        ],
        out_specs=[
            pl.BlockSpec((None, ts, WIDTH_A), lambda b, i: (b, i, 0)),
            pl.BlockSpec((None, ts, 128), lambda b, i: (b, i, 0)),
            feat_spec(512),
            pl.BlockSpec((None, N_HEADS_A, DIFF_V_DIM + ONES_ROWS, ts), lambda b, i: (b, 0, 0, i)),
            feat_spec(512), feat_spec(512),
            pl.BlockSpec((None, N_KV_B, HEAD_DIM_B + ONES_ROWS, ts), lambda b, i: (b, 0, 0, i)),
            feat_spec(512),
            pl.BlockSpec((None, None, 1, 128), lambda b, i: (b, i, 0, 0)),
            pl.BlockSpec((None, 2 * N_HEADS_A, ts, kdim), lambda b, i: (b, 0, i, 0)),
            pl.BlockSpec((None, N_KV_B, ts, kdim), lambda b, i: (b, 0, i, 0)),
            pl.BlockSpec((None, 2 * N_HEADS_A, kdim, ts), lambda b, i: (b, 0, 0, i)),
            pl.BlockSpec((None, N_KV_B * GQA_GROUP, kdim, ts), lambda b, i: (b, 0, 0, i)),
        ],
        out_shape=[
            jax.ShapeDtypeStruct((bsz, seq, WIDTH_A), bf),
            jax.ShapeDtypeStruct((bsz, seq, 128), bf),
            jax.ShapeDtypeStruct((bsz, 512, seq), bf),
            jax.ShapeDtypeStruct((bsz, N_HEADS_A, DIFF_V_DIM + ONES_ROWS, seq), bf),
            jax.ShapeDtypeStruct((bsz, 512, seq), bf),
            jax.ShapeDtypeStruct((bsz, 512, seq), bf),
            jax.ShapeDtypeStruct((bsz, N_KV_B, HEAD_DIM_B + ONES_ROWS, seq), bf),
            jax.ShapeDtypeStruct((bsz, 512, seq), bf),
            jax.ShapeDtypeStruct((bsz, seq // ts, 1, 128), jnp.float32),
            jax.ShapeDtypeStruct((bsz, 2 * N_HEADS_A, seq, kdim), F8),
            jax.ShapeDtypeStruct((bsz, N_KV_B, seq, kdim), F8),
            jax.ShapeDtypeStruct((bsz, 2 * N_HEADS_A, kdim, seq), F8),
            jax.ShapeDtypeStruct((bsz, N_KV_B * GQA_GROUP, kdim, seq), F8),
        ],
        compiler_params=pltpu.CompilerParams(
            dimension_semantics=("arbitrary", "arbitrary"), vmem_limit_bytes=VMEM_LIMIT_BYTES),
        name="in_proj",
    )(x, mod3, norm_w, w_tok, w_feat_t, qnw_col, knw_row, knws_row, cosq, sinq, cosk, sink, grp)


def _softmax_over_tiles(n_rest, first_tile, tile_fn, first_logits_fn, logits_fn, off_fn, vt_fn,
                        fast_logits_fn, fast_off_fn, fast_vt_fn,
                        f8_ok, bound_ok_fn, key_chunk, lane_chunk, s_ref, p_refs, m_ref, acc_ref):
    chunks = [slice(c, c + lane_chunk) for c in range(0, m_ref.shape[1], lane_chunk)]
    tk = s_ref.shape[0]
    pieces = [(slice(r, r + key_chunk), cols) for r in range(0, tk, key_chunk) for cols in chunks]
    acc_ref[...] = jnp.zeros(acc_ref.shape, jnp.float32)

    off_first = off_fn(first_tile)

    def first_scores(fast):
        for cols in chunks:
            u = first_logits_fn(cols, fast)
            s_ref[:, cols] = u
            m_ref[:, cols] = jnp.max(u, axis=0, keepdims=True) + off_first[:, cols]

    first_scores(True)

    @pl.when(jnp.logical_not(f8_ok))
    def _():
        first_scores(False)

    fixed_ok = bound_ok_fn(m_ref[...]) & f8_ok

    def first_probs(rows, cols):
        shift = m_ref[:, cols] - off_first[:, cols]
        return jnp.exp2((s_ref[rows, cols] - shift).astype(jnp.bfloat16))

    def probs(t, slot, rows, cols):
        j = tile_fn(t)
        shift = m_ref[:, cols] - fast_off_fn(j)[:, cols]
        p_refs[slot][rows, cols] = jnp.exp2(
            (fast_logits_fn(j, rows, cols) - shift).astype(jnp.bfloat16))

    def tile_values(t):
        if t < 0:
            return {rows.start: vt_fn(first_tile, rows) for rows, _ in pieces}
        return {rows.start: fast_vt_fn(tile_fn(t), rows) for rows, _ in pieces}

    def accumulate(slot, rows, cols, vts):
        acc_ref[:, cols] += jnp.dot(vts[rows.start], p_refs[slot][rows, cols],
                                    preferred_element_type=jnp.float32)

    @pl.when(fixed_ok)
    def _():
        for rows, cols in pieces:
            p_refs[1][rows, cols] = first_probs(rows, cols)
        for t in range(-1, n_rest - 1):
            vts = tile_values(t)
            for rows, cols in pieces:
                probs(t + 1, (t + 1) % 2, rows, cols)
                accumulate(t % 2, rows, cols, vts)
        vts = tile_values(n_rest - 1)
        for rows, cols in pieces:
            accumulate((n_rest - 1) % 2, rows, cols, vts)

    @pl.when(jnp.logical_not(fixed_ok))
    def _():
        vt = vt_fn(first_tile)
        for cols in chunks:
            acc_ref[:, cols] = jnp.dot(vt, first_probs(slice(0, tk), cols),
                                       preferred_element_type=jnp.float32)

        def body(t, carry):
            j = tile_fn(t)
            off = off_fn(j)
            vt = vt_fn(j)
            for cols in chunks:
                u = logits_fn(j, cols)
                m_old = m_ref[:, cols]
                m_new = jnp.maximum(m_old, jnp.max(u, axis=0, keepdims=True) + off[:, cols])
                alpha = jnp.exp2(m_old - m_new)
                p = jnp.exp2((u - (m_new - off[:, cols])).astype(jnp.bfloat16))
                acc_ref[:, cols] = alpha * acc_ref[:, cols] + jnp.dot(
                    vt, p, preferred_element_type=jnp.float32)
                m_ref[:, cols] = m_new
            return carry

        lax.fori_loop(0, n_rest, body, 0)


def _attn_diff_kernel(qt_ref, k_ref, vt_ref, gt_ref, qf_ref, kf_ref, slope_ref, kmax_ref,
                      sublnw_ref, lq1_ref, lk1_ref, lq2_ref, lk2_ref, o_ref,
                      bias_ref, qbd_ref, s_ref, p0_ref, p1_ref, m_ref, acc_ref, *, lam_init):
    tq, tk = TQ_A, TK_A
    b = pl.program_id(0)
    hd = pl.program_id(1)
    qi = pl.program_id(2)
    n_kv = k_ref.shape[0] // tk
    sigma = slope_ref[hd] * LOG2E

    @pl.when(qi == 0)
    def _():
        key = lax.broadcasted_iota(jnp.int32, (tk, tq), 0).astype(jnp.float32)
        query = lax.broadcasted_iota(jnp.int32, (tk, tq), 1).astype(jnp.float32)
        bias_ref[...] = sigma * (query - jnp.abs(query - key))

    zeros = jnp.zeros((DIFF_HEAD_DIM, tq), jnp.bfloat16)
    qbd_ref[0:64, 0:tq] = qt_ref[0:64, :]
    qbd_ref[64:128, 0:tq] = zeros
    qbd_ref[0:64, tq:2 * tq] = zeros
    qbd_ref[64:128, tq:2 * tq] = qt_ref[64:128, :]

    lane = lax.broadcasted_iota(jnp.int32, (1, 2 * tq), 1)
    il = jnp.where(lane >= tq, lane - tq, lane).astype(jnp.float32)

    whole = slice(0, tk)

    def key_rows(j, rows):
        return pl.ds(pl.multiple_of(j * tk + rows.start, KEY_ALIGN), rows.stop - rows.start)

    def scores(j, cols):
        return jnp.dot(k_ref[key_rows(j, whole), :], qbd_ref[:, cols], preferred_element_type=jnp.float32)

    def first_logits(cols, fast):
        queries = slice(cols.start % tq, cols.start % tq + cols.stop - cols.start)
        return (fast_logits(qi, whole, cols) if fast else scores(qi, cols)) + bias_ref[:, queries]

    def logits(j, cols):
        key = lax.broadcasted_iota(jnp.int32, (tk, LANE_CHUNK_A), 0).astype(jnp.float32)
        return scores(j, cols) + jnp.where(j < qi, sigma, -sigma) * key

    def offset(j):
        per_query = jnp.where(j > qi, 2.0 * sigma, 0.0)
        return per_query * il - sigma * (jnp.abs(j - qi) * tk).astype(jnp.float32)

    def values(j, rows=whole):
        return vt_ref[:, key_rows(j, rows)]

    def fast_logits(j, rows, cols):
        mp = cols.start // tq
        queries = slice(cols.start - mp * tq, cols.stop - mp * tq)
        return jnp.dot(kf_ref[mp, key_rows(j, rows), :], qf_ref[mp, :, queries],
                       preferred_element_type=jnp.float32)

    def fast_offset(j):
        return offset(j) + jnp.where(j < qi, sigma * tk, 0.0)

    jl = lax.broadcasted_iota(jnp.int32, (1, tk), 1).astype(jnp.float32)
    factor_left = jnp.exp2(sigma * (jl - tk))
    factor_right = jnp.exp2(-sigma * jl)

    def fast_values(j, rows):
        factor = jnp.where(j < qi, factor_left[:, rows], factor_right[:, rows])
        return (values(j, rows).astype(jnp.float32) * factor).astype(jnp.bfloat16)

    q = qbd_ref[...].astype(jnp.float32)
    q_norm = jnp.sqrt(jnp.sum(q * q, axis=0, keepdims=True))
    k_norm0 = kmax_ref[(b * N_HEADS_A + hd) * 2]
    k_norm1 = kmax_ref[(b * N_HEADS_A + hd) * 2 + 1]
    bound = q_norm * jnp.where(lane < tq, k_norm0, k_norm1) + sigma * il
    f8_ok = ((jnp.max(q_norm) * (QK_SCALE_ROOT / QK_SCALE) < F8_SAFE)
             & (jnp.maximum(k_norm0, k_norm1) * QK_SCALE_ROOT < F8_SAFE))

    _softmax_over_tiles(n_kv - 1, qi, lambda t: jnp.where(t >= qi, t + 1, t),
                        first_logits, logits, offset, values,
                        fast_logits, fast_offset, fast_values,
                        f8_ok, lambda m1: jnp.max(bound - m1) <= GUARD_LOG2,
                        KEY_CHUNK_A, LANE_CHUNK_A, s_ref, (p0_ref, p1_ref), m_ref, acc_ref)

    lam = (jnp.exp(jnp.sum(lq1_ref[...] * lk1_ref[...], axis=1, keepdims=True))
           - jnp.exp(jnp.sum(lq2_ref[...] * lk2_ref[...], axis=1, keepdims=True)) + lam_init)
    inv_l = 1.0 / acc_ref[DIFF_V_DIM:DIFF_V_DIM + 1, :]
    o1 = acc_ref[0:DIFF_V_DIM, 0:tq] * inv_l[:, 0:tq]
    o2 = acc_ref[0:DIFF_V_DIM, tq:2 * tq] * inv_l[:, tq:2 * tq]
    diff = o1 - lam * o2
    ms = jnp.mean(diff * diff, axis=0, keepdims=True)
    y = diff * lax.rsqrt(ms + NORM_EPS) * sublnw_ref[...] * (1.0 - lam_init)
    gated = y * gt_ref[...].astype(jnp.float32)
    o_ref[...] = gated.T.astype(o_ref.dtype)


def _attn_diff(qa_t, ka, va_t, ga_t, qa_f8, ka_f8, slopes, kmax, subln_col, lq1, lk1, lq2, lk2,
               lam_init):
    bsz, seq, _ = ka.shape
    tq = TQ_A
    n = 2 * tq
    kdim = SPLIT * DIFF_HEAD_DIM
    vec = pl.BlockSpec((1, DIFF_HEAD_DIM), lambda b, h, i: (0, 0))
    smem = pl.BlockSpec(memory_space=pltpu.SMEM)
    return pl.pallas_call(
        functools.partial(_attn_diff_kernel, lam_init=lam_init),
        grid=(bsz, N_HEADS_A, seq // tq),
        in_specs=[
            pl.BlockSpec((None, 128, tq), lambda b, h, i: (b, h, i)),
            pl.BlockSpec((None, seq, 128), lambda b, h, i: (b, 0, h)),
            pl.BlockSpec((None, None, DIFF_V_DIM + ONES_ROWS, seq), lambda b, h, i: (b, h, 0, 0)),
            pl.BlockSpec((None, 128, tq), lambda b, h, i: (b, h, i)),
            pl.BlockSpec((None, None, 2, kdim, tq), lambda b, h, i: (b, h, 0, 0, i)),
            pl.BlockSpec((None, None, 2, seq, kdim), lambda b, h, i: (b, h, 0, 0, 0)),
            smem, smem,
            pl.BlockSpec((DIFF_V_DIM, 1), lambda b, h, i: (0, 0)),
            vec, vec, vec, vec,
        ],
        out_specs=pl.BlockSpec((None, tq, 128), lambda b, h, i: (b, i, h)),
        out_shape=jax.ShapeDtypeStruct((bsz, seq, WIDTH_A), jnp.bfloat16),
        scratch_shapes=[
            pltpu.VMEM((TK_A, tq), jnp.float32),
            pltpu.VMEM((128, n), jnp.bfloat16),
            pltpu.VMEM((TK_A, n), jnp.float32),
            pltpu.VMEM((TK_A, n), jnp.bfloat16),
            pltpu.VMEM((TK_A, n), jnp.bfloat16),
            pltpu.VMEM((1, n), jnp.float32),
            pltpu.VMEM((DIFF_V_DIM + ONES_ROWS, n), jnp.float32),
        ],
        compiler_params=pltpu.CompilerParams(
            dimension_semantics=("arbitrary", "arbitrary", "arbitrary"),
            vmem_limit_bytes=VMEM_LIMIT_BYTES),
        name="attn_diff",
    )(qa_t, ka, va_t, ga_t, qa_f8, ka_f8, slopes, kmax, subln_col, lq1, lk1, lq2, lk2)


def _attn_gqa_kernel(qt_ref, k_ref, vt_ref, gt_ref, qf_ref, kf_ref, kmax_ref, o_ref,
                     qp_ref, qpf_ref, s_ref, p0_ref, p1_ref, m_ref, acc_ref):
    tq, tk = TQ_B, TK_B
    g = pl.program_id(1)
    n_kv = k_ref.shape[0] // tk
    n = GQA_GROUP * tq

    row = lax.broadcasted_iota(jnp.int32, (128, tq), 0)
    lo = g * HEAD_DIM_B
    mine = (row >= lo) & (row < lo + HEAD_DIM_B)
    for r in range(GQA_GROUP):
        q = qt_ref[r * 64:(r + 1) * 64, :].astype(jnp.float32)
        q2 = jnp.concatenate([q, q], axis=0)
        qp_ref[:, r * tq:(r + 1) * tq] = jnp.where(mine, q2, 0.0).astype(jnp.bfloat16)
        qpf_ref[:, r * tq:(r + 1) * tq] = qf_ref[r]

    whole = slice(0, tk)

    def key_rows(j, rows):
        return pl.ds(pl.multiple_of(j * tk + rows.start, KEY_ALIGN), rows.stop - rows.start)

    def logits(j, cols):
        return jnp.dot(k_ref[key_rows(j, whole), :], qp_ref[:, cols], preferred_element_type=jnp.float32)

    def fast_logits(j, rows, cols):
        return jnp.dot(kf_ref[key_rows(j, rows), :], qpf_ref[:, cols], preferred_element_type=jnp.float32)

    def first_logits(cols, fast):
        return fast_logits(jnp.int32(0), whole, cols) if fast else logits(jnp.int32(0), cols)

    def values(j, rows=whole):
        return vt_ref[:, key_rows(j, rows)]

    no_offset = jnp.zeros((1, n), jnp.float32)
    q = qp_ref[...].astype(jnp.float32)
    q_norm = jnp.sqrt(jnp.sum(q * q, axis=0, keepdims=True))
    bound = q_norm * kmax_ref[0]
    f8_ok = ((jnp.max(q_norm) * (QK_SCALE_ROOT / QK_SCALE) < F8_SAFE)
             & (kmax_ref[0] * QK_SCALE_ROOT < F8_SAFE))

    _softmax_over_tiles(n_kv - 1, jnp.int32(0), lambda t: jnp.asarray(t, jnp.int32) + 1,
                        first_logits, logits, lambda j: no_offset, values,
                        fast_logits, lambda j: no_offset, values,
                        f8_ok, lambda m1: jnp.max(bound - m1) <= GUARD_LOG2,
                        KEY_CHUNK_B, LANE_CHUNK_B, s_ref, (p0_ref, p1_ref), m_ref, acc_ref)

    o = acc_ref[0:HEAD_DIM_B, :] * (1.0 / acc_ref[HEAD_DIM_B:HEAD_DIM_B + 1, :])
    o = jnp.concatenate([o[:, r * tq:(r + 1) * tq] for r in range(GQA_GROUP)], axis=0)
    gated = o * gt_ref[...].astype(jnp.float32)
    o_ref[...] = gated.T.astype(o_ref.dtype)


def _attn_gqa(qb_t, kb, vb_t, gb_t, qb_f8, kb_f8, kmax):
    bsz, seq, _ = kb.shape
    tq = TQ_B
    n = GQA_GROUP * tq
    kdim = SPLIT * HEAD_DIM_B
    return pl.pallas_call(
        _attn_gqa_kernel,
        grid=(bsz, N_KV_B, seq // tq),
        in_specs=[
            pl.BlockSpec((None, 256, tq), lambda b, g, i: (b, g, i)),
            pl.BlockSpec((None, seq, 128), lambda b, g, i: (b, 0, 0)),
            pl.BlockSpec((None, None, HEAD_DIM_B + ONES_ROWS, seq), lambda b, g, i: (b, g, 0, 0)),
            pl.BlockSpec((None, 256, tq), lambda b, g, i: (b, g, i)),
            pl.BlockSpec((None, GQA_GROUP, kdim, tq), lambda b, g, i: (b, g, 0, i)),
            pl.BlockSpec((None, None, seq, kdim), lambda b, g, i: (b, g, 0, 0)),
            pl.BlockSpec(memory_space=pltpu.SMEM),
        ],
        out_specs=pl.BlockSpec((None, tq, 256), lambda b, g, i: (b, i, g)),
        out_shape=jax.ShapeDtypeStruct((bsz, seq, WIDTH_B), jnp.bfloat16),
        scratch_shapes=[
            pltpu.VMEM((128, n), jnp.bfloat16),
            pltpu.VMEM((kdim, n), F8),
            pltpu.VMEM((TK_B, n), jnp.float32),
            pltpu.VMEM((TK_B, n), jnp.bfloat16),
            pltpu.VMEM((TK_B, n), jnp.bfloat16),
            pltpu.VMEM((1, n), jnp.float32),
            pltpu.VMEM((HEAD_DIM_B + ONES_ROWS, n), jnp.float32),
        ],
        compiler_params=pltpu.CompilerParams(
            dimension_semantics=("arbitrary", "arbitrary", "arbitrary"),
            vmem_limit_bytes=VMEM_LIMIT_BYTES),
        name="attn_gqa",
    )(qb_t, kb, vb_t, gb_t, qb_f8, kb_f8, kmax)


def _out_proj_kernel(ga_ref, gb_ref, wa_ref, wb_ref, x_ref, mod_ref, fw_ref, o_ref):
    for r in range(0, x_ref.shape[0], OUT_ROW_CHUNK):
        rows = slice(r, r + OUT_ROW_CHUNK)
        y = jnp.dot(ga_ref[rows, :], wa_ref[...], preferred_element_type=jnp.float32)
        y = y + jnp.dot(gb_ref[rows, :], wb_ref[...], preferred_element_type=jnp.float32)
        z = x_ref[rows, :] + mod_ref[2:3, :] * y
        ms = jnp.mean(z * z, axis=1, keepdims=True)
        o_ref[rows, :] = z * lax.rsqrt(ms + NORM_EPS) * fw_ref[...]


def _out_proj(ga, gb, w_a, w_b, x, mod3, final_w):
    bsz, seq, d = x.shape
    ts = PROJ_TILE
    const = lambda b, i: (0, 0)
    return pl.pallas_call(
        _out_proj_kernel,
        grid=(bsz, seq // ts),
        in_specs=[
            pl.BlockSpec((None, ts, WIDTH_A), lambda b, i: (b, i, 0)),
            pl.BlockSpec((None, ts, WIDTH_B), lambda b, i: (b, i, 0)),
            pl.BlockSpec(w_a.shape, const),
            pl.BlockSpec(w_b.shape, const),
            pl.BlockSpec((None, ts, d), lambda b, i: (b, i, 0)),
            pl.BlockSpec((None, 3, d), lambda b, i: (b, 0, 0)),
            pl.BlockSpec((1, d), const),
        ],
        out_specs=pl.BlockSpec((None, ts, d), lambda b, i: (b, i, 0)),
        out_shape=jax.ShapeDtypeStruct((bsz, seq, d), jnp.float32),
        compiler_params=pltpu.CompilerParams(
            dimension_semantics=("arbitrary", "arbitrary"), vmem_limit_bytes=VMEM_LIMIT_BYTES),
        name="out_proj",
    )(ga, gb, w_a, w_b, x, mod3, final_w)


def _rope_tables(seq):
    pos = np.arange(seq)
    row = (pos // GRID_W).astype(np.float64)
    col = (pos % GRID_W).astype(np.float64)
    n_freq = ROT_HALF // 2
    freqs = 1.0 / (ROPE_THETA ** (np.arange(n_freq, dtype=np.float64) * 2.0 / ROT_HALF))
    ang_r = row[:, None] * freqs[None, :]
    ang_c = col[:, None] * freqs[None, :]
    cos = np.concatenate([np.cos(ang_r), np.cos(ang_r), np.cos(ang_c), np.cos(ang_c)], axis=1)
    sin = np.concatenate([-np.sin(ang_r), np.sin(ang_r), -np.sin(ang_c), np.sin(ang_c)], axis=1)
    d = np.arange(HEAD_DIM_B)
    partner = np.where((d // n_freq) % 2 == 0, d + n_freq, d - n_freq)
    return cos.astype(np.float32), sin.astype(np.float32), partner


def kernel(x, c, w_ada, b_ada, norm_w, w_in, lambda_q1, lambda_k1, lambda_q2, lambda_k2,
           subln_w, q_norm_w, k_norm_w, w_out, final_norm_w):
    assert w_ada.shape[0] == 1, "single-layer problem: the final norm is fused into the output projection"
    bsz, seq, d = x.shape
    assert seq % TQ_A == 0 and seq % TK_B == 0 and seq % PROJ_TILE == 0
    assert (seq // TK_A) % 2 == 0 and (seq // TK_B) % 2 == 0
    bf = jnp.bfloat16
    kdim = SPLIT * DIFF_HEAD_DIM
    cos, sin, partner = _rope_tables(seq)
    cosq, sinq = (jnp.asarray(np.ascontiguousarray(t.T)) for t in (cos, sin))
    cosk, sink = (jnp.asarray(np.tile(t, (1, N_KV_B))) for t in (cos, sin))
    slopes = jnp.asarray(2.0 ** (-8.0 * np.arange(1, N_HEADS_A + 1) / N_HEADS_A), jnp.float32)
    lam_init = 0.8 - 0.6 * math.exp(-0.3 * 0)
    grp = jnp.asarray(np.arange(WIDTH_A)[:, None] // DIFF_HEAD_DIM == np.arange(128)[None, :], jnp.float32)

    w = w_in[0]
    q_a, k_a, v_a, g_a = w[:, 0:512], w[:, 512:1024], w[:, 1024:1536], w[:, 1536:2048]
    q_b, k_b, v_b, g_b = w[:, 2048:2560], w[:, 2560:2688], w[:, 2688:2816], w[:, 2816:3328]
    partner2 = np.concatenate([partner, partner + HEAD_DIM_B])
    w_tok = jnp.concatenate([k_a, k_b, k_b[:, partner2]], axis=1).astype(bf)
    w_feat_t = jnp.concatenate([q_a, v_a, g_a, q_b, v_b, g_b], axis=1).T.astype(bf)
    knw = jnp.tile(k_norm_w[0], N_KV_B)
    wo = w_out[0].astype(bf)

    mod3 = _adaln_mod(c, w_ada[0], b_ada[0]).reshape(bsz, 3, d)
    ka, kb, qa_t, va_t, ga_t, qb_t, vb_t, gb_t, kn, ka_f8, kb_f8, qa_f8, qb_f8 = _in_proj(
        x, mod3, norm_w[0].reshape(1, d), w_tok, w_feat_t,
        q_norm_w[0].reshape(HEAD_DIM_B, 1), knw.reshape(1, 128), knw[partner2].reshape(1, 128),
        cosq, sinq, cosk, sink, grp)
    kmax_a = (jnp.sqrt(jnp.max(kn[:, :, 0, :2 * N_HEADS_A], axis=1)) * NORM_MARGIN).reshape(-1)
    kmax_b = (math.sqrt(HEAD_DIM_B) * NORM_MARGIN * jnp.max(jnp.abs(k_norm_w[0]))).reshape(1)
    oa = _attn_diff(qa_t, ka, va_t, ga_t,
                    qa_f8.reshape(bsz, N_HEADS_A, 2, kdim, seq), ka_f8.reshape(bsz, N_HEADS_A, 2, seq, kdim),
                    slopes, kmax_a, subln_w[0].reshape(DIFF_V_DIM, 1),
                    lambda_q1[0].reshape(1, -1), lambda_k1[0].reshape(1, -1),
                    lambda_q2[0].reshape(1, -1), lambda_k2[0].reshape(1, -1), lam_init)
    ob = _attn_gqa(qb_t, kb, vb_t, gb_t, qb_f8, kb_f8, kmax_b)
    return _out_proj(oa, ob, wo[:WIDTH_A], wo[WIDTH_A:], x, mod3, final_norm_w.reshape(1, d))
```

```python
import functools
import math

import numpy as np
import jax
import jax.numpy as jnp
from jax import lax
from jax.experimental import pallas as pl
from jax.experimental.pallas import tpu as pltpu

D_MODEL = 1024
N_HEADS_A = 4
DIFF_HEAD_DIM = 64
DIFF_V_DIM = 128
WIDTH_A = 512
N_KV_B = 2
GQA_GROUP = 4
HEAD_DIM_B = 64
WIDTH_B = 512
GRID_W = 64
ROPE_THETA = 10000.0
ROT_HALF = 32
NORM_EPS = 1e-6
MXU_TILE = 256
LANE_CHUNK_A = MXU_TILE
LANE_CHUNK_B = MXU_TILE
KEY_CHUNK_A = 512
KEY_CHUNK_B = 512
KEY_ALIGN = MXU_TILE
ONES_ROWS = 16
LOG2E = 1.4426950408889634
NEG_BIG = -1e30
GUARD_LOG2 = 100.0
NORM_MARGIN = 1.01
F8 = jnp.float8_e4m3fn
F8_SAFE = 440.0
SPLIT = 4
QK_SCALE = LOG2E / math.sqrt(DIFF_HEAD_DIM)
QK_SCALE_ROOT = math.sqrt(QK_SCALE)

VMEM_LIMIT_BYTES = 56 * 1024 * 1024

PROJ_TILE = 1024
OUT_ROW_CHUNK = 256
TQ_A = 512
TK_A = 512
TQ_B = 512
TK_B = 512

assert DIFF_HEAD_DIM == HEAD_DIM_B and TQ_A % LANE_CHUNK_A == 0 and TQ_B % LANE_CHUNK_B == 0


def _silu(v):
    return v * (1.0 / (1.0 + jnp.exp(-v)))


def _split_f8(v):
    hi = v.astype(F8)
    lo = (v - hi.astype(jnp.float32)).astype(F8)
    return hi, lo


def _mod_kernel(ct_ref, w_ref, b_ref, o_ref):
    s = _silu(ct_ref[...])
    w = w_ref[...]
    for b in range(ct_ref.shape[1]):
        o_ref[b:b + 1, :] = jnp.sum(w * s[:, b:b + 1], axis=0, keepdims=True) + b_ref[...]


def _adaln_mod(c, w_ada, b_ada):
    bsz, d = c.shape
    n = w_ada.shape[1]
    tn = 1024
    return pl.pallas_call(
        _mod_kernel,
        grid=(n // tn,),
        in_specs=[pl.BlockSpec((d, bsz), lambda j: (0, 0)),
                  pl.BlockSpec((d, tn), lambda j: (0, j)),
                  pl.BlockSpec((1, tn), lambda j: (0, j))],
        out_specs=pl.BlockSpec((bsz, tn), lambda j: (0, j)),
        out_shape=jax.ShapeDtypeStruct((bsz, n), jnp.float32),
        name="adaln_mod",
    )(c.T, w_ada, b_ada.reshape(1, n))


def _store_keys_f8(k2, out_ref, first):
    lane = lax.broadcasted_iota(jnp.int32, k2.shape, 1)
    swapped = pltpu.roll(k2, HEAD_DIM_B, axis=1)
    for g, dup in enumerate((jnp.where(lane < HEAD_DIM_B, k2, swapped),
                             jnp.where(lane < HEAD_DIM_B, swapped, k2))):
        hi, lo = _split_f8(dup)
        out_ref[first + g, :, 0:128] = hi
        out_ref[first + g, :, 128:256] = lo


def _store_queries_f8(q, out_ref, g):
    hi, lo = _split_f8(q)
    for part, val in enumerate((hi, lo, hi, lo)):
        out_ref[g, part * 64:(part + 1) * 64, :] = val


def _in_proj_kernel(x_ref, mod_ref, nw_ref, wtok_ref, wfeat_ref, qnw_ref, knw_ref, knws_ref,
                    cosq_ref, sinq_ref, cosk_ref, sink_ref, grp_ref,
                    ka_ref, kb_ref, qa_ref, va_ref, ga_ref, qb_ref, vb_ref, gb_ref, kn_ref,
                    kaf_ref, kbf_ref, qaf_ref, qbf_ref):
    x = x_ref[...]
    shift = mod_ref[0:1, :]
    scale = mod_ref[1:2, :]
    ms = jnp.mean(x * x, axis=1, keepdims=True)
    h = (x * lax.rsqrt(ms + NORM_EPS)) * nw_ref[...] * (1.0 + scale) + shift
    hb = h.astype(jnp.bfloat16)

    tok = jnp.dot(hb, wtok_ref[...], preferred_element_type=jnp.float32)
    ka = tok[:, 0:WIDTH_A]
    ka_ref[...] = ka.astype(jnp.bfloat16)
    for pair in range(N_HEADS_A):
        _store_keys_f8(ka[:, pair * 128:(pair + 1) * 128] * QK_SCALE_ROOT, kaf_ref, 2 * pair)
    kn = jnp.dot(ka * ka, grp_ref[...], preferred_element_type=jnp.float32)
    kn_ref[...] = jnp.max(kn, axis=0, keepdims=True)
    kb = tok[:, 512:640]
    kbs = tok[:, 640:768]
    sq = kb * kb
    r0 = lax.rsqrt(jnp.sum(sq[:, 0:64], axis=1, keepdims=True) * (1.0 / HEAD_DIM_B) + NORM_EPS)
    r1 = lax.rsqrt(jnp.sum(sq[:, 64:128], axis=1, keepdims=True) * (1.0 / HEAD_DIM_B) + NORM_EPS)
    lane = lax.broadcasted_iota(jnp.int32, kb.shape, 1)
    r = jnp.where(lane < HEAD_DIM_B, r0, r1)
    kr = r * ((kb * knw_ref[...]) * cosk_ref[...] + (kbs * knws_ref[...]) * sink_ref[...])
    kb_ref[...] = kr.astype(jnp.bfloat16)
    _store_keys_f8(kr * QK_SCALE_ROOT, kbf_ref, 0)

    def feat(lo, hi):
        return lax.dot_general(wfeat_ref[lo:hi, :], hb, (((1,), (1,)), ((), ())),
                               preferred_element_type=jnp.float32)

    qa = feat(0, 512)
    qa_ref[...] = (qa * QK_SCALE).astype(jnp.bfloat16)
    for g in range(2 * N_HEADS_A):
        _store_queries_f8(qa[g * 64:(g + 1) * 64, :] * QK_SCALE_ROOT, qaf_ref, g)
    va = feat(512, 1024).astype(jnp.bfloat16)
    ones = jnp.ones((ONES_ROWS, va.shape[1]), jnp.bfloat16)
    for hd in range(N_HEADS_A):
        va_ref[hd, 0:DIFF_V_DIM, :] = va[hd * DIFF_V_DIM:(hd + 1) * DIFF_V_DIM, :]
        va_ref[hd, DIFF_V_DIM:DIFF_V_DIM + ONES_ROWS, :] = ones
    ga_ref[...] = _silu(feat(1024, 1536)).astype(jnp.bfloat16)
    qb = feat(1536, 2048)
    cq = cosq_ref[...]
    sq_ = sinq_ref[...]
    qnw = qnw_ref[...]
    for hd in range(N_KV_B * GQA_GROUP):
        q = qb[hd * 64:(hd + 1) * 64, :]
        rq = lax.rsqrt(jnp.mean(q * q, axis=0, keepdims=True) + NORM_EPS)
        qn = q * rq * qnw
        partner = jnp.concatenate([qn[16:32], qn[0:16], qn[48:64], qn[32:48]], axis=0)
        rot = qn * cq + partner * sq_
        qb_ref[hd * 64:(hd + 1) * 64, :] = (rot * QK_SCALE).astype(jnp.bfloat16)
        _store_queries_f8(rot * QK_SCALE_ROOT, qbf_ref, hd)
    vb = feat(2048, 2176).astype(jnp.bfloat16)
    for g in range(N_KV_B):
        vb_ref[g, 0:HEAD_DIM_B, :] = vb[g * HEAD_DIM_B:(g + 1) * HEAD_DIM_B, :]
        vb_ref[g, HEAD_DIM_B:HEAD_DIM_B + ONES_ROWS, :] = ones
    gb_ref[...] = _silu(feat(2176, 2688)).astype(jnp.bfloat16)


def _in_proj(x, mod3, norm_w, w_tok, w_feat_t, qnw_col, knw_row, knws_row, cosq, sinq, cosk, sink,
             grp):
    bsz, seq, d = x.shape
    ts = PROJ_TILE
    bf = jnp.bfloat16
    kdim = SPLIT * DIFF_HEAD_DIM
    const = lambda b, i: (0, 0)
    feat_spec = lambda rows: pl.BlockSpec((None, rows, ts), lambda b, i: (b, 0, i))
    return pl.pallas_call(
        _in_proj_kernel,
        grid=(bsz, seq // ts),
        in_specs=[
            pl.BlockSpec((None, ts, d), lambda b, i: (b, i, 0)),
            pl.BlockSpec((None, 3, d), lambda b, i: (b, 0, 0)),
            pl.BlockSpec((1, d), const),
            pl.BlockSpec(w_tok.shape, const),
            pl.BlockSpec(w_feat_t.shape, const),
            pl.BlockSpec((HEAD_DIM_B, 1), const),
            pl.BlockSpec((1, 128), const),
            pl.BlockSpec((1, 128), const),
            pl.BlockSpec((HEAD_DIM_B, ts), lambda b, i: (0, i)),
            pl.BlockSpec((HEAD_DIM_B, ts), lambda b, i: (0, i)),
            pl.BlockSpec((ts, 128), lambda b, i: (i, 0)),
            pl.BlockSpec((ts, 128), lambda b, i: (i, 0)),
            pl.BlockSpec(grp.shape, const),
        ],
        out_specs=[
            pl.BlockSpec((None, ts, WIDTH_A), lambda b, i: (b, i, 0)),
            pl.BlockSpec((None, ts, 128), lambda b, i: (b, i, 0)),
            feat_spec(512),
            pl.BlockSpec((None, N_HEADS_A, DIFF_V_DIM + ONES_ROWS, ts), lambda b, i: (b, 0, 0, i)),
            feat_spec(512), feat_spec(512),
            pl.BlockSpec((None, N_KV_B, HEAD_DIM_B + ONES_ROWS, ts), lambda b, i: (b, 0, 0, i)),
            feat_spec(512),
            pl.BlockSpec((None, None, 1, 128), lambda b, i: (b, i, 0, 0)),
            pl.BlockSpec((None, 2 * N_HEADS_A, ts, kdim), lambda b, i: (b, 0, i, 0)),
            pl.BlockSpec((None, N_KV_B, ts, kdim), lambda b, i: (b, 0, i, 0)),
            pl.BlockSpec((None, 2 * N_HEADS_A, kdim, ts), lambda b, i: (b, 0, 0, i)),
            pl.BlockSpec((None, N_KV_B * GQA_GROUP, kdim, ts), lambda b, i: (b, 0, 0, i)),
        ],
        out_shape=[
            jax.ShapeDtypeStruct((bsz, seq, WIDTH_A), bf),
            jax.ShapeDtypeStruct((bsz, seq, 128), bf),
            jax.ShapeDtypeStruct((bsz, 512, seq), bf),
            jax.ShapeDtypeStruct((bsz, N_HEADS_A, DIFF_V_DIM + ONES_ROWS, seq), bf),
            jax.ShapeDtypeStruct((bsz, 512, seq), bf),
            jax.ShapeDtypeStruct((bsz, 512, seq), bf),
            jax.ShapeDtypeStruct((bsz, N_KV_B, HEAD_DIM_B + ONES_ROWS, seq), bf),
            jax.ShapeDtypeStruct((bsz, 512, seq), bf),
            jax.ShapeDtypeStruct((bsz, seq // ts, 1, 128), jnp.float32),
            jax.ShapeDtypeStruct((bsz, 2 * N_HEADS_A, seq, kdim), F8),
            jax.ShapeDtypeStruct((bsz, N_KV_B, seq, kdim), F8),
            jax.ShapeDtypeStruct((bsz, 2 * N_HEADS_A, kdim, seq), F8),
            jax.ShapeDtypeStruct((bsz, N_KV_B * GQA_GROUP, kdim, seq), F8),
        ],
        compiler_params=pltpu.CompilerParams(
            dimension_semantics=("arbitrary", "arbitrary"), vmem_limit_bytes=VMEM_LIMIT_BYTES),
        name="in_proj",
    )(x, mod3, norm_w, w_tok, w_feat_t, qnw_col, knw_row, knws_row, cosq, sinq, cosk, sink, grp)


def _softmax_over_tiles(n_rest, first_tile, tile_fn, first_logits_fn, logits_fn, off_fn, vt_fn,
                        fast_logits_fn, fast_off_fn, fast_vt_fn,
                        f8_ok, bound_ok_fn, key_chunk, lane_chunk, s_ref, p_refs, m_ref, acc_ref):
    chunks = [slice(c, c + lane_chunk) for c in range(0, m_ref.shape[1], lane_chunk)]
    tk = s_ref.shape[0]
    pieces = [(slice(r, r + key_chunk), cols) for r in range(0, tk, key_chunk) for cols in chunks]
    acc_ref[...] = jnp.zeros(acc_ref.shape, jnp.float32)

    off_first = off_fn(first_tile)

    def first_scores(fast):
        for cols in chunks:
            u = first_logits_fn(cols, fast)
            s_ref[:, cols] = u
            m_ref[:, cols] = jnp.max(u, axis=0, keepdims=True) + off_first[:, cols]

    first_scores(True)

    @pl.when(jnp.logical_not(f8_ok))
    def _():
        first_scores(False)

    fixed_ok = bound_ok_fn(m_ref[...]) & f8_ok

    def first_probs(rows, cols):
        shift = m_ref[:, cols] - off_first[:, cols]
        return jnp.exp2((s_ref[rows, cols] - shift).astype(jnp.bfloat16))

    def probs(t, slot, rows, cols):
        j = tile_fn(t)
        shift = m_ref[:, cols] - fast_off_fn(j)[:, cols]
        p_refs[slot][rows, cols] = jnp.exp2(
            (fast_logits_fn(j, rows, cols) - shift).astype(jnp.bfloat16))

    def tile_values(t):
        if t < 0:
            return {rows.start: vt_fn(first_tile, rows) for rows, _ in pieces}
        return {rows.start: fast_vt_fn(tile_fn(t), rows) for rows, _ in pieces}

    def accumulate(slot, rows, cols, vts):
        acc_ref[:, cols] += jnp.dot(vts[rows.start], p_refs[slot][rows, cols],
                                    preferred_element_type=jnp.float32)

    @pl.when(fixed_ok)
    def _():
        for rows, cols in pieces:
            p_refs[1][rows, cols] = first_probs(rows, cols)
        for t in range(-1, n_rest - 1):
            vts = tile_values(t)
            for rows, cols in pieces:
                probs(t + 1, (t + 1) % 2, rows, cols)
                accumulate(t % 2, rows, cols, vts)
        vts = tile_values(n_rest - 1)
        for rows, cols in pieces:
            accumulate((n_rest - 1) % 2, rows, cols, vts)

    @pl.when(jnp.logical_not(fixed_ok))
    def _():
        vt = vt_fn(first_tile)
        for cols in chunks:
            acc_ref[:, cols] = jnp.dot(vt, first_probs(slice(0, tk), cols),
                                       preferred_element_type=jnp.float32)

        def body(t, carry):
            j = tile_fn(t)
            off = off_fn(j)
            vt = vt_fn(j)
            for cols in chunks:
                u = logits_fn(j, cols)
                m_old = m_ref[:, cols]
                m_new = jnp.maximum(m_old, jnp.max(u, axis=0, keepdims=True) + off[:, cols])
                alpha = jnp.exp2(m_old - m_new)
                p = jnp.exp2((u - (m_new - off[:, cols])).astype(jnp.bfloat16))
                acc_ref[:, cols] = alpha * acc_ref[:, cols] + jnp.dot(
                    vt, p, preferred_element_type=jnp.float32)
                m_ref[:, cols] = m_new
            return carry

        lax.fori_loop(0, n_rest, body, 0)


def _attn_diff_kernel(qt_ref, k_ref, vt_ref, gt_ref, qf_ref, kf_ref, slope_ref, kmax_ref,
                      sublnw_ref, lq1_ref, lk1_ref, lq2_ref, lk2_ref, o_ref,
                      bias_ref, qbd_ref, s_ref, p0_ref, p1_ref, m_ref, acc_ref, *, lam_init):
    tq, tk = TQ_A, TK_A
    b = pl.program_id(0)
    hd = pl.program_id(1)
    qi = pl.program_id(2)
    n_kv = k_ref.shape[0] // tk
    sigma = slope_ref[hd] * LOG2E

    @pl.when(qi == 0)
    def _():
        key = lax.broadcasted_iota(jnp.int32, (tk, tq), 0).astype(jnp.float32)
        query = lax.broadcasted_iota(jnp.int32, (tk, tq), 1).astype(jnp.float32)
        bias_ref[...] = sigma * (query - jnp.abs(query - key))

    zeros = jnp.zeros((DIFF_HEAD_DIM, tq), jnp.bfloat16)
    qbd_ref[0:64, 0:tq] = qt_ref[0:64, :]
    qbd_ref[64:128, 0:tq] = zeros
    qbd_ref[0:64, tq:2 * tq] = zeros
    qbd_ref[64:128, tq:2 * tq] = qt_ref[64:128, :]

    lane = lax.broadcasted_iota(jnp.int32, (1, 2 * tq), 1)
    il = jnp.where(lane >= tq, lane - tq, lane).astype(jnp.float32)

    whole = slice(0, tk)

    def key_rows(j, rows):
        return pl.ds(pl.multiple_of(j * tk + rows.start, KEY_ALIGN), rows.stop - rows.start)

    def scores(j, cols):
        return jnp.dot(k_ref[key_rows(j, whole), :], qbd_ref[:, cols], preferred_element_type=jnp.float32)

    def first_logits(cols, fast):
        queries = slice(cols.start % tq, cols.start % tq + cols.stop - cols.start)
        return (fast_logits(qi, whole, cols) if fast else scores(qi, cols)) + bias_ref[:, queries]

    def logits(j, cols):
        key = lax.broadcasted_iota(jnp.int32, (tk, LANE_CHUNK_A), 0).astype(jnp.float32)
        return scores(j, cols) + jnp.where(j < qi, sigma, -sigma) * key

    def offset(j):
        per_query = jnp.where(j > qi, 2.0 * sigma, 0.0)
        return per_query * il - sigma * (jnp.abs(j - qi) * tk).astype(jnp.float32)

    def values(j, rows=whole):
        return vt_ref[:, key_rows(j, rows)]

    def fast_logits(j, rows, cols):
        mp = cols.start // tq
        queries = slice(cols.start - mp * tq, cols.stop - mp * tq)
        return jnp.dot(kf_ref[mp, key_rows(j, rows), :], qf_ref[mp, :, queries],
                       preferred_element_type=jnp.float32)

    def fast_offset(j):
        return offset(j) + jnp.where(j < qi, sigma * tk, 0.0)

    jl = lax.broadcasted_iota(jnp.int32, (1, tk), 1).astype(jnp.float32)
    factor_left = jnp.exp2(sigma * (jl - tk))
    factor_right = jnp.exp2(-sigma * jl)

    def fast_values(j, rows):
        factor = jnp.where(j < qi, factor_left[:, rows], factor_right[:, rows])
        return (values(j, rows).astype(jnp.float32) * factor).astype(jnp.bfloat16)

    q = qbd_ref[...].astype(jnp.float32)
    q_norm = jnp.sqrt(jnp.sum(q * q, axis=0, keepdims=True))
    k_norm0 = kmax_ref[(b * N_HEADS_A + hd) * 2]
    k_norm1 = kmax_ref[(b * N_HEADS_A + hd) * 2 + 1]
    bound = q_norm * jnp.where(lane < tq, k_norm0, k_norm1) + sigma * il
    f8_ok = ((jnp.max(q_norm) * (QK_SCALE_ROOT / QK_SCALE) < F8_SAFE)
             & (jnp.maximum(k_norm0, k_norm1) * QK_SCALE_ROOT < F8_SAFE))

    _softmax_over_tiles(n_kv - 1, qi, lambda t: jnp.where(t >= qi, t + 1, t),
                        first_logits, logits, offset, values,
                        fast_logits, fast_offset, fast_values,
                        f8_ok, lambda m1: jnp.max(bound - m1) <= GUARD_LOG2,
                        KEY_CHUNK_A, LANE_CHUNK_A, s_ref, (p0_ref, p1_ref), m_ref, acc_ref)

    lam = (jnp.exp(jnp.sum(lq1_ref[...] * lk1_ref[...], axis=1, keepdims=True))
           - jnp.exp(jnp.sum(lq2_ref[...] * lk2_ref[...], axis=1, keepdims=True)) + lam_init)
    inv_l = 1.0 / acc_ref[DIFF_V_DIM:DIFF_V_DIM + 1, :]
    o1 = acc_ref[0:DIFF_V_DIM, 0:tq] * inv_l[:, 0:tq]
    o2 = acc_ref[0:DIFF_V_DIM, tq:2 * tq] * inv_l[:, tq:2 * tq]
    diff = o1 - lam * o2
    ms = jnp.mean(diff * diff, axis=0, keepdims=True)
    y = diff * lax.rsqrt(ms + NORM_EPS) * sublnw_ref[...] * (1.0 - lam_init)
    gated = y * gt_ref[...].astype(jnp.float32)
    o_ref[...] = gated.T.astype(o_ref.dtype)


def _attn_diff(qa_t, ka, va_t, ga_t, qa_f8, ka_f8, slopes, kmax, subln_col, lq1, lk1, lq2, lk2,
               lam_init):
    bsz, seq, _ = ka.shape
    tq = TQ_A
    n = 2 * tq
    kdim = SPLIT * DIFF_HEAD_DIM
    vec = pl.BlockSpec((1, DIFF_HEAD_DIM), lambda b, h, i: (0, 0))
    smem = pl.BlockSpec(memory_space=pltpu.SMEM)
    return pl.pallas_call(
        functools.partial(_attn_diff_kernel, lam_init=lam_init),
        grid=(bsz, N_HEADS_A, seq // tq),
        in_specs=[
            pl.BlockSpec((None, 128, tq), lambda b, h, i: (b, h, i)),
            pl.BlockSpec((None, seq, 128), lambda b, h, i: (b, 0, h)),
            pl.BlockSpec((None, None, DIFF_V_DIM + ONES_ROWS, seq), lambda b, h, i: (b, h, 0, 0)),
            pl.BlockSpec((None, 128, tq), lambda b, h, i: (b, h, i)),
            pl.BlockSpec((None, None, 2, kdim, tq), lambda b, h, i: (b, h, 0, 0, i)),
            pl.BlockSpec((None, None, 2, seq, kdim), lambda b, h, i: (b, h, 0, 0, 0)),
            smem, smem,
            pl.BlockSpec((DIFF_V_DIM, 1), lambda b, h, i: (0, 0)),
            vec, vec, vec, vec,
        ],
        out_specs=pl.BlockSpec((None, tq, 128), lambda b, h, i: (b, i, h)),
        out_shape=jax.ShapeDtypeStruct((bsz, seq, WIDTH_A), jnp.bfloat16),
        scratch_shapes=[
            pltpu.VMEM((TK_A, tq), jnp.float32),
            pltpu.VMEM((128, n), jnp.bfloat16),
            pltpu.VMEM((TK_A, n), jnp.float32),
            pltpu.VMEM((TK_A, n), jnp.bfloat16),
            pltpu.VMEM((TK_A, n), jnp.bfloat16),
            pltpu.VMEM((1, n), jnp.float32),
            pltpu.VMEM((DIFF_V_DIM + ONES_ROWS, n), jnp.float32),
        ],
        compiler_params=pltpu.CompilerParams(
            dimension_semantics=("arbitrary", "arbitrary", "arbitrary"),
            vmem_limit_bytes=VMEM_LIMIT_BYTES),
        name="attn_diff",
    )(qa_t, ka, va_t, ga_t, qa_f8, ka_f8, slopes, kmax, subln_col, lq1, lk1, lq2, lk2)


def _attn_gqa_kernel(qt_ref, k_ref, vt_ref, gt_ref, qf_ref, kf_ref, kmax_ref, o_ref,
                     qp_ref, qpf_ref, s_ref, p0_ref, p1_ref, m_ref, acc_ref):
    tq, tk = TQ_B, TK_B
    g = pl.program_id(1)
    n_kv = k_ref.shape[0] // tk
    n = GQA_GROUP * tq

    row = lax.broadcasted_iota(jnp.int32, (128, tq), 0)
    lo = g * HEAD_DIM_B
    mine = (row >= lo) & (row < lo + HEAD_DIM_B)
    for r in range(GQA_GROUP):
        q = qt_ref[r * 64:(r + 1) * 64, :].astype(jnp.float32)
        q2 = jnp.concatenate([q, q], axis=0)
        qp_ref[:, r * tq:(r + 1) * tq] = jnp.where(mine, q2, 0.0).astype(jnp.bfloat16)
        qpf_ref[:, r * tq:(r + 1) * tq] = qf_ref[r]

    whole = slice(0, tk)

    def key_rows(j, rows):
        return pl.ds(pl.multiple_of(j * tk + rows.start, KEY_ALIGN), rows.stop - rows.start)

    def logits(j, cols):
        return jnp.dot(k_ref[key_rows(j, whole), :], qp_ref[:, cols], preferred_element_type=jnp.float32)

    def fast_logits(j, rows, cols):
        return jnp.dot(kf_ref[key_rows(j, rows), :], qpf_ref[:, cols], preferred_element_type=jnp.float32)

    def first_logits(cols, fast):
        return fast_logits(jnp.int32(0), whole, cols) if fast else logits(jnp.int32(0), cols)

    def values(j, rows=whole):
        return vt_ref[:, key_rows(j, rows)]

    no_offset = jnp.zeros((1, n), jnp.float32)
    q = qp_ref[...].astype(jnp.float32)
    q_norm = jnp.sqrt(jnp.sum(q * q, axis=0, keepdims=True))
    bound = q_norm * kmax_ref[0]
    f8_ok = ((jnp.max(q_norm) * (QK_SCALE_ROOT / QK_SCALE) < F8_SAFE)
             & (kmax_ref[0] * QK_SCALE_ROOT < F8_SAFE))

    _softmax_over_tiles(n_kv - 1, jnp.int32(0), lambda t: jnp.asarray(t, jnp.int32) + 1,
                        first_logits, logits, lambda j: no_offset, values,
                        fast_logits, lambda j: no_offset, values,
                        f8_ok, lambda m1: jnp.max(bound - m1) <= GUARD_LOG2,
                        KEY_CHUNK_B, LANE_CHUNK_B, s_ref, (p0_ref, p1_ref), m_ref, acc_ref)

    o = acc_ref[0:HEAD_DIM_B, :] * (1.0 / acc_ref[HEAD_DIM_B:HEAD_DIM_B + 1, :])
    o = jnp.concatenate([o[:, r * tq:(r + 1) * tq] for r in range(GQA_GROUP)], axis=0)
    gated = o * gt_ref[...].astype(jnp.float32)
    o_ref[...] = gated.T.astype(o_ref.dtype)


def _attn_gqa(qb_t, kb, vb_t, gb_t, qb_f8, kb_f8, kmax):
    bsz, seq, _ = kb.shape
    tq = TQ_B
    n = GQA_GROUP * tq
    kdim = SPLIT * HEAD_DIM_B
    return pl.pallas_call(
        _attn_gqa_kernel,
        grid=(bsz, N_KV_B, seq // tq),
        in_specs=[
            pl.BlockSpec((None, 256, tq), lambda b, g, i: (b, g, i)),
            pl.BlockSpec((None, seq, 128), lambda b, g, i: (b, 0, 0)),
            pl.BlockSpec((None, None, HEAD_DIM_B + ONES_ROWS, seq), lambda b, g, i: (b, g, 0, 0)),
            pl.BlockSpec((None, 256, tq), lambda b, g, i: (b, g, i)),
            pl.BlockSpec((None, GQA_GROUP, kdim, tq), lambda b, g, i: (b, g, 0, i)),
            pl.BlockSpec((None, None, seq, kdim), lambda b, g, i: (b, g, 0, 0)),
            pl.BlockSpec(memory_space=pltpu.SMEM),
        ],
        out_specs=pl.BlockSpec((None, tq, 256), lambda b, g, i: (b, i, g)),
        out_shape=jax.ShapeDtypeStruct((bsz, seq, WIDTH_B), jnp.bfloat16),
        scratch_shapes=[
            pltpu.VMEM((128, n), jnp.bfloat16),
            pltpu.VMEM((kdim, n), F8),
            pltpu.VMEM((TK_B, n), jnp.float32),
            pltpu.VMEM((TK_B, n), jnp.bfloat16),
            pltpu.VMEM((TK_B, n), jnp.bfloat16),
            pltpu.VMEM((1, n), jnp.float32),
            pltpu.VMEM((HEAD_DIM_B + ONES_ROWS, n), jnp.float32),
        ],
        compiler_params=pltpu.CompilerParams(
            dimension_semantics=("arbitrary", "arbitrary", "arbitrary"),
            vmem_limit_bytes=VMEM_LIMIT_BYTES),
        name="attn_gqa",
    )(qb_t, kb, vb_t, gb_t, qb_f8, kb_f8, kmax)


def _out_proj_kernel(ga_ref, gb_ref, wa_ref, wb_ref, x_ref, mod_ref, fw_ref, o_ref):
    for r in range(0, x_ref.shape[0], OUT_ROW_CHUNK):
        rows = slice(r, r + OUT_ROW_CHUNK)
        y = jnp.dot(ga_ref[rows, :], wa_ref[...], preferred_element_type=jnp.float32)
        y = y + jnp.dot(gb_ref[rows, :], wb_ref[...], preferred_element_type=jnp.float32)
        z = x_ref[rows, :] + mod_ref[2:3, :] * y
        ms = jnp.mean(z * z, axis=1, keepdims=True)
        o_ref[rows, :] = z * lax.rsqrt(ms + NORM_EPS) * fw_ref[...]


def _out_proj(ga, gb, w_a, w_b, x, mod3, final_w):
    bsz, seq, d = x.shape
    ts = PROJ_TILE
    const = lambda b, i: (0, 0)
    return pl.pallas_call(
        _out_proj_kernel,
        grid=(bsz, seq // ts),
        in_specs=[
            pl.BlockSpec((None, ts, WIDTH_A), lambda b, i: (b, i, 0)),
            pl.BlockSpec((None, ts, WIDTH_B), lambda b, i: (b, i, 0)),
            pl.BlockSpec(w_a.shape, const),
            pl.BlockSpec(w_b.shape, const),
            pl.BlockSpec((None, ts, d), lambda b, i: (b, i, 0)),
            pl.BlockSpec((None, 3, d), lambda b, i: (b, 0, 0)),
            pl.BlockSpec((1, d), const),
        ],
        out_specs=pl.BlockSpec((None, ts, d), lambda b, i: (b, i, 0)),
        out_shape=jax.ShapeDtypeStruct((bsz, seq, d), jnp.float32),
        compiler_params=pltpu.CompilerParams(
            dimension_semantics=("arbitrary", "arbitrary"), vmem_limit_bytes=VMEM_LIMIT_BYTES),
        name="out_proj",
    )(ga, gb, w_a, w_b, x, mod3, final_w)


def _rope_tables(seq):
    pos = np.arange(seq)
    row = (pos // GRID_W).astype(np.float64)
    col = (pos % GRID_W).astype(np.float64)
    n_freq = ROT_HALF // 2
    freqs = 1.0 / (ROPE_THETA ** (np.arange(n_freq, dtype=np.float64) * 2.0 / ROT_HALF))
    ang_r = row[:, None] * freqs[None, :]
    ang_c = col[:, None] * freqs[None, :]
    cos = np.concatenate([np.cos(ang_r), np.cos(ang_r), np.cos(ang_c), np.cos(ang_c)], axis=1)
    sin = np.concatenate([-np.sin(ang_r), np.sin(ang_r), -np.sin(ang_c), np.sin(ang_c)], axis=1)
    d = np.arange(HEAD_DIM_B)
    partner = np.where((d // n_freq) % 2 == 0, d + n_freq, d - n_freq)
    return cos.astype(np.float32), sin.astype(np.float32), partner


def kernel(x, c, w_ada, b_ada, norm_w, w_in, lambda_q1, lambda_k1, lambda_q2, lambda_k2,
           subln_w, q_norm_w, k_norm_w, w_out, final_norm_w):
    assert w_ada.shape[0] == 1, "single-layer problem: the final norm is fused into the output projection"
    bsz, seq, d = x.shape
    assert seq % TQ_A == 0 and seq % TK_B == 0 and seq % PROJ_TILE == 0
    assert (seq // TK_A) % 2 == 0 and (seq // TK_B) % 2 == 0
    bf = jnp.bfloat16
    kdim = SPLIT * DIFF_HEAD_DIM
    cos, sin, partner = _rope_tables(seq)
    cosq, sinq = (jnp.asarray(np.ascontiguousarray(t.T)) for t in (cos, sin))
    cosk, sink = (jnp.asarray(np.tile(t, (1, N_KV_B))) for t in (cos, sin))
    slopes = jnp.asarray(2.0 ** (-8.0 * np.arange(1, N_HEADS_A + 1) / N_HEADS_A), jnp.float32)
    lam_init = 0.8 - 0.6 * math.exp(-0.3 * 0)
    grp = jnp.asarray(np.arange(WIDTH_A)[:, None] // DIFF_HEAD_DIM == np.arange(128)[None, :], jnp.float32)

    w = w_in[0]
    q_a, k_a, v_a, g_a = w[:, 0:512], w[:, 512:1024], w[:, 1024:1536], w[:, 1536:2048]
    q_b, k_b, v_b, g_b = w[:, 2048:2560], w[:, 2560:2688], w[:, 2688:2816], w[:, 2816:3328]
    partner2 = np.concatenate([partner, partner + HEAD_DIM_B])
    w_tok = jnp.concatenate([k_a, k_b, k_b[:, partner2]], axis=1).astype(bf)
    w_feat_t = jnp.concatenate([q_a, v_a, g_a, q_b, v_b, g_b], axis=1).T.astype(bf)
    knw = jnp.tile(k_norm_w[0], N_KV_B)
    wo = w_out[0].astype(bf)

    mod3 = _adaln_mod(c, w_ada[0], b_ada[0]).reshape(bsz, 3, d)
    ka, kb, qa_t, va_t, ga_t, qb_t, vb_t, gb_t, kn, ka_f8, kb_f8, qa_f8, qb_f8 = _in_proj(
        x, mod3, norm_w[0].reshape(1, d), w_tok, w_feat_t,
        q_norm_w[0].reshape(HEAD_DIM_B, 1), knw.reshape(1, 128), knw[partner2].reshape(1, 128),
        cosq, sinq, cosk, sink, grp)
    kmax_a = (jnp.sqrt(jnp.max(kn[:, :, 0, :2 * N_HEADS_A], axis=1)) * NORM_MARGIN).reshape(-1)
    kmax_b = (math.sqrt(HEAD_DIM_B) * NORM_MARGIN * jnp.max(jnp.abs(k_norm_w[0]))).reshape(1)
    oa = _attn_diff(qa_t, ka, va_t, ga_t,
                    qa_f8.reshape(bsz, N_HEADS_A, 2, kdim, seq), ka_f8.reshape(bsz, N_HEADS_A, 2, seq, kdim),
                    slopes, kmax_a, subln_w[0].reshape(DIFF_V_DIM, 1),
                    lambda_q1[0].reshape(1, -1), lambda_k1[0].reshape(1, -1),
                    lambda_q2[0].reshape(1, -1), lambda_k2[0].reshape(1, -1), lam_init)
    ob = _attn_gqa(qb_t, kb, vb_t, gb_t, qb_f8, kb_f8, kmax_b)
    return _out_proj(oa, ob, wo[:WIDTH_A], wo[WIDTH_A:], x, mod3, final_norm_w.reshape(1, d))
```

```python
import functools
import math

import numpy as np
import jax
import jax.numpy as jnp
from jax import lax
from jax.experimental import pallas as pl
from jax.experimental.pallas import tpu as pltpu

D_MODEL = 1024
N_HEADS_A = 4
DIFF_HEAD_DIM = 64
DIFF_V_DIM = 128
WIDTH_A = 512
N_KV_B = 2
GQA_GROUP = 4
HEAD_DIM_B = 64
WIDTH_B = 512
GRID_W = 64
ROPE_THETA = 10000.0
ROT_HALF = 32
NORM_EPS = 1e-6
MXU_TILE = 256
LANE_CHUNK = MXU_TILE
KEY_CHUNK = 2 * MXU_TILE
KEY_ALIGN = MXU_TILE
ONES_ROWS = 16
LOG2E = 1.4426950408889634
NEG_BIG = -1e30
GUARD_LOG2 = 100.0
NORM_MARGIN = 1.01
F8 = jnp.float8_e4m3fn
F8_SAFE = 440.0
SPLIT = 4
QK_SCALE = LOG2E / math.sqrt(DIFF_HEAD_DIM)
QK_SCALE_ROOT = math.sqrt(QK_SCALE)

VMEM_LIMIT_BYTES = 56 * 1024 * 1024

PROJ_TILE = 1024
OUT_ROW_CHUNK = 256
TQ_A = 512
TK_A = 512
TQ_B = 512
TK_B = 512

assert DIFF_HEAD_DIM == HEAD_DIM_B and TQ_A == TK_A
assert TQ_A % LANE_CHUNK == 0 and TQ_B % LANE_CHUNK == 0 and TK_A % KEY_CHUNK == 0 and TK_B % KEY_CHUNK == 0


def _silu(v):
    return v * (1.0 / (1.0 + jnp.exp(-v)))


def _split_f8(v):
    hi = v.astype(F8)
    lo = (v - hi.astype(jnp.float32)).astype(F8)
    return hi, lo


def _mod_kernel(ct_ref, w_ref, b_ref, o_ref):
    s = _silu(ct_ref[...])
    w = w_ref[...]
    for b in range(ct_ref.shape[1]):
        o_ref[b:b + 1, :] = jnp.sum(w * s[:, b:b + 1], axis=0, keepdims=True) + b_ref[...]


def _adaln_mod(c, w_ada, b_ada):
    bsz, d = c.shape
    n = w_ada.shape[1]
    tn = 1024
    return pl.pallas_call(
        _mod_kernel,
        grid=(n // tn,),
        in_specs=[pl.BlockSpec((d, bsz), lambda j: (0, 0)),
                  pl.BlockSpec((d, tn), lambda j: (0, j)),
                  pl.BlockSpec((1, tn), lambda j: (0, j))],
        out_specs=pl.BlockSpec((bsz, tn), lambda j: (0, j)),
        out_shape=jax.ShapeDtypeStruct((bsz, n), jnp.float32),
        name="adaln_mod",
    )(c.T, w_ada, b_ada.reshape(1, n))


def _store_keys_f8(k2, out_ref, first):
    lane = lax.broadcasted_iota(jnp.int32, k2.shape, 1)
    swapped = pltpu.roll(k2, HEAD_DIM_B, axis=1)
    for g, dup in enumerate((jnp.where(lane < HEAD_DIM_B, k2, swapped),
                             jnp.where(lane < HEAD_DIM_B, swapped, k2))):
        hi, lo = _split_f8(dup)
        out_ref[first + g, :, 0:128] = hi
        out_ref[first + g, :, 128:256] = lo


def _store_queries_f8(q, out_ref, g):
    hi, lo = _split_f8(q)
    for part, val in enumerate((hi, lo, hi, lo)):
        out_ref[g, part * 64:(part + 1) * 64, :] = val


def _in_proj_kernel(x_ref, mod_ref, nw_ref, wtok_ref, wfeat_ref, qnw_ref, knw_ref, knws_ref,
                    cosq_ref, sinq_ref, cosk_ref, sink_ref, grp_ref,
                    ka_ref, kb_ref, qa_ref, va_ref, ga_ref, qb_ref, vb_ref, gb_ref, kn_ref,
                    kaf_ref, kbf_ref, qaf_ref, qbf_ref):
    x = x_ref[...]
    shift = mod_ref[0:1, :]
    scale = mod_ref[1:2, :]
    ms = jnp.mean(x * x, axis=1, keepdims=True)
    h = (x * lax.rsqrt(ms + NORM_EPS)) * nw_ref[...] * (1.0 + scale) + shift
    hb = h.astype(jnp.bfloat16)

    tok = jnp.dot(hb, wtok_ref[...], preferred_element_type=jnp.float32)
    ka = tok[:, 0:WIDTH_A]
    ka_ref[...] = ka.astype(jnp.bfloat16)
    for pair in range(N_HEADS_A):
        _store_keys_f8(ka[:, pair * 128:(pair + 1) * 128] * QK_SCALE_ROOT, kaf_ref, 2 * pair)
    kn = jnp.dot(ka * ka, grp_ref[...], preferred_element_type=jnp.float32)
    kn_ref[...] = jnp.max(kn, axis=0, keepdims=True)
    kb = tok[:, 512:640]
    kbs = tok[:, 640:768]
    sq = kb * kb
    r0 = lax.rsqrt(jnp.sum(sq[:, 0:64], axis=1, keepdims=True) * (1.0 / HEAD_DIM_B) + NORM_EPS)
    r1 = lax.rsqrt(jnp.sum(sq[:, 64:128], axis=1, keepdims=True) * (1.0 / HEAD_DIM_B) + NORM_EPS)
    lane = lax.broadcasted_iota(jnp.int32, kb.shape, 1)
    r = jnp.where(lane < HEAD_DIM_B, r0, r1)
    kr = r * ((kb * knw_ref[...]) * cosk_ref[...] + (kbs * knws_ref[...]) * sink_ref[...])
    kb_ref[...] = kr.astype(jnp.bfloat16)
    _store_keys_f8(kr * QK_SCALE_ROOT, kbf_ref, 0)

    def feat(lo, hi):
        return lax.dot_general(wfeat_ref[lo:hi, :], hb, (((1,), (1,)), ((), ())),
                               preferred_element_type=jnp.float32)

    qa = feat(0, 512)
    qa_ref[...] = (qa * QK_SCALE).astype(jnp.bfloat16)
    for g in range(2 * N_HEADS_A):
        _store_queries_f8(qa[g * 64:(g + 1) * 64, :] * QK_SCALE_ROOT, qaf_ref, g)
    va = feat(512, 1024).astype(jnp.bfloat16)
    ones = jnp.ones((ONES_ROWS, va.shape[1]), jnp.bfloat16)
    for hd in range(N_HEADS_A):
        va_ref[hd, 0:DIFF_V_DIM, :] = va[hd * DIFF_V_DIM:(hd + 1) * DIFF_V_DIM, :]
        va_ref[hd, DIFF_V_DIM:DIFF_V_DIM + ONES_ROWS, :] = ones
    ga_ref[...] = _silu(feat(1024, 1536)).astype(jnp.bfloat16)
    qb = feat(1536, 2048)
    cq = cosq_ref[...]
    sq_ = sinq_ref[...]
    qnw = qnw_ref[...]
    for hd in range(N_KV_B * GQA_GROUP):
        q = qb[hd * 64:(hd + 1) * 64, :]
        rq = lax.rsqrt(jnp.mean(q * q, axis=0, keepdims=True) + NORM_EPS)
        qn = q * rq * qnw
        partner = jnp.concatenate([qn[16:32], qn[0:16], qn[48:64], qn[32:48]], axis=0)
        rot = qn * cq + partner * sq_
        qb_ref[hd * 64:(hd + 1) * 64, :] = (rot * QK_SCALE).astype(jnp.bfloat16)
        _store_queries_f8(rot * QK_SCALE_ROOT, qbf_ref, hd)
    vb = feat(2048, 2176).astype(jnp.bfloat16)
    for g in range(N_KV_B):
        vb_ref[g, 0:HEAD_DIM_B, :] = vb[g * HEAD_DIM_B:(g + 1) * HEAD_DIM_B, :]
        vb_ref[g, HEAD_DIM_B:HEAD_DIM_B + ONES_ROWS, :] = ones
    gb_ref[...] = _silu(feat(2176, 2688)).astype(jnp.bfloat16)


def _in_proj(x, mod3, norm_w, w_tok, w_feat_t, qnw_col, knw_row, knws_row, cosq, sinq, cosk, sink,
             grp):
    bsz, seq, d = x.shape
    ts = PROJ_TILE
    bf = jnp.bfloat16
    kdim = SPLIT * DIFF_HEAD_DIM
    const = lambda b, i: (0, 0)
    feat_spec = lambda rows: pl.BlockSpec((None, rows, ts), lambda b, i: (b, 0, i))
    return pl.pallas_call(
        _in_proj_kernel,
        grid=(bsz, seq // ts),
        in_specs=[
            pl.BlockSpec((None, ts, d), lambda b, i: (b, i, 0)),
            pl.BlockSpec((None, 3, d), lambda b, i: (b, 0, 0)),
            pl.BlockSpec((1, d), const),
            pl.BlockSpec(w_tok.shape, const),
            pl.BlockSpec(w_feat_t.shape, const),
            pl.BlockSpec((HEAD_DIM_B, 1), const),
            pl.BlockSpec((1, 128), const),
            pl.BlockSpec((1, 128), const),
            pl.BlockSpec((HEAD_DIM_B, ts), lambda b, i: (0, i)),
            pl.BlockSpec((HEAD_DIM_B, ts), lambda b, i: (0, i)),
            pl.BlockSpec((ts, 128), lambda b, i: (i, 0)),
            pl.BlockSpec((ts, 128), lambda b, i: (i, 0)),
            pl.BlockSpec(grp.shape, const),
        ],
        out_specs=[
            pl.BlockSpec((None, ts, WIDTH_A), lambda b, i: (b, i, 0)),
            pl.BlockSpec((None, ts, 128), lambda b, i: (b, i, 0)),
            feat_spec(512),
            pl.BlockSpec((None, N_HEADS_A, DIFF_V_DIM + ONES_ROWS, ts), lambda b, i: (b, 0, 0, i)),
            feat_spec(512), feat_spec(512),
            pl.BlockSpec((None, N_KV_B, HEAD_DIM_B + ONES_ROWS, ts), lambda b, i: (b, 0, 0, i)),
            feat_spec(512),
            pl.BlockSpec((None, None, 1, 128), lambda b, i: (b, i, 0, 0)),
            pl.BlockSpec((None, 2 * N_HEADS_A, ts, kdim), lambda b, i: (b, 0, i, 0)),
            pl.BlockSpec((None, N_KV_B, ts, kdim), lambda b, i: (b, 0, i, 0)),
            pl.BlockSpec((None, 2 * N_HEADS_A, kdim, ts), lambda b, i: (b, 0, 0, i)),
            pl.BlockSpec((None, N_KV_B * GQA_GROUP, kdim, ts), lambda b, i: (b, 0, 0, i)),
        ],
        out_shape=[
            jax.ShapeDtypeStruct((bsz, seq, WIDTH_A), bf),
            jax.ShapeDtypeStruct((bsz, seq, 128), bf),
            jax.ShapeDtypeStruct((bsz, 512, seq), bf),
            jax.ShapeDtypeStruct((bsz, N_HEADS_A, DIFF_V_DIM + ONES_ROWS, seq), bf),
            jax.ShapeDtypeStruct((bsz, 512, seq), bf),
            jax.ShapeDtypeStruct((bsz, 512, seq), bf),
            jax.ShapeDtypeStruct((bsz, N_KV_B, HEAD_DIM_B + ONES_ROWS, seq), bf),
            jax.ShapeDtypeStruct((bsz, 512, seq), bf),
            jax.ShapeDtypeStruct((bsz, seq // ts, 1, 128), jnp.float32),
            jax.ShapeDtypeStruct((bsz, 2 * N_HEADS_A, seq, kdim), F8),
            jax.ShapeDtypeStruct((bsz, N_KV_B, seq, kdim), F8),
            jax.ShapeDtypeStruct((bsz, 2 * N_HEADS_A, kdim, seq), F8),
            jax.ShapeDtypeStruct((bsz, N_KV_B * GQA_GROUP, kdim, seq), F8),
        ],
        compiler_params=pltpu.CompilerParams(
            dimension_semantics=("arbitrary", "arbitrary"), vmem_limit_bytes=VMEM_LIMIT_BYTES),
        name="in_proj",
    )(x, mod3, norm_w, w_tok, w_feat_t, qnw_col, knw_row, knws_row, cosq, sinq, cosk, sink, grp)


def _softmax_over_tiles(n_rest, first_tile, tile_fn, first_logits_fn, logits_fn, off_fn, vt_fn,
                        fast_logits_fn, fast_off_fn, fast_vt_fn,
                        f8_ok, bound_ok_fn, s_ref, p_refs, m_ref, acc_ref):
    chunks = [slice(c, c + LANE_CHUNK) for c in range(0, m_ref.shape[1], LANE_CHUNK)]
    tk = s_ref.shape[0]
    pieces = [(slice(r, r + KEY_CHUNK), cols) for r in range(0, tk, KEY_CHUNK) for cols in chunks]
    acc_ref[...] = jnp.zeros(acc_ref.shape, jnp.float32)

    off_first = off_fn(first_tile)

    def first_scores(fast):
        for cols in chunks:
            u = first_logits_fn(cols, fast)
            s_ref[:, cols] = u
            m_ref[:, cols] = jnp.max(u, axis=0, keepdims=True) + off_first[:, cols]

    first_scores(True)

    @pl.when(jnp.logical_not(f8_ok))
    def _():
        first_scores(False)

    fixed_ok = bound_ok_fn(m_ref[...]) & f8_ok

    def first_probs(rows, cols):
        shift = m_ref[:, cols] - off_first[:, cols]
        return jnp.exp2((s_ref[rows, cols] - shift).astype(jnp.bfloat16))

    def probs(t, slot, rows, cols):
        j = tile_fn(t)
        shift = m_ref[:, cols] - fast_off_fn(j)[:, cols]
        p_refs[slot][rows, cols] = jnp.exp2(
            (fast_logits_fn(j, rows, cols) - shift).astype(jnp.bfloat16))

    def tile_values(t):
        if t < 0:
            return {rows.start: vt_fn(first_tile, rows) for rows, _ in pieces}
        return {rows.start: fast_vt_fn(tile_fn(t), rows) for rows, _ in pieces}

    def accumulate(slot, rows, cols, vts):
        acc_ref[:, cols] += jnp.dot(vts[rows.start], p_refs[slot][rows, cols],
                                    preferred_element_type=jnp.float32)

    @pl.when(fixed_ok)
    def _():
        for rows, cols in pieces:
            p_refs[1][rows, cols] = first_probs(rows, cols)
        for t in range(-1, n_rest - 1):
            vts = tile_values(t)
            for rows, cols in pieces:
                probs(t + 1, (t + 1) % 2, rows, cols)
                accumulate(t % 2, rows, cols, vts)
        vts = tile_values(n_rest - 1)
        for rows, cols in pieces:
            accumulate((n_rest - 1) % 2, rows, cols, vts)

    @pl.when(jnp.logical_not(fixed_ok))
    def _():
        vt = vt_fn(first_tile)
        for cols in chunks:
            acc_ref[:, cols] = jnp.dot(vt, first_probs(slice(0, tk), cols),
                                       preferred_element_type=jnp.float32)

        def body(t, carry):
            j = tile_fn(t)
            off = off_fn(j)
            vt = vt_fn(j)
            for cols in chunks:
                u = logits_fn(j, cols)
                m_old = m_ref[:, cols]
                m_new = jnp.maximum(m_old, jnp.max(u, axis=0, keepdims=True) + off[:, cols])
                alpha = jnp.exp2(m_old - m_new)
                p = jnp.exp2((u - (m_new - off[:, cols])).astype(jnp.bfloat16))
                acc_ref[:, cols] = alpha * acc_ref[:, cols] + jnp.dot(
                    vt, p, preferred_element_type=jnp.float32)
                m_ref[:, cols] = m_new
            return carry

        lax.fori_loop(0, n_rest, body, 0)


def _attn_diff_kernel(qt_ref, k_ref, vt_ref, gt_ref, qf_ref, kf_ref, slope_ref, kmax_ref,
                      sublnw_ref, lq1_ref, lk1_ref, lq2_ref, lk2_ref, o_ref,
                      bias_ref, qbd_ref, s_ref, p0_ref, p1_ref, m_ref, acc_ref, *, lam_init):
    tq, tk = TQ_A, TK_A
    b = pl.program_id(0)
    hd = pl.program_id(1)
    qi = pl.program_id(2)
    n_kv = k_ref.shape[0] // tk
    sigma = slope_ref[hd] * LOG2E

    @pl.when(qi == 0)
    def _():
        key = lax.broadcasted_iota(jnp.int32, (tk, tq), 0).astype(jnp.float32)
        query = lax.broadcasted_iota(jnp.int32, (tk, tq), 1).astype(jnp.float32)
        bias_ref[...] = sigma * (query - jnp.abs(query - key))

    zeros = jnp.zeros((DIFF_HEAD_DIM, tq), jnp.bfloat16)
    qbd_ref[0:64, 0:tq] = qt_ref[0:64, :]
    qbd_ref[64:128, 0:tq] = zeros
    qbd_ref[0:64, tq:2 * tq] = zeros
    qbd_ref[64:128, tq:2 * tq] = qt_ref[64:128, :]

    lane = lax.broadcasted_iota(jnp.int32, (1, 2 * tq), 1)
    il = jnp.where(lane >= tq, lane - tq, lane).astype(jnp.float32)

    whole = slice(0, tk)

    def key_rows(j, rows):
        return pl.ds(pl.multiple_of(j * tk + rows.start, KEY_ALIGN), rows.stop - rows.start)

    def scores(j, cols):
        return jnp.dot(k_ref[key_rows(j, whole), :], qbd_ref[:, cols], preferred_element_type=jnp.float32)

    def first_logits(cols, fast):
        queries = slice(cols.start % tq, cols.start % tq + cols.stop - cols.start)
        return (fast_logits(qi, whole, cols) if fast else scores(qi, cols)) + bias_ref[:, queries]

    def logits(j, cols):
        key = lax.broadcasted_iota(jnp.int32, (tk, LANE_CHUNK), 0).astype(jnp.float32)
        return scores(j, cols) + jnp.where(j < qi, sigma, -sigma) * key

    def offset(j):
        per_query = jnp.where(j > qi, 2.0 * sigma, 0.0)
        return per_query * il - sigma * (jnp.abs(j - qi) * tk).astype(jnp.float32)

    def values(j, rows=whole):
        return vt_ref[:, key_rows(j, rows)]

    def fast_logits(j, rows, cols):
        mp = cols.start // tq
        queries = slice(cols.start - mp * tq, cols.stop - mp * tq)
        return jnp.dot(kf_ref[mp, key_rows(j, rows), :], qf_ref[mp, :, queries],
                       preferred_element_type=jnp.float32)

    def fast_offset(j):
        return offset(j) + jnp.where(j < qi, sigma * tk, 0.0)

    jl = lax.broadcasted_iota(jnp.int32, (1, tk), 1).astype(jnp.float32)
    factor_left = jnp.exp2(sigma * (jl - tk))
    factor_right = jnp.exp2(-sigma * jl)

    def fast_values(j, rows):
        factor = jnp.where(j < qi, factor_left[:, rows], factor_right[:, rows])
        return (values(j, rows).astype(jnp.float32) * factor).astype(jnp.bfloat16)

    q = qbd_ref[...].astype(jnp.float32)
    q_norm = jnp.sqrt(jnp.sum(q * q, axis=0, keepdims=True))
    k_norm0 = kmax_ref[(b * N_HEADS_A + hd) * 2]
    k_norm1 = kmax_ref[(b * N_HEADS_A + hd) * 2 + 1]
    bound = q_norm * jnp.where(lane < tq, k_norm0, k_norm1) + sigma * il
    f8_ok = ((jnp.max(q_norm) * (QK_SCALE_ROOT / QK_SCALE) < F8_SAFE)
             & (jnp.maximum(k_norm0, k_norm1) * QK_SCALE_ROOT < F8_SAFE))

    _softmax_over_tiles(n_kv - 1, qi, lambda t: jnp.where(t >= qi, t + 1, t),
                        first_logits, logits, offset, values,
                        fast_logits, fast_offset, fast_values,
                        f8_ok, lambda m1: jnp.max(bound - m1) <= GUARD_LOG2,
                        s_ref, (p0_ref, p1_ref), m_ref, acc_ref)

    lam = (jnp.exp(jnp.sum(lq1_ref[...] * lk1_ref[...], axis=1, keepdims=True))
           - jnp.exp(jnp.sum(lq2_ref[...] * lk2_ref[...], axis=1, keepdims=True)) + lam_init)
    inv_l = 1.0 / acc_ref[DIFF_V_DIM:DIFF_V_DIM + 1, :]
    o1 = acc_ref[0:DIFF_V_DIM, 0:tq] * inv_l[:, 0:tq]
    o2 = acc_ref[0:DIFF_V_DIM, tq:2 * tq] * inv_l[:, tq:2 * tq]
    diff = o1 - lam * o2
    ms = jnp.mean(diff * diff, axis=0, keepdims=True)
    y = diff * lax.rsqrt(ms + NORM_EPS) * sublnw_ref[...] * (1.0 - lam_init)
    gated = y * gt_ref[...].astype(jnp.float32)
    o_ref[...] = gated.T.astype(o_ref.dtype)


def _attn_diff(qa_t, ka, va_t, ga_t, qa_f8, ka_f8, slopes, kmax, subln_col, lq1, lk1, lq2, lk2,
               lam_init):
    bsz, seq, _ = ka.shape
    tq = TQ_A
    n = 2 * tq
    kdim = SPLIT * DIFF_HEAD_DIM
    vec = pl.BlockSpec((1, DIFF_HEAD_DIM), lambda b, h, i: (0, 0))
    smem = pl.BlockSpec(memory_space=pltpu.SMEM)
    return pl.pallas_call(
        functools.partial(_attn_diff_kernel, lam_init=lam_init),
        grid=(bsz, N_HEADS_A, seq // tq),
        in_specs=[
            pl.BlockSpec((None, 128, tq), lambda b, h, i: (b, h, i)),
            pl.BlockSpec((None, seq, 128), lambda b, h, i: (b, 0, h)),
            pl.BlockSpec((None, None, DIFF_V_DIM + ONES_ROWS, seq), lambda b, h, i: (b, h, 0, 0)),
            pl.BlockSpec((None, 128, tq), lambda b, h, i: (b, h, i)),
            pl.BlockSpec((None, None, 2, kdim, tq), lambda b, h, i: (b, h, 0, 0, i)),
            pl.BlockSpec((None, None, 2, seq, kdim), lambda b, h, i: (b, h, 0, 0, 0)),
            smem, smem,
            pl.BlockSpec((DIFF_V_DIM, 1), lambda b, h, i: (0, 0)),
            vec, vec, vec, vec,
        ],
        out_specs=pl.BlockSpec((None, tq, 128), lambda b, h, i: (b, i, h)),
        out_shape=jax.ShapeDtypeStruct((bsz, seq, WIDTH_A), jnp.bfloat16),
        scratch_shapes=[
            pltpu.VMEM((TK_A, tq), jnp.float32),
            pltpu.VMEM((128, n), jnp.bfloat16),
            pltpu.VMEM((TK_A, n), jnp.float32),
            pltpu.VMEM((TK_A, n), jnp.bfloat16),
            pltpu.VMEM((TK_A, n), jnp.bfloat16),
            pltpu.VMEM((1, n), jnp.float32),
            pltpu.VMEM((DIFF_V_DIM + ONES_ROWS, n), jnp.float32),
        ],
        compiler_params=pltpu.CompilerParams(
            dimension_semantics=("arbitrary", "arbitrary", "arbitrary"),
            vmem_limit_bytes=VMEM_LIMIT_BYTES),
        name="attn_diff",
    )(qa_t, ka, va_t, ga_t, qa_f8, ka_f8, slopes, kmax, subln_col, lq1, lk1, lq2, lk2)


def _attn_gqa_kernel(qt_ref, k_ref, vt_ref, gt_ref, qf_ref, kf_ref, kmax_ref, o_ref,
                     qp_ref, qpf_ref, s_ref, p0_ref, p1_ref, m_ref, acc_ref):
    tq, tk = TQ_B, TK_B
    g = pl.program_id(1)
    n_kv = k_ref.shape[0] // tk
    n = GQA_GROUP * tq

    row = lax.broadcasted_iota(jnp.int32, (128, tq), 0)
    lo = g * HEAD_DIM_B
    mine = (row >= lo) & (row < lo + HEAD_DIM_B)
    for r in range(GQA_GROUP):
        q = qt_ref[r * 64:(r + 1) * 64, :].astype(jnp.float32)
        q2 = jnp.concatenate([q, q], axis=0)
        qp_ref[:, r * tq:(r + 1) * tq] = jnp.where(mine, q2, 0.0).astype(jnp.bfloat16)
        qpf_ref[:, r * tq:(r + 1) * tq] = qf_ref[r]

    whole = slice(0, tk)

    def key_rows(j, rows):
        return pl.ds(pl.multiple_of(j * tk + rows.start, KEY_ALIGN), rows.stop - rows.start)

    def logits(j, cols):
        return jnp.dot(k_ref[key_rows(j, whole), :], qp_ref[:, cols], preferred_element_type=jnp.float32)

    def fast_logits(j, rows, cols):
        return jnp.dot(kf_ref[key_rows(j, rows), :], qpf_ref[:, cols], preferred_element_type=jnp.float32)

    def first_logits(cols, fast):
        return fast_logits(jnp.int32(0), whole, cols) if fast else logits(jnp.int32(0), cols)

    def values(j, rows=whole):
        return vt_ref[:, key_rows(j, rows)]

    no_offset = jnp.zeros((1, n), jnp.float32)
    q = qp_ref[...].astype(jnp.float32)
    q_norm = jnp.sqrt(jnp.sum(q * q, axis=0, keepdims=True))
    bound = q_norm * kmax_ref[0]
    f8_ok = ((jnp.max(q_norm) * (QK_SCALE_ROOT / QK_SCALE) < F8_SAFE)
             & (kmax_ref[0] * QK_SCALE_ROOT < F8_SAFE))

    _softmax_over_tiles(n_kv - 1, jnp.int32(0), lambda t: jnp.asarray(t, jnp.int32) + 1,
                        first_logits, logits, lambda j: no_offset, values,
                        fast_logits, lambda j: no_offset, values,
                        f8_ok, lambda m1: jnp.max(bound - m1) <= GUARD_LOG2,
                        s_ref, (p0_ref, p1_ref), m_ref, acc_ref)

    o = acc_ref[0:HEAD_DIM_B, :] * (1.0 / acc_ref[HEAD_DIM_B:HEAD_DIM_B + 1, :])
    o = jnp.concatenate([o[:, r * tq:(r + 1) * tq] for r in range(GQA_GROUP)], axis=0)
    gated = o * gt_ref[...].astype(jnp.float32)
    o_ref[...] = gated.T.astype(o_ref.dtype)


def _attn_gqa(qb_t, kb, vb_t, gb_t, qb_f8, kb_f8, kmax):
    bsz, seq, _ = kb.shape
    tq = TQ_B
    n = GQA_GROUP * tq
    kdim = SPLIT * HEAD_DIM_B
    return pl.pallas_call(
        _attn_gqa_kernel,
        grid=(bsz, N_KV_B, seq // tq),
        in_specs=[
            pl.BlockSpec((None, 256, tq), lambda b, g, i: (b, g, i)),
            pl.BlockSpec((None, seq, 128), lambda b, g, i: (b, 0, 0)),
            pl.BlockSpec((None, None, HEAD_DIM_B + ONES_ROWS, seq), lambda b, g, i: (b, g, 0, 0)),
            pl.BlockSpec((None, 256, tq), lambda b, g, i: (b, g, i)),
            pl.BlockSpec((None, GQA_GROUP, kdim, tq), lambda b, g, i: (b, g, 0, i)),
            pl.BlockSpec((None, None, seq, kdim), lambda b, g, i: (b, g, 0, 0)),
            pl.BlockSpec(memory_space=pltpu.SMEM),
        ],
        out_specs=pl.BlockSpec((None, tq, 256), lambda b, g, i: (b, i, g)),
        out_shape=jax.ShapeDtypeStruct((bsz, seq, WIDTH_B), jnp.bfloat16),
        scratch_shapes=[
            pltpu.VMEM((128, n), jnp.bfloat16),
            pltpu.VMEM((kdim, n), F8),
            pltpu.VMEM((TK_B, n), jnp.float32),
            pltpu.VMEM((TK_B, n), jnp.bfloat16),
            pltpu.VMEM((TK_B, n), jnp.bfloat16),
            pltpu.VMEM((1, n), jnp.float32),
            pltpu.VMEM((HEAD_DIM_B + ONES_ROWS, n), jnp.float32),
        ],
        compiler_params=pltpu.CompilerParams(
            dimension_semantics=("arbitrary", "arbitrary", "arbitrary"),
            vmem_limit_bytes=VMEM_LIMIT_BYTES),
        name="attn_gqa",
    )(qb_t, kb, vb_t, gb_t, qb_f8, kb_f8, kmax)


def _out_proj_kernel(ga_ref, gb_ref, wa_ref, wb_ref, x_ref, mod_ref, fw_ref, o_ref):
    for r in range(0, x_ref.shape[0], OUT_ROW_CHUNK):
        rows = slice(r, r + OUT_ROW_CHUNK)
        y = jnp.dot(ga_ref[rows, :], wa_ref[...], preferred_element_type=jnp.float32)
        y = y + jnp.dot(gb_ref[rows, :], wb_ref[...], preferred_element_type=jnp.float32)
        z = x_ref[rows, :] + mod_ref[2:3, :] * y
        ms = jnp.mean(z * z, axis=1, keepdims=True)
        o_ref[rows, :] = z * lax.rsqrt(ms + NORM_EPS) * fw_ref[...]


def _out_proj(ga, gb, w_a, w_b, x, mod3, final_w):
    bsz, seq, d = x.shape
    ts = PROJ_TILE
    const = lambda b, i: (0, 0)
    return pl.pallas_call(
        _out_proj_kernel,
        grid=(bsz, seq // ts),
        in_specs=[
            pl.BlockSpec((None, ts, WIDTH_A), lambda b, i: (b, i, 0)),
            pl.BlockSpec((None, ts, WIDTH_B), lambda b, i: (b, i, 0)),
            pl.BlockSpec(w_a.shape, const),
            pl.BlockSpec(w_b.shape, const),
            pl.BlockSpec((None, ts, d), lambda b, i: (b, i, 0)),
            pl.BlockSpec((None, 3, d), lambda b, i: (b, 0, 0)),
            pl.BlockSpec((1, d), const),
        ],
        out_specs=pl.BlockSpec((None, ts, d), lambda b, i: (b, i, 0)),
        out_shape=jax.ShapeDtypeStruct((bsz, seq, d), jnp.float32),
        compiler_params=pltpu.CompilerParams(
            dimension_semantics=("arbitrary", "arbitrary"), vmem_limit_bytes=VMEM_LIMIT_BYTES),
        name="out_proj",
    )(ga, gb, w_a, w_b, x, mod3, final_w)


def _rope_tables(seq):
    pos = np.arange(seq)
    row = (pos // GRID_W).astype(np.float64)
    col = (pos % GRID_W).astype(np.float64)
    n_freq = ROT_HALF // 2
    freqs = 1.0 / (ROPE_THETA ** (np.arange(n_freq, dtype=np.float64) * 2.0 / ROT_HALF))
    ang_r = row[:, None] * freqs[None, :]
    ang_c = col[:, None] * freqs[None, :]
    cos = np.concatenate([np.cos(ang_r), np.cos(ang_r), np.cos(ang_c), np.cos(ang_c)], axis=1)
    sin = np.concatenate([-np.sin(ang_r), np.sin(ang_r), -np.sin(ang_c), np.sin(ang_c)], axis=1)
    d = np.arange(HEAD_DIM_B)
    partner = np.where((d // n_freq) % 2 == 0, d + n_freq, d - n_freq)
    return cos.astype(np.float32), sin.astype(np.float32), partner


def kernel(x, c, w_ada, b_ada, norm_w, w_in, lambda_q1, lambda_k1, lambda_q2, lambda_k2,
           subln_w, q_norm_w, k_norm_w, w_out, final_norm_w):
    assert w_ada.shape[0] == 1, "single-layer problem: the final norm is fused into the output projection"
    bsz, seq, d = x.shape
    assert seq % TQ_A == 0 and seq % TK_B == 0 and seq % PROJ_TILE == 0
    assert (seq // TK_A) % 2 == 0 and (seq // TK_B) % 2 == 0
    bf = jnp.bfloat16
    kdim = SPLIT * DIFF_HEAD_DIM
    cos, sin, partner = _rope_tables(seq)
    cosq, sinq = (jnp.asarray(np.ascontiguousarray(t.T)) for t in (cos, sin))
    cosk, sink = (jnp.asarray(np.tile(t, (1, N_KV_B))) for t in (cos, sin))
    slopes = jnp.asarray(2.0 ** (-8.0 * np.arange(1, N_HEADS_A + 1) / N_HEADS_A), jnp.float32)
    lam_init = 0.8 - 0.6 * math.exp(-0.3 * 0)
    grp = jnp.asarray(np.arange(WIDTH_A)[:, None] // DIFF_HEAD_DIM == np.arange(128)[None, :], jnp.float32)

    w = w_in[0]
    q_a, k_a, v_a, g_a = w[:, 0:512], w[:, 512:1024], w[:, 1024:1536], w[:, 1536:2048]
    q_b, k_b, v_b, g_b = w[:, 2048:2560], w[:, 2560:2688], w[:, 2688:2816], w[:, 2816:3328]
    partner2 = np.concatenate([partner, partner + HEAD_DIM_B])
    w_tok = jnp.concatenate([k_a, k_b, k_b[:, partner2]], axis=1).astype(bf)
    w_feat_t = jnp.concatenate([q_a, v_a, g_a, q_b, v_b, g_b], axis=1).T.astype(bf)
    knw = jnp.tile(k_norm_w[0], N_KV_B)
    wo = w_out[0].astype(bf)

    mod3 = _adaln_mod(c, w_ada[0], b_ada[0]).reshape(bsz, 3, d)
    ka, kb, qa_t, va_t, ga_t, qb_t, vb_t, gb_t, kn, ka_f8, kb_f8, qa_f8, qb_f8 = _in_proj(
        x, mod3, norm_w[0].reshape(1, d), w_tok, w_feat_t,
        q_norm_w[0].reshape(HEAD_DIM_B, 1), knw.reshape(1, 128), knw[partner2].reshape(1, 128),
        cosq, sinq, cosk, sink, grp)
    kmax_a = (jnp.sqrt(jnp.max(kn[:, :, 0, :2 * N_HEADS_A], axis=1)) * NORM_MARGIN).reshape(-1)
    kmax_b = (math.sqrt(HEAD_DIM_B) * NORM_MARGIN * jnp.max(jnp.abs(k_norm_w[0]))).reshape(1)
    oa = _attn_diff(qa_t, ka, va_t, ga_t,
                    qa_f8.reshape(bsz, N_HEADS_A, 2, kdim, seq), ka_f8.reshape(bsz, N_HEADS_A, 2, seq, kdim),
                    slopes, kmax_a, subln_w[0].reshape(DIFF_V_DIM, 1),
                    lambda_q1[0].reshape(1, -1), lambda_k1[0].reshape(1, -1),
                    lambda_q2[0].reshape(1, -1), lambda_k2[0].reshape(1, -1), lam_init)
    ob = _attn_gqa(qb_t, kb, vb_t, gb_t, qb_f8, kb_f8, kmax_b)
    return _out_proj(oa, ob, wo[:WIDTH_A], wo[WIDTH_A:], x, mod3, final_norm_w.reshape(1, d))
```

```python
import functools
import math

import numpy as np
import jax
import jax.numpy as jnp
from jax import lax
from jax.experimental import pallas as pl
from jax.experimental.pallas import tpu as pltpu

D_MODEL = 1024
N_HEADS_A = 4
DIFF_HEAD_DIM = 64
DIFF_V_DIM = 128
WIDTH_A = 512
N_KV_B = 2
GQA_GROUP = 4
HEAD_DIM_B = 64
WIDTH_B = 512
GRID_W = 64
ROPE_THETA = 10000.0
ROT_HALF = 32
NORM_EPS = 1e-6
MXU_TILE = 256
LANE_CHUNK = MXU_TILE
KEY_CHUNK = MXU_TILE
KEY_ALIGN = MXU_TILE
ONES_ROWS = 16
LOG2E = 1.4426950408889634
NEG_BIG = -1e30
GUARD_LOG2 = 100.0
NORM_MARGIN = 1.01
F8 = jnp.float8_e4m3fn
F8_SAFE = 440.0
SPLIT = 4
QK_SCALE = LOG2E / math.sqrt(DIFF_HEAD_DIM)
QK_SCALE_ROOT = math.sqrt(QK_SCALE)

VMEM_LIMIT_BYTES = 56 * 1024 * 1024

PROJ_TILE = 1024
OUT_ROW_CHUNK = 256
TQ_A = 512
TK_A = 512
TQ_B = 512
TK_B = 512

assert DIFF_HEAD_DIM == HEAD_DIM_B and TQ_A == TK_A
assert TQ_A % LANE_CHUNK == 0 and TQ_B % LANE_CHUNK == 0 and TK_A % KEY_CHUNK == 0 and TK_B % KEY_CHUNK == 0


def _silu(v):
    return v * (1.0 / (1.0 + jnp.exp(-v)))


def _split_f8(v):
    hi = v.astype(F8)
    lo = (v - hi.astype(jnp.float32)).astype(F8)
    return hi, lo


def _mod_kernel(ct_ref, w_ref, b_ref, o_ref):
    s = _silu(ct_ref[...])
    w = w_ref[...]
    for b in range(ct_ref.shape[1]):
        o_ref[b:b + 1, :] = jnp.sum(w * s[:, b:b + 1], axis=0, keepdims=True) + b_ref[...]


def _adaln_mod(c, w_ada, b_ada):
    bsz, d = c.shape
    n = w_ada.shape[1]
    tn = 1024
    return pl.pallas_call(
        _mod_kernel,
        grid=(n // tn,),
        in_specs=[pl.BlockSpec((d, bsz), lambda j: (0, 0)),
                  pl.BlockSpec((d, tn), lambda j: (0, j)),
                  pl.BlockSpec((1, tn), lambda j: (0, j))],
        out_specs=pl.BlockSpec((bsz, tn), lambda j: (0, j)),
        out_shape=jax.ShapeDtypeStruct((bsz, n), jnp.float32),
        name="adaln_mod",
    )(c.T, w_ada, b_ada.reshape(1, n))


def _store_keys_f8(k2, out_ref, first):
    lane = lax.broadcasted_iota(jnp.int32, k2.shape, 1)
    swapped = pltpu.roll(k2, HEAD_DIM_B, axis=1)
    for g, dup in enumerate((jnp.where(lane < HEAD_DIM_B, k2, swapped),
                             jnp.where(lane < HEAD_DIM_B, swapped, k2))):
        hi, lo = _split_f8(dup)
        out_ref[first + g, :, 0:128] = hi
        out_ref[first + g, :, 128:256] = lo


def _store_queries_f8(q, out_ref, g):
    hi, lo = _split_f8(q)
    for part, val in enumerate((hi, lo, hi, lo)):
        out_ref[g, part * 64:(part + 1) * 64, :] = val


def _in_proj_kernel(x_ref, mod_ref, nw_ref, wtok_ref, wfeat_ref, qnw_ref, knw_ref, knws_ref,
                    cosq_ref, sinq_ref, cosk_ref, sink_ref, grp_ref,
                    ka_ref, kb_ref, qa_ref, va_ref, ga_ref, qb_ref, vb_ref, gb_ref, kn_ref,
                    kaf_ref, kbf_ref, qaf_ref, qbf_ref):
    x = x_ref[...]
    shift = mod_ref[0:1, :]
    scale = mod_ref[1:2, :]
    ms = jnp.mean(x * x, axis=1, keepdims=True)
    h = (x * lax.rsqrt(ms + NORM_EPS)) * nw_ref[...] * (1.0 + scale) + shift
    hb = h.astype(jnp.bfloat16)

    tok = jnp.dot(hb, wtok_ref[...], preferred_element_type=jnp.float32)
    ka = tok[:, 0:WIDTH_A]
    ka_ref[...] = ka.astype(jnp.bfloat16)
    for pair in range(N_HEADS_A):
        _store_keys_f8(ka[:, pair * 128:(pair + 1) * 128] * QK_SCALE_ROOT, kaf_ref, 2 * pair)
    kn = jnp.dot(ka * ka, grp_ref[...], preferred_element_type=jnp.float32)
    kn_ref[...] = jnp.max(kn, axis=0, keepdims=True)
    kb = tok[:, 512:640]
    kbs = tok[:, 640:768]
    sq = kb * kb
    r0 = lax.rsqrt(jnp.sum(sq[:, 0:64], axis=1, keepdims=True) * (1.0 / HEAD_DIM_B) + NORM_EPS)
    r1 = lax.rsqrt(jnp.sum(sq[:, 64:128], axis=1, keepdims=True) * (1.0 / HEAD_DIM_B) + NORM_EPS)
    lane = lax.broadcasted_iota(jnp.int32, kb.shape, 1)
    r = jnp.where(lane < HEAD_DIM_B, r0, r1)
    kr = r * ((kb * knw_ref[...]) * cosk_ref[...] + (kbs * knws_ref[...]) * sink_ref[...])
    kb_ref[...] = kr.astype(jnp.bfloat16)
    _store_keys_f8(kr * QK_SCALE_ROOT, kbf_ref, 0)

    def feat(lo, hi):
        return lax.dot_general(wfeat_ref[lo:hi, :], hb, (((1,), (1,)), ((), ())),
                               preferred_element_type=jnp.float32)

    qa = feat(0, 512)
    qa_ref[...] = (qa * QK_SCALE).astype(jnp.bfloat16)
    for g in range(2 * N_HEADS_A):
        _store_queries_f8(qa[g * 64:(g + 1) * 64, :] * QK_SCALE_ROOT, qaf_ref, g)
    va = feat(512, 1024).astype(jnp.bfloat16)
    ones = jnp.ones((ONES_ROWS, va.shape[1]), jnp.bfloat16)
    for hd in range(N_HEADS_A):
        va_ref[hd, 0:DIFF_V_DIM, :] = va[hd * DIFF_V_DIM:(hd + 1) * DIFF_V_DIM, :]
        va_ref[hd, DIFF_V_DIM:DIFF_V_DIM + ONES_ROWS, :] = ones
    ga_ref[...] = _silu(feat(1024, 1536)).astype(jnp.bfloat16)
    qb = feat(1536, 2048)
    cq = cosq_ref[...]
    sq_ = sinq_ref[...]
    qnw = qnw_ref[...]
    for hd in range(N_KV_B * GQA_GROUP):
        q = qb[hd * 64:(hd + 1) * 64, :]
        rq = lax.rsqrt(jnp.mean(q * q, axis=0, keepdims=True) + NORM_EPS)
        qn = q * rq * qnw
        partner = jnp.concatenate([qn[16:32], qn[0:16], qn[48:64], qn[32:48]], axis=0)
        rot = qn * cq + partner * sq_
        qb_ref[hd * 64:(hd + 1) * 64, :] = (rot * QK_SCALE).astype(jnp.bfloat16)
        _store_queries_f8(rot * QK_SCALE_ROOT, qbf_ref, hd)
    vb = feat(2048, 2176).astype(jnp.bfloat16)
    for g in range(N_KV_B):
        vb_ref[g, 0:HEAD_DIM_B, :] = vb[g * HEAD_DIM_B:(g + 1) * HEAD_DIM_B, :]
        vb_ref[g, HEAD_DIM_B:HEAD_DIM_B + ONES_ROWS, :] = ones
    gb_ref[...] = _silu(feat(2176, 2688)).astype(jnp.bfloat16)


def _in_proj(x, mod3, norm_w, w_tok, w_feat_t, qnw_col, knw_row, knws_row, cosq, sinq, cosk, sink,
             grp):
    bsz, seq, d = x.shape
    ts = PROJ_TILE
    bf = jnp.bfloat16
    kdim = SPLIT * DIFF_HEAD_DIM
    const = lambda b, i: (0, 0)
    feat_spec = lambda rows: pl.BlockSpec((None, rows, ts), lambda b, i: (b, 0, i))
    return pl.pallas_call(
        _in_proj_kernel,
        grid=(bsz, seq // ts),
        in_specs=[
            pl.BlockSpec((None, ts, d), lambda b, i: (b, i, 0)),
            pl.BlockSpec((None, 3, d), lambda b, i: (b, 0, 0)),
            pl.BlockSpec((1, d), const),
            pl.BlockSpec(w_tok.shape, const),
            pl.BlockSpec(w_feat_t.shape, const),
            pl.BlockSpec((HEAD_DIM_B, 1), const),
            pl.BlockSpec((1, 128), const),
            pl.BlockSpec((1, 128), const),
            pl.BlockSpec((HEAD_DIM_B, ts), lambda b, i: (0, i)),
            pl.BlockSpec((HEAD_DIM_B, ts), lambda b, i: (0, i)),
            pl.BlockSpec((ts, 128), lambda b, i: (i, 0)),
            pl.BlockSpec((ts, 128), lambda b, i: (i, 0)),
            pl.BlockSpec(grp.shape, const),
        ],
        out_specs=[
            pl.BlockSpec((None, ts, WIDTH_A), lambda b, i: (b, i, 0)),
            pl.BlockSpec((None, ts, 128), lambda b, i: (b, i, 0)),
            feat_spec(512),
            pl.BlockSpec((None, N_HEADS_A, DIFF_V_DIM + ONES_ROWS, ts), lambda b, i: (b, 0, 0, i)),
            feat_spec(512), feat_spec(512),
            pl.BlockSpec((None, N_KV_B, HEAD_DIM_B + ONES_ROWS, ts), lambda b, i: (b, 0, 0, i)),
            feat_spec(512),
            pl.BlockSpec((None, None, 1, 128), lambda b, i: (b, i, 0, 0)),
            pl.BlockSpec((None, 2 * N_HEADS_A, ts, kdim), lambda b, i: (b, 0, i, 0)),
            pl.BlockSpec((None, N_KV_B, ts, kdim), lambda b, i: (b, 0, i, 0)),
            pl.BlockSpec((None, 2 * N_HEADS_A, kdim, ts), lambda b, i: (b, 0, 0, i)),
            pl.BlockSpec((None, N_KV_B * GQA_GROUP, kdim, ts), lambda b, i: (b, 0, 0, i)),
        ],
        out_shape=[
            jax.ShapeDtypeStruct((bsz, seq, WIDTH_A), bf),
            jax.ShapeDtypeStruct((bsz, seq, 128), bf),
            jax.ShapeDtypeStruct((bsz, 512, seq), bf),
            jax.ShapeDtypeStruct((bsz, N_HEADS_A, DIFF_V_DIM + ONES_ROWS, seq), bf),
            jax.ShapeDtypeStruct((bsz, 512, seq), bf),
            jax.ShapeDtypeStruct((bsz, 512, seq), bf),
            jax.ShapeDtypeStruct((bsz, N_KV_B, HEAD_DIM_B + ONES_ROWS, seq), bf),
            jax.ShapeDtypeStruct((bsz, 512, seq), bf),
            jax.ShapeDtypeStruct((bsz, seq // ts, 1, 128), jnp.float32),
            jax.ShapeDtypeStruct((bsz, 2 * N_HEADS_A, seq, kdim), F8),
            jax.ShapeDtypeStruct((bsz, N_KV_B, seq, kdim), F8),
            jax.ShapeDtypeStruct((bsz, 2 * N_HEADS_A, kdim, seq), F8),
            jax.ShapeDtypeStruct((bsz, N_KV_B * GQA_GROUP, kdim, seq), F8),
        ],
        compiler_params=pltpu.CompilerParams(
            dimension_semantics=("arbitrary", "arbitrary"), vmem_limit_bytes=VMEM_LIMIT_BYTES),
        name="in_proj",
    )(x, mod3, norm_w, w_tok, w_feat_t, qnw_col, knw_row, knws_row, cosq, sinq, cosk, sink, grp)


def _softmax_over_tiles(n_rest, first_tile, tile_fn, first_logits_fn, logits_fn, off_fn, vt_fn,
                        fast_logits_fn, fast_off_fn, fast_vt_fn,
                        f8_ok, bound_ok_fn, s_ref, p_refs, m_ref, acc_ref):
    chunks = [slice(c, c + LANE_CHUNK) for c in range(0, m_ref.shape[1], LANE_CHUNK)]
    tk = s_ref.shape[0]
    pieces = [(slice(r, r + KEY_CHUNK), cols) for r in range(0, tk, KEY_CHUNK) for cols in chunks]
    acc_ref[...] = jnp.zeros(acc_ref.shape, jnp.float32)

    off_first = off_fn(first_tile)

    def first_scores(fast):
        for cols in chunks:
            u = first_logits_fn(cols, fast)
            s_ref[:, cols] = u
            m_ref[:, cols] = jnp.max(u, axis=0, keepdims=True) + off_first[:, cols]

    first_scores(True)

    @pl.when(jnp.logical_not(f8_ok))
    def _():
        first_scores(False)

    fixed_ok = bound_ok_fn(m_ref[...]) & f8_ok

    def first_probs(rows, cols):
        shift = m_ref[:, cols] - off_first[:, cols]
        return jnp.exp2((s_ref[rows, cols] - shift).astype(jnp.bfloat16))

    def probs(t, slot, rows, cols):
        j = tile_fn(t)
        shift = m_ref[:, cols] - fast_off_fn(j)[:, cols]
        p_refs[slot][rows, cols] = jnp.exp2(
            (fast_logits_fn(j, rows, cols) - shift).astype(jnp.bfloat16))

    def tile_values(t):
        if t < 0:
            return {rows.start: vt_fn(first_tile, rows) for rows, _ in pieces}
        return {rows.start: fast_vt_fn(tile_fn(t), rows) for rows, _ in pieces}

    def accumulate(slot, rows, cols, vts):
        acc_ref[:, cols] += jnp.dot(vts[rows.start], p_refs[slot][rows, cols],
                                    preferred_element_type=jnp.float32)

    @pl.when(fixed_ok)
    def _():
        for rows, cols in pieces:
            p_refs[1][rows, cols] = first_probs(rows, cols)
        for t in range(-1, n_rest - 1):
            vts = tile_values(t)
            for rows, cols in pieces:
                probs(t + 1, (t + 1) % 2, rows, cols)
                accumulate(t % 2, rows, cols, vts)
        vts = tile_values(n_rest - 1)
        for rows, cols in pieces:
            accumulate((n_rest - 1) % 2, rows, cols, vts)

    @pl.when(jnp.logical_not(fixed_ok))
    def _():
        vt = vt_fn(first_tile)
        for cols in chunks:
            acc_ref[:, cols] = jnp.dot(vt, first_probs(slice(0, tk), cols),
                                       preferred_element_type=jnp.float32)

        def body(t, carry):
            j = tile_fn(t)
            off = off_fn(j)
            vt = vt_fn(j)
            for cols in chunks:
                u = logits_fn(j, cols)
                m_old = m_ref[:, cols]
                m_new = jnp.maximum(m_old, jnp.max(u, axis=0, keepdims=True) + off[:, cols])
                alpha = jnp.exp2(m_old - m_new)
                p = jnp.exp2((u - (m_new - off[:, cols])).astype(jnp.bfloat16))
                acc_ref[:, cols] = alpha * acc_ref[:, cols] + jnp.dot(
                    vt, p, preferred_element_type=jnp.float32)
                m_ref[:, cols] = m_new
            return carry

        lax.fori_loop(0, n_rest, body, 0)


def _attn_diff_kernel(qt_ref, k_ref, vt_ref, gt_ref, qf_ref, kf_ref, slope_ref, kmax_ref,
                      sublnw_ref, lq1_ref, lk1_ref, lq2_ref, lk2_ref, o_ref,
                      bias_ref, qbd_ref, s_ref, p0_ref, p1_ref, m_ref, acc_ref, *, lam_init):
    tq, tk = TQ_A, TK_A
    b = pl.program_id(0)
    hd = pl.program_id(1)
    qi = pl.program_id(2)
    n_kv = k_ref.shape[0] // tk
    sigma = slope_ref[hd] * LOG2E

    @pl.when(qi == 0)
    def _():
        key = lax.broadcasted_iota(jnp.int32, (tk, tq), 0).astype(jnp.float32)
        query = lax.broadcasted_iota(jnp.int32, (tk, tq), 1).astype(jnp.float32)
        bias_ref[...] = sigma * (query - jnp.abs(query - key))

    zeros = jnp.zeros((DIFF_HEAD_DIM, tq), jnp.bfloat16)
    qbd_ref[0:64, 0:tq] = qt_ref[0:64, :]
    qbd_ref[64:128, 0:tq] = zeros
    qbd_ref[0:64, tq:2 * tq] = zeros
    qbd_ref[64:128, tq:2 * tq] = qt_ref[64:128, :]

    lane = lax.broadcasted_iota(jnp.int32, (1, 2 * tq), 1)
    il = jnp.where(lane >= tq, lane - tq, lane).astype(jnp.float32)

    whole = slice(0, tk)

    def key_rows(j, rows):
        return pl.ds(pl.multiple_of(j * tk + rows.start, KEY_ALIGN), rows.stop - rows.start)

    def scores(j, cols):
        return jnp.dot(k_ref[key_rows(j, whole), :], qbd_ref[:, cols], preferred_element_type=jnp.float32)

    def first_logits(cols, fast):
        queries = slice(cols.start % tq, cols.start % tq + cols.stop - cols.start)
        return (fast_logits(qi, whole, cols) if fast else scores(qi, cols)) + bias_ref[:, queries]

    def logits(j, cols):
        key = lax.broadcasted_iota(jnp.int32, (tk, LANE_CHUNK), 0).astype(jnp.float32)
        return scores(j, cols) + jnp.where(j < qi, sigma, -sigma) * key

    def offset(j):
        per_query = jnp.where(j > qi, 2.0 * sigma, 0.0)
        return per_query * il - sigma * (jnp.abs(j - qi) * tk).astype(jnp.float32)

    def values(j, rows=whole):
        return vt_ref[:, key_rows(j, rows)]

    def fast_logits(j, rows, cols):
        mp = cols.start // tq
        queries = slice(cols.start - mp * tq, cols.stop - mp * tq)
        return jnp.dot(kf_ref[mp, key_rows(j, rows), :], qf_ref[mp, :, queries],
                       preferred_element_type=jnp.float32)

    def fast_offset(j):
        return offset(j) + jnp.where(j < qi, sigma * tk, 0.0)

    jl = lax.broadcasted_iota(jnp.int32, (1, tk), 1).astype(jnp.float32)
    factor_left = jnp.exp2(sigma * (jl - tk))
    factor_right = jnp.exp2(-sigma * jl)

    def fast_values(j, rows):
        factor = jnp.where(j < qi, factor_left[:, rows], factor_right[:, rows])
        return (values(j, rows).astype(jnp.float32) * factor).astype(jnp.bfloat16)

    q = qbd_ref[...].astype(jnp.float32)
    q_norm = jnp.sqrt(jnp.sum(q * q, axis=0, keepdims=True))
    k_norm0 = kmax_ref[(b * N_HEADS_A + hd) * 2]
    k_norm1 = kmax_ref[(b * N_HEADS_A + hd) * 2 + 1]
    bound = q_norm * jnp.where(lane < tq, k_norm0, k_norm1) + sigma * il
    f8_ok = ((jnp.max(q_norm) * (QK_SCALE_ROOT / QK_SCALE) < F8_SAFE)
             & (jnp.maximum(k_norm0, k_norm1) * QK_SCALE_ROOT < F8_SAFE))

    _softmax_over_tiles(n_kv - 1, qi, lambda t: jnp.where(t >= qi, t + 1, t),
                        first_logits, logits, offset, values,
                        fast_logits, fast_offset, fast_values,
                        f8_ok, lambda m1: jnp.max(bound - m1) <= GUARD_LOG2,
                        s_ref, (p0_ref, p1_ref), m_ref, acc_ref)

    lam = (jnp.exp(jnp.sum(lq1_ref[...] * lk1_ref[...], axis=1, keepdims=True))
           - jnp.exp(jnp.sum(lq2_ref[...] * lk2_ref[...], axis=1, keepdims=True)) + lam_init)
    inv_l = 1.0 / acc_ref[DIFF_V_DIM:DIFF_V_DIM + 1, :]
    o1 = acc_ref[0:DIFF_V_DIM, 0:tq] * inv_l[:, 0:tq]
    o2 = acc_ref[0:DIFF_V_DIM, tq:2 * tq] * inv_l[:, tq:2 * tq]
    diff = o1 - lam * o2
    ms = jnp.mean(diff * diff, axis=0, keepdims=True)
    y = diff * lax.rsqrt(ms + NORM_EPS) * sublnw_ref[...] * (1.0 - lam_init)
    gated = y * gt_ref[...].astype(jnp.float32)
    o_ref[...] = gated.T.astype(o_ref.dtype)


def _attn_diff(qa_t, ka, va_t, ga_t, qa_f8, ka_f8, slopes, kmax, subln_col, lq1, lk1, lq2, lk2,
               lam_init):
    bsz, seq, _ = ka.shape
    tq = TQ_A
    n = 2 * tq
    kdim = SPLIT * DIFF_HEAD_DIM
    vec = pl.BlockSpec((1, DIFF_HEAD_DIM), lambda b, h, i: (0, 0))
    smem = pl.BlockSpec(memory_space=pltpu.SMEM)
    return pl.pallas_call(
        functools.partial(_attn_diff_kernel, lam_init=lam_init),
        grid=(bsz, N_HEADS_A, seq // tq),
        in_specs=[
            pl.BlockSpec((None, 128, tq), lambda b, h, i: (b, h, i)),
            pl.BlockSpec((None, seq, 128), lambda b, h, i: (b, 0, h)),
            pl.BlockSpec((None, None, DIFF_V_DIM + ONES_ROWS, seq), lambda b, h, i: (b, h, 0, 0)),
            pl.BlockSpec((None, 128, tq), lambda b, h, i: (b, h, i)),
            pl.BlockSpec((None, None, 2, kdim, tq), lambda b, h, i: (b, h, 0, 0, i)),
            pl.BlockSpec((None, None, 2, seq, kdim), lambda b, h, i: (b, h, 0, 0, 0)),
            smem, smem,
            pl.BlockSpec((DIFF_V_DIM, 1), lambda b, h, i: (0, 0)),
            vec, vec, vec, vec,
        ],
        out_specs=pl.BlockSpec((None, tq, 128), lambda b, h, i: (b, i, h)),
        out_shape=jax.ShapeDtypeStruct((bsz, seq, WIDTH_A), jnp.bfloat16),
        scratch_shapes=[
            pltpu.VMEM((TK_A, tq), jnp.float32),
            pltpu.VMEM((128, n), jnp.bfloat16),
            pltpu.VMEM((TK_A, n), jnp.float32),
            pltpu.VMEM((TK_A, n), jnp.bfloat16),
            pltpu.VMEM((TK_A, n), jnp.bfloat16),
            pltpu.VMEM((1, n), jnp.float32),
            pltpu.VMEM((DIFF_V_DIM + ONES_ROWS, n), jnp.float32),
        ],
        compiler_params=pltpu.CompilerParams(
            dimension_semantics=("arbitrary", "arbitrary", "arbitrary"),
            vmem_limit_bytes=VMEM_LIMIT_BYTES),
        name="attn_diff",
    )(qa_t, ka, va_t, ga_t, qa_f8, ka_f8, slopes, kmax, subln_col, lq1, lk1, lq2, lk2)


def _attn_gqa_kernel(qt_ref, k_ref, vt_ref, gt_ref, qf_ref, kf_ref, kmax_ref, o_ref,
                     qp_ref, qpf_ref, s_ref, p0_ref, p1_ref, m_ref, acc_ref):
    tq, tk = TQ_B, TK_B
    g = pl.program_id(1)
    n_kv = k_ref.shape[0] // tk
    n = GQA_GROUP * tq

    row = lax.broadcasted_iota(jnp.int32, (128, tq), 0)
    lo = g * HEAD_DIM_B
    mine = (row >= lo) & (row < lo + HEAD_DIM_B)
    for r in range(GQA_GROUP):
        q = qt_ref[r * 64:(r + 1) * 64, :].astype(jnp.float32)
        q2 = jnp.concatenate([q, q], axis=0)
        qp_ref[:, r * tq:(r + 1) * tq] = jnp.where(mine, q2, 0.0).astype(jnp.bfloat16)
        qpf_ref[:, r * tq:(r + 1) * tq] = qf_ref[r]

    whole = slice(0, tk)

    def key_rows(j, rows):
        return pl.ds(pl.multiple_of(j * tk + rows.start, KEY_ALIGN), rows.stop - rows.start)

    def logits(j, cols):
        return jnp.dot(k_ref[key_rows(j, whole), :], qp_ref[:, cols], preferred_element_type=jnp.float32)

    def fast_logits(j, rows, cols):
        return jnp.dot(kf_ref[key_rows(j, rows), :], qpf_ref[:, cols], preferred_element_type=jnp.float32)

    def first_logits(cols, fast):
        return fast_logits(jnp.int32(0), whole, cols) if fast else logits(jnp.int32(0), cols)

    def values(j, rows=whole):
        return vt_ref[:, key_rows(j, rows)]

    no_offset = jnp.zeros((1, n), jnp.float32)
    q = qp_ref[...].astype(jnp.float32)
    q_norm = jnp.sqrt(jnp.sum(q * q, axis=0, keepdims=True))
    bound = q_norm * kmax_ref[0]
    f8_ok = ((jnp.max(q_norm) * (QK_SCALE_ROOT / QK_SCALE) < F8_SAFE)
             & (kmax_ref[0] * QK_SCALE_ROOT < F8_SAFE))

    _softmax_over_tiles(n_kv - 1, jnp.int32(0), lambda t: jnp.asarray(t, jnp.int32) + 1,
                        first_logits, logits, lambda j: no_offset, values,
                        fast_logits, lambda j: no_offset, values,
                        f8_ok, lambda m1: jnp.max(bound - m1) <= GUARD_LOG2,
                        s_ref, (p0_ref, p1_ref), m_ref, acc_ref)

    o = acc_ref[0:HEAD_DIM_B, :] * (1.0 / acc_ref[HEAD_DIM_B:HEAD_DIM_B + 1, :])
    o = jnp.concatenate([o[:, r * tq:(r + 1) * tq] for r in range(GQA_GROUP)], axis=0)
    gated = o * gt_ref[...].astype(jnp.float32)
    o_ref[...] = gated.T.astype(o_ref.dtype)


def _attn_gqa(qb_t, kb, vb_t, gb_t, qb_f8, kb_f8, kmax):
    bsz, seq, _ = kb.shape
    tq = TQ_B
    n = GQA_GROUP * tq
    kdim = SPLIT * HEAD_DIM_B
    return pl.pallas_call(
        _attn_gqa_kernel,
        grid=(bsz, N_KV_B, seq // tq),
        in_specs=[
            pl.BlockSpec((None, 256, tq), lambda b, g, i: (b, g, i)),
            pl.BlockSpec((None, seq, 128), lambda b, g, i: (b, 0, 0)),
            pl.BlockSpec((None, None, HEAD_DIM_B + ONES_ROWS, seq), lambda b, g, i: (b, g, 0, 0)),
            pl.BlockSpec((None, 256, tq), lambda b, g, i: (b, g, i)),
            pl.BlockSpec((None, GQA_GROUP, kdim, tq), lambda b, g, i: (b, g, 0, i)),
            pl.BlockSpec((None, None, seq, kdim), lambda b, g, i: (b, g, 0, 0)),
            pl.BlockSpec(memory_space=pltpu.SMEM),
        ],
        out_specs=pl.BlockSpec((None, tq, 256), lambda b, g, i: (b, i, g)),
        out_shape=jax.ShapeDtypeStruct((bsz, seq, WIDTH_B), jnp.bfloat16),
        scratch_shapes=[
            pltpu.VMEM((128, n), jnp.bfloat16),
            pltpu.VMEM((kdim, n), F8),
            pltpu.VMEM((TK_B, n), jnp.float32),
            pltpu.VMEM((TK_B, n), jnp.bfloat16),
            pltpu.VMEM((TK_B, n), jnp.bfloat16),
            pltpu.VMEM((1, n), jnp.float32),
            pltpu.VMEM((HEAD_DIM_B + ONES_ROWS, n), jnp.float32),
        ],
        compiler_params=pltpu.CompilerParams(
            dimension_semantics=("arbitrary", "arbitrary", "arbitrary"),
            vmem_limit_bytes=VMEM_LIMIT_BYTES),
        name="attn_gqa",
    )(qb_t, kb, vb_t, gb_t, qb_f8, kb_f8, kmax)


def _out_proj_kernel(ga_ref, gb_ref, wa_ref, wb_ref, x_ref, mod_ref, fw_ref, o_ref):
    for r in range(0, x_ref.shape[0], OUT_ROW_CHUNK):
        rows = slice(r, r + OUT_ROW_CHUNK)
        y = jnp.dot(ga_ref[rows, :], wa_ref[...], preferred_element_type=jnp.float32)
        y = y + jnp.dot(gb_ref[rows, :], wb_ref[...], preferred_element_type=jnp.float32)
        z = x_ref[rows, :] + mod_ref[2:3, :] * y
        ms = jnp.mean(z * z, axis=1, keepdims=True)
        o_ref[rows, :] = z * lax.rsqrt(ms + NORM_EPS) * fw_ref[...]


def _out_proj(ga, gb, w_a, w_b, x, mod3, final_w):
    bsz, seq, d = x.shape
    ts = PROJ_TILE
    const = lambda b, i: (0, 0)
    return pl.pallas_call(
        _out_proj_kernel,
        grid=(bsz, seq // ts),
        in_specs=[
            pl.BlockSpec((None, ts, WIDTH_A), lambda b, i: (b, i, 0)),
            pl.BlockSpec((None, ts, WIDTH_B), lambda b, i: (b, i, 0)),
            pl.BlockSpec(w_a.shape, const),
            pl.BlockSpec(w_b.shape, const),
            pl.BlockSpec((None, ts, d), lambda b, i: (b, i, 0)),
            pl.BlockSpec((None, 3, d), lambda b, i: (b, 0, 0)),
            pl.BlockSpec((1, d), const),
        ],
        out_specs=pl.BlockSpec((None, ts, d), lambda b, i: (b, i, 0)),
        out_shape=jax.ShapeDtypeStruct((bsz, seq, d), jnp.float32),
        compiler_params=pltpu.CompilerParams(
            dimension_semantics=("arbitrary", "arbitrary"), vmem_limit_bytes=VMEM_LIMIT_BYTES),
        name="out_proj",
    )(ga, gb, w_a, w_b, x, mod3, final_w)


def _rope_tables(seq):
    pos = np.arange(seq)
    row = (pos // GRID_W).astype(np.float64)
    col = (pos % GRID_W).astype(np.float64)
    n_freq = ROT_HALF // 2
    freqs = 1.0 / (ROPE_THETA ** (np.arange(n_freq, dtype=np.float64) * 2.0 / ROT_HALF))
    ang_r = row[:, None] * freqs[None, :]
    ang_c = col[:, None] * freqs[None, :]
    cos = np.concatenate([np.cos(ang_r), np.cos(ang_r), np.cos(ang_c), np.cos(ang_c)], axis=1)
    sin = np.concatenate([-np.sin(ang_r), np.sin(ang_r), -np.sin(ang_c), np.sin(ang_c)], axis=1)
    d = np.arange(HEAD_DIM_B)
    partner = np.where((d // n_freq) % 2 == 0, d + n_freq, d - n_freq)
    return cos.astype(np.float32), sin.astype(np.float32), partner


def kernel(x, c, w_ada, b_ada, norm_w, w_in, lambda_q1, lambda_k1, lambda_q2, lambda_k2,
           subln_w, q_norm_w, k_norm_w, w_out, final_norm_w):
    assert w_ada.shape[0] == 1, "single-layer problem: the final norm is fused into the output projection"
    bsz, seq, d = x.shape
    assert seq % TQ_A == 0 and seq % TK_B == 0 and seq % PROJ_TILE == 0
    assert (seq // TK_A) % 2 == 0 and (seq // TK_B) % 2 == 0
    bf = jnp.bfloat16
    kdim = SPLIT * DIFF_HEAD_DIM
    cos, sin, partner = _rope_tables(seq)
    cosq, sinq = (jnp.asarray(np.ascontiguousarray(t.T)) for t in (cos, sin))
    cosk, sink = (jnp.asarray(np.tile(t, (1, N_KV_B))) for t in (cos, sin))
    slopes = jnp.asarray(2.0 ** (-8.0 * np.arange(1, N_HEADS_A + 1) / N_HEADS_A), jnp.float32)
    lam_init = 0.8 - 0.6 * math.exp(-0.3 * 0)
    grp = jnp.asarray(np.arange(WIDTH_A)[:, None] // DIFF_HEAD_DIM == np.arange(128)[None, :], jnp.float32)

    w = w_in[0]
    q_a, k_a, v_a, g_a = w[:, 0:512], w[:, 512:1024], w[:, 1024:1536], w[:, 1536:2048]
    q_b, k_b, v_b, g_b = w[:, 2048:2560], w[:, 2560:2688], w[:, 2688:2816], w[:, 2816:3328]
    partner2 = np.concatenate([partner, partner + HEAD_DIM_B])
    w_tok = jnp.concatenate([k_a, k_b, k_b[:, partner2]], axis=1).astype(bf)
    w_feat_t = jnp.concatenate([q_a, v_a, g_a, q_b, v_b, g_b], axis=1).T.astype(bf)
    knw = jnp.tile(k_norm_w[0], N_KV_B)
    wo = w_out[0].astype(bf)

    mod3 = _adaln_mod(c, w_ada[0], b_ada[0]).reshape(bsz, 3, d)
    ka, kb, qa_t, va_t, ga_t, qb_t, vb_t, gb_t, kn, ka_f8, kb_f8, qa_f8, qb_f8 = _in_proj(
        x, mod3, norm_w[0].reshape(1, d), w_tok, w_feat_t,
        q_norm_w[0].reshape(HEAD_DIM_B, 1), knw.reshape(1, 128), knw[partner2].reshape(1, 128),
        cosq, sinq, cosk, sink, grp)
    kmax_a = (jnp.sqrt(jnp.max(kn[:, :, 0, :2 * N_HEADS_A], axis=1)) * NORM_MARGIN).reshape(-1)
    kmax_b = (math.sqrt(HEAD_DIM_B) * NORM_MARGIN * jnp.max(jnp.abs(k_norm_w[0]))).reshape(1)
    oa = _attn_diff(qa_t, ka, va_t, ga_t,
                    qa_f8.reshape(bsz, N_HEADS_A, 2, kdim, seq), ka_f8.reshape(bsz, N_HEADS_A, 2, seq, kdim),
                    slopes, kmax_a, subln_w[0].reshape(DIFF_V_DIM, 1),
                    lambda_q1[0].reshape(1, -1), lambda_k1[0].reshape(1, -1),
                    lambda_q2[0].reshape(1, -1), lambda_k2[0].reshape(1, -1), lam_init)
    ob = _attn_gqa(qb_t, kb, vb_t, gb_t, qb_f8, kb_f8, kmax_b)
    return _out_proj(oa, ob, wo[:WIDTH_A], wo[WIDTH_A:], x, mod3, final_norm_w.reshape(1, d))
```
